```python
import math
import jax
import jax.numpy as jnp
from jax import lax
import numpy as np

D_MODEL = 1024
BATCH = 2
SEQ = 16384
DEPTH = 4

GRID_W = 64
CTX_LEN = 256
N_MIXERS = 3
N_A = (DEPTH + 2) // 3
N_B = (DEPTH + 1) // 3
N_C = DEPTH // 3
D_FF = 2816
N_MOD = 9
ATTN_HEADS = 8
ATTN_SUB_DIM = D_MODEL // ATTN_HEADS // 2
ATTN_V_DIM = 2 * ATTN_SUB_DIM
Q_BLOCK = 128
ROPE_THETA = 10000.0
HY_EMB_DIM = 33
HY_BANDS = (HY_EMB_DIM - 1) // 2
HY_FILTER_HIDDEN = 64
HY_SHORT_W = 3
HY_FAST_DECAY = 0.3
HY_SLOW_DECAY = 1.5
HY_DECAY_TARGET = 1e-2
CV_WIDTH = 31
EPS = 1e-6
LN_EPS = 1e-5

kernel_name = 'hybrid_diffattn_hyena_conformer_dit'


def rmsnorm(x, g):
    x32 = x.astype(jnp.float32)
    y = x32 * lax.rsqrt(jnp.mean(x32 * x32, axis=-1, keepdims=True) + EPS)
    return y.astype(x.dtype) * g


def layernorm(x, g, b):
    x32 = x.astype(jnp.float32)
    mu = jnp.mean(x32, axis=-1, keepdims=True)
    var = jnp.mean(jnp.square(x32 - mu), axis=-1, keepdims=True)
    return ((x32 - mu) * lax.rsqrt(var + LN_EPS)).astype(x.dtype) * g + b


def modulate(x, g, shift, scale):
    return rmsnorm(x, g) * (1.0 + scale) + shift


def swiglu(h, w_in, w_out):
    gate, up = jnp.split(h @ w_in, 2, axis=-1)
    return (jax.nn.silu(gate) * up) @ w_out


def depthwise_conv(x, w, b):
    pad = (w.shape[0] - 1) // 2
    y = lax.conv_general_dilated(x, w[:, None, :].astype(x.dtype), window_strides=(1,),
                                 padding=[(pad, pad)], dimension_numbers=('NWC', 'WIO', 'NWC'),
                                 feature_group_count=x.shape[-1])
    return y + b


def axial_rope_tables(n):
    rows = n // GRID_W
    row = jnp.repeat(jnp.arange(rows), GRID_W).astype(jnp.float32)
    col = jnp.tile(jnp.arange(GRID_W), rows).astype(jnp.float32)
    half = ATTN_SUB_DIM // 2
    quarter = half // 2
    inv = ROPE_THETA ** (-(2.0 * jnp.arange(quarter, dtype=jnp.float32)) / half)
    ang = jnp.concatenate([row[:, None] * inv, col[:, None] * inv], axis=-1)
    return jnp.cos(ang), jnp.sin(ang)


def apply_axial_rope(t, cos, sin):
    q = t.shape[-1] // 4
    cs = cos[:, None, None, :]
    sn = sin[:, None, None, :]

    def rot(seg, c_, s_):
        lo, hi = seg[..., :q], seg[..., q:]
        return jnp.concatenate([lo * c_ - hi * s_, lo * s_ + hi * c_], axis=-1)

    return jnp.concatenate([rot(t[..., :2 * q], cs[..., :q], sn[..., :q]),
                            rot(t[..., 2 * q:], cs[..., q:], sn[..., q:])], axis=-1)


def diff_attention(h_lat, h_ctx, w_qkv, w_o, lam_vecs, subln_g, lam_init, need_ctx_out):
    bsz, n_lat, _ = h_lat.shape
    out_dtype = h_lat.dtype

    def project(h):
        n = h.shape[1]
        q, k, v = jnp.split((h @ w_qkv).astype(jnp.float32), 3, axis=-1)
        return (q.reshape(bsz, n, ATTN_HEADS, 2, ATTN_SUB_DIM),
                k.reshape(bsz, n, ATTN_HEADS, 2, ATTN_SUB_DIM),
                v.reshape(bsz, n, ATTN_HEADS, ATTN_V_DIM))

    lv = lam_vecs.astype(jnp.float32)
    lam = jnp.exp(jnp.sum(lv[0] * lv[1])) - jnp.exp(jnp.sum(lv[2] * lv[3])) + lam_init
    scale = ATTN_SUB_DIM ** -0.5

    def attend(q, k, v):
        s = jnp.einsum('bqhmd,bkhmd->bhmqk', q * scale, k)
        p = jax.nn.softmax(s, axis=-1)
        w = p[:, :, 0] - lam * p[:, :, 1]
        return jnp.einsum('bhqk,bkhe->bqhe', w, v)

    def merge(o):
        n = o.shape[1]
        o = rmsnorm(o, subln_g) * (1.0 - lam_init)
        return o.reshape(bsz, n, D_MODEL).astype(out_dtype) @ w_o

    q_l, k_l, v_l = project(h_lat)
    cos, sin = axial_rope_tables(n_lat)
    q_l = apply_axial_rope(q_l, cos, sin)
    k_l = apply_axial_rope(k_l, cos, sin)
    q_c, k_c, v_c = project(h_ctx)
    k_all = jnp.concatenate([k_l, k_c], axis=1)
    v_all = jnp.concatenate([v_l, v_c], axis=1)
    n_blk = n_lat // Q_BLOCK
    q_blk = jnp.moveaxis(q_l.reshape(bsz, n_blk, Q_BLOCK, ATTN_HEADS, 2, ATTN_SUB_DIM), 1, 0)
    o_blk = lax.map(lambda qb: attend(qb, k_all, v_all), q_blk)
    o_l = jnp.moveaxis(o_blk, 0, 1).reshape(bsz, n_lat, ATTN_HEADS, ATTN_V_DIM)
    y_lat = merge(o_l)
    y_ctx = merge(attend(q_c, k_c, v_c)) if need_ctx_out else None
    return y_lat, y_ctx


def hyena_filter(L, w1, b1, w2, b2, w3, b3, freq, w4):
    f32 = jnp.float32
    t = jnp.linspace(0.0, 1.0, L, dtype=f32)[:, None]
    wpos = (2.0 * math.pi / L) * jnp.arange(L, dtype=f32)
    bands = jnp.linspace(1e-4, HY_BANDS - 1, HY_BANDS, dtype=f32)
    fw = wpos[:, None] * bands[None, :]
    emb = jnp.concatenate([t, jnp.cos(fw), -jnp.sin(fw)], axis=-1)
    fr = freq.astype(f32)
    hdn = jnp.sin(fr * (emb @ w1.astype(f32) + b1.astype(f32)))
    hdn = jnp.sin(fr * (hdn @ w2.astype(f32) + b2.astype(f32)))
    hdn = jnp.sin(fr * (hdn @ w3.astype(f32) + b3.astype(f32)))
    h = hdn @ w4.astype(f32)
    max_decay = math.log(HY_DECAY_TARGET) / HY_FAST_DECAY
    min_decay = math.log(HY_DECAY_TARGET) / HY_SLOW_DECAY
    deltas = jnp.abs(jnp.linspace(min_decay, max_decay, D_MODEL, dtype=f32))
    decay = jnp.exp(-t * deltas[None, :])
    return h * jnp.concatenate([decay, decay], axis=-1)


def bidir_long_conv(v, h_fwd, h_bwd):
    L = v.shape[1]
    n = 2 * L
    taps = jnp.concatenate([h_fwd, jnp.zeros((1, h_fwd.shape[1]), jnp.float32), h_bwd[:0:-1]], axis=0)
    vf = jnp.fft.rfft(v.astype(jnp.float32), n=n, axis=1)
    hf = jnp.fft.rfft(taps, n=n, axis=0)
    return jnp.fft.irfft(vf * hf[None], n=n, axis=1)[:, :L]


def hyena_mixer(h, w_in, b_in, w_short, b_short, f_w1, f_b1, f_w2, f_b2, f_w3, f_b3, f_freq, f_w4,
                skip_bias, w_out, b_out):
    L = h.shape[1]
    u = depthwise_conv(h @ w_in + b_in, w_short, b_short)
    x0, x1, v = jnp.split(u, 3, axis=-1)
    h_fwd, h_bwd = jnp.split(hyena_filter(L, f_w1, f_b1, f_w2, f_b2, f_w3, f_b3, f_freq, f_w4), 2, axis=-1)
    v = v * x1
    y = bidir_long_conv(v, h_fwd, h_bwd).astype(v.dtype) + v * skip_bias
    return (y * x0) @ w_out + b_out


def conformer_conv(h, w_pw1, b_pw1, w_dw, b_dw, ln_g, ln_b, w_pw2, b_pw2):
    a, g = jnp.split(h @ w_pw1 + b_pw1, 2, axis=-1)
    u = a * jax.nn.sigmoid(g)
    u = depthwise_conv(u, w_dw, b_dw)
    u = jax.nn.silu(layernorm(u, ln_g, ln_b))
    return u @ w_pw2 + b_pw2


def setup_inputs(seed: int = 0) -> dict:
    key = jax.random.key(seed)
    ks = iter(jax.random.split(key, 48))
    f32 = jnp.float32

    def nrm(shape, scale):
        return jax.random.normal(next(ks), shape, f32) * scale

    D = D_MODEL
    return {
        'x': nrm((BATCH, SEQ, D), 1.0),
        'c': nrm((BATCH, D), 1.0),
        'ctx': nrm((BATCH, CTX_LEN, D), 1.0),
        'c_ctx': nrm((D,), 1.0),
        'w_mod': nrm((DEPTH, D, N_MOD * D), 0.5 * D ** -0.5),
        'b_mod': nrm((DEPTH, N_MOD * D), 0.01),
        'norm_g': 1.0 + nrm((DEPTH, 3, D), 0.01),
        'w_ffn_in': nrm((DEPTH, 2, D, 2 * D_FF), D ** -0.5),
        'w_ffn_out': nrm((DEPTH, 2, D_FF, D), D_FF ** -0.5),
        'attn_w_qkv': nrm((N_A, D, 3 * D), D ** -0.5),
        'attn_w_o': nrm((N_A, D, D), D ** -0.5),
        'attn_lambda': nrm((N_A, 4, ATTN_SUB_DIM), 0.1),
        'attn_subln_g': 1.0 + nrm((N_A, ATTN_V_DIM), 0.01),
        'hy_w_in': nrm((N_B, D, 3 * D), D ** -0.5),
        'hy_b_in': nrm((N_B, 3 * D), 0.01),
        'hy_w_short': nrm((N_B, HY_SHORT_W, 3 * D), HY_SHORT_W ** -0.5),
        'hy_b_short': nrm((N_B, 3 * D), 0.01),
        'hy_f_w1': nrm((N_B, HY_EMB_DIM, HY_FILTER_HIDDEN), HY_EMB_DIM ** -0.5),
        'hy_f_b1': nrm((N_B, HY_FILTER_HIDDEN), 0.1),
        'hy_f_w2': nrm((N_B, HY_FILTER_HIDDEN, HY_FILTER_HIDDEN), HY_FILTER_HIDDEN ** -0.5),
        'hy_f_b2': nrm((N_B, HY_FILTER_HIDDEN), 0.1),
        'hy_f_w3': nrm((N_B, HY_FILTER_HIDDEN, HY_FILTER_HIDDEN), HY_FILTER_HIDDEN ** -0.5),
        'hy_f_b3': nrm((N_B, HY_FILTER_HIDDEN), 0.1),
        'hy_f_freq': 1.0 + nrm((N_B, HY_FILTER_HIDDEN), 0.01),
        'hy_f_w4': nrm((N_B, HY_FILTER_HIDDEN, 2 * D), 0.01),
        'hy_skip': nrm((N_B, D), 1.0),
        'hy_w_out': nrm((N_B, D, D), D ** -0.5),
        'hy_b_out': nrm((N_B, D), 0.01),
        'cv_w_pw1': nrm((N_C, D, 2 * D), D ** -0.5),
        'cv_b_pw1': nrm((N_C, 2 * D), 0.01),
        'cv_w_dw': nrm((N_C, CV_WIDTH, D), CV_WIDTH ** -0.5),
        'cv_b_dw': nrm((N_C, D), 0.01),
        'cv_ln_g': 1.0 + nrm((N_C, D), 0.01),
        'cv_ln_b': nrm((N_C, D), 0.01),
        'cv_w_pw2': nrm((N_C, D, D), D ** -0.5),
        'cv_b_pw2': nrm((N_C, D), 0.01),
        'final_g': 1.0 + nrm((D,), 0.01),
    }


def reference(x, c, ctx, c_ctx, w_mod, b_mod, norm_g, w_ffn_in, w_ffn_out,
              attn_w_qkv, attn_w_o, attn_lambda, attn_subln_g,
              hy_w_in, hy_b_in, hy_w_short, hy_b_short, hy_f_w1, hy_f_b1, hy_f_w2, hy_f_b2,
              hy_f_w3, hy_f_b3, hy_f_freq, hy_f_w4, hy_skip, hy_w_out, hy_b_out,
              cv_w_pw1, cv_b_pw1, cv_w_dw, cv_b_dw, cv_ln_g, cv_ln_b, cv_w_pw2, cv_b_pw2,
              final_g):
    silu_c = jax.nn.silu(c)
    silu_cc = jax.nn.silu(c_ctx)
    xc = ctx
    for i in range(DEPTH):
        kind = i % N_MIXERS
        j = i // N_MIXERS
        last = i == DEPTH - 1
        ctx_in_use = (not last) or kind == 0
        ctx_advance = not last
        ml = jnp.split((silu_c @ w_mod[i] + b_mod[i])[:, None, :], N_MOD, axis=-1)
        mc = jnp.split(silu_cc @ w_mod[i] + b_mod[i], N_MOD, axis=-1)

        x = x + 0.5 * ml[2] * swiglu(modulate(x, norm_g[i, 0], ml[0], ml[1]), w_ffn_in[i, 0], w_ffn_out[i, 0])
        if ctx_in_use:
            xc = xc + 0.5 * mc[2] * swiglu(modulate(xc, norm_g[i, 0], mc[0], mc[1]), w_ffn_in[i, 0], w_ffn_out[i, 0])

        hl = modulate(x, norm_g[i, 1], ml[3], ml[4])
        hc = modulate(xc, norm_g[i, 1], mc[3], mc[4]) if ctx_in_use else None
        if kind == 0:
            lam_init = 0.8 - 0.6 * math.exp(-0.3 * i)
            yl, yc = diff_attention(hl, hc, attn_w_qkv[j], attn_w_o[j], attn_lambda[j], attn_subln_g[j],
                                    lam_init, ctx_advance)
        elif kind == 1:
            hy = (hy_w_in[j], hy_b_in[j], hy_w_short[j], hy_b_short[j], hy_f_w1[j], hy_f_b1[j],
                  hy_f_w2[j], hy_f_b2[j], hy_f_w3[j], hy_f_b3[j], hy_f_freq[j], hy_f_w4[j],
                  hy_skip[j], hy_w_out[j], hy_b_out[j])
            yl = hyena_mixer(hl, *hy)
            yc = hyena_mixer(hc, *hy) if ctx_advance else None
        else:
            cv = (cv_w_pw1[j], cv_b_pw1[j], cv_w_dw[j], cv_b_dw[j], cv_ln_g[j], cv_ln_b[j],
                  cv_w_pw2[j], cv_b_pw2[j])
            yl = conformer_conv(hl, *cv)
            yc = conformer_conv(hc, *cv) if ctx_advance else None
        x = x + ml[5] * yl

        x = x + 0.5 * ml[8] * swiglu(modulate(x, norm_g[i, 2], ml[6], ml[7]), w_ffn_in[i, 1], w_ffn_out[i, 1])
        if ctx_advance:
            xc = xc + mc[5] * yc
            xc = xc + 0.5 * mc[8] * swiglu(modulate(xc, norm_g[i, 2], mc[6], mc[7]), w_ffn_in[i, 1], w_ffn_out[i, 1])
    return rmsnorm(x, final_g)
```

```python
import functools
import math

import numpy as np
import jax
import jax.numpy as jnp
from jax import lax
from jax.experimental import pallas as pl
from jax.experimental.pallas import tpu as pltpu

F32 = jnp.float32
BF16 = jnp.bfloat16

GRID_W = 64
ATTN_HEADS = 8
ROPE_THETA = 10000.0
HY_FAST_DECAY = 0.3
HY_SLOW_DECAY = 1.5
HY_DECAY_TARGET = 1e-2
EPS = 1e-6
LN_EPS = 1e-5
N_MIXERS = 3
N_MOD = 9

LANES = 128
SUBLANES = 8
BF16_SUBLANES = 16
VMEM_LIMIT_BYTES = 56 * 1024 * 1024

MOD_ROWS = SUBLANES
HIGHEST = lax.Precision.HIGHEST


def _cparams(*sem):
    return pltpu.CompilerParams(dimension_semantics=sem, vmem_limit_bytes=VMEM_LIMIT_BYTES)


def _row_tile(n, want):
    t = min(n, want)
    assert n % t == 0, (n, t)
    return t


def _dot(a, b):
    return jnp.dot(a, b, preferred_element_type=F32)


def _dot32(a, b):
    return jnp.dot(a, b, preferred_element_type=F32, precision=HIGHEST)


def _rmsnorm(x, g):
    return x * lax.rsqrt(jnp.mean(x * x, axis=-1, keepdims=True) + EPS) * g


def _modulate(x, g, shift, scale):
    return _rmsnorm(x, g) * (1.0 + scale) + shift


def _silu(x):
    return x * jax.nn.sigmoid(x)


def _mod_kernel(r_ref, w_ref, b_ref, o_ref):
    r = _silu(r_ref[...]).astype(BF16)
    o_ref[0] = _dot(r, w_ref[0].astype(BF16)) + b_ref[0]


def mod_table(rows, w_mod, b_mod):
    depth, d, nm = w_mod.shape
    tn = nm // N_MOD
    return pl.pallas_call(
        _mod_kernel,
        grid=(depth, nm // tn),
        in_specs=[pl.BlockSpec((MOD_ROWS, d), lambda i, j: (0, 0)),
                  pl.BlockSpec((1, d, tn), lambda i, j: (i, 0, j)),
                  pl.BlockSpec((1, 1, tn), lambda i, j: (i, 0, j))],
        out_specs=pl.BlockSpec((1, MOD_ROWS, tn), lambda i, j: (i, 0, j)),
        out_shape=jax.ShapeDtypeStruct((depth, MOD_ROWS, nm), F32),
        compiler_params=_cparams("parallel", "parallel"),
        name="mod_table",
    )(rows, w_mod, b_mod.reshape(depth, 1, nm))


def _ffn_kernel(x_ref, mod_ref, g_ref, wg_ref, wu_ref, wo_ref, *rest, nf, final):
    if final:
        fg_ref, o_ref, h_scr, acc_scr = rest
    else:
        o_ref, h_scr, acc_scr = rest
    f = pl.program_id(2)

    @pl.when(f == 0)
    def _():
        h = _modulate(x_ref[0], g_ref[...], mod_ref[0, 0:1, :], mod_ref[0, 1:2, :])
        h_scr[...] = h.astype(BF16)
        acc_scr[...] = jnp.zeros_like(acc_scr)

    h = h_scr[...]
    gate = _dot(h, wg_ref[...])
    up = _dot(h, wu_ref[...])
    act = (_silu(gate) * up).astype(BF16)
    acc_scr[...] += _dot(act, wo_ref[...])

    @pl.when(f == nf - 1)
    def _():
        xn = x_ref[0] + 0.5 * mod_ref[0, 2:3, :] * acc_scr[...]
        if final:
            xn = _rmsnorm(xn, fg_ref[...])
        o_ref[0] = xn


def ffn(x, mod, g, w_in, w_out, final_g=None, tm=512):
    b, n, d = x.shape
    ff = w_out.shape[0]
    tm = _row_tile(n, tm)
    nf = 2
    tf = ff // nf
    assert tf % LANES == 0
    final = final_g is not None
    in_specs = [pl.BlockSpec((1, tm, d), lambda bi, i, f: (bi, i, 0)),
                pl.BlockSpec((1, MOD_ROWS, d), lambda bi, i, f: (bi, 0, 0)),
                pl.BlockSpec((1, d), lambda bi, i, f: (0, 0)),
                pl.BlockSpec((d, tf), lambda bi, i, f: (0, f)),
                pl.BlockSpec((d, tf), lambda bi, i, f: (0, nf + f)),
                pl.BlockSpec((tf, d), lambda bi, i, f: (f, 0))]
    args = [x, mod, g.reshape(1, d), w_in, w_in, w_out]
    if final:
        in_specs.append(pl.BlockSpec((1, d), lambda bi, i, f: (0, 0)))
        args.append(final_g.reshape(1, d))
    return pl.pallas_call(
        functools.partial(_ffn_kernel, nf=nf, final=final),
        grid=(b, n // tm, nf),
        in_specs=in_specs,
        out_specs=pl.BlockSpec((1, tm, d), lambda bi, i, f: (bi, i, 0)),
        out_shape=jax.ShapeDtypeStruct((b, n, d), F32),
        scratch_shapes=[pltpu.VMEM((tm, d), BF16), pltpu.VMEM((tm, d), F32)],
        compiler_params=_cparams("parallel", "parallel", "arbitrary"),
        name="ffn",
    )(*args)


def _rope_cols(y, cos, sa, sb):
    outs = []
    for c in range(y.shape[1] // LANES):
        yc = y[:, c * LANES:(c + 1) * LANES]
        outs.append(yc * cos + pltpu.roll(yc, LANES - 16, 1) * sa + pltpu.roll(yc, 16, 1) * sb)
    return jnp.concatenate(outs, axis=1)


def _modproj_kernel(*refs, has_bias, rope_blocks):
    x_ref, mod_ref, g_ref, w_ref = refs[:4]
    rest = refs[4:]
    if has_bias:
        b_ref, rest = rest[0], rest[1:]
    if rope_blocks:
        cos_ref, sa_ref, sb_ref, o_ref, h_scr = rest
    else:
        o_ref, h_scr = rest
    j = pl.program_id(2)

    @pl.when(j == 0)
    def _():
        h = _modulate(x_ref[0], g_ref[...], mod_ref[0, 0:1, :], mod_ref[0, 1:2, :])
        h_scr[...] = h.astype(BF16)

    y = _dot(h_scr[...], w_ref[...])
    if has_bias:
        y = y + b_ref[...]
    if rope_blocks:
        @pl.when(j < rope_blocks)
        def _():
            o_ref[0] = _rope_cols(y, cos_ref[...], sa_ref[...], sb_ref[...]).astype(o_ref.dtype)

        @pl.when(j >= rope_blocks)
        def _():
            o_ref[0] = y.astype(o_ref.dtype)
    else:
        o_ref[0] = y.astype(o_ref.dtype)


def modproj(x, mod, g, w, bias, out_dtype, rope=None, rope_blocks=0, tm=512, tn=1024):
    b, n, d = x.shape
    nout = w.shape[1]
    tm = _row_tile(n, tm)
    in_specs = [pl.BlockSpec((1, tm, d), lambda bi, i, j: (bi, i, 0)),
                pl.BlockSpec((1, MOD_ROWS, d), lambda bi, i, j: (bi, 0, 0)),
                pl.BlockSpec((1, d), lambda bi, i, j: (0, 0)),
                pl.BlockSpec((d, tn), lambda bi, i, j: (0, j))]
    args = [x, mod, g.reshape(1, d), w]
    if bias is not None:
        in_specs.append(pl.BlockSpec((1, tn), lambda bi, i, j: (0, j)))
        args.append(bias.reshape(1, nout))
    if rope_blocks:
        in_specs += [pl.BlockSpec((tm, LANES), lambda bi, i, j: (i, 0))] * 3
        args += list(rope)
    return pl.pallas_call(
        functools.partial(_modproj_kernel, has_bias=bias is not None, rope_blocks=rope_blocks),
        grid=(b, n // tm, nout // tn),
        in_specs=in_specs,
        out_specs=pl.BlockSpec((1, tm, tn), lambda bi, i, j: (bi, i, j)),
        out_shape=jax.ShapeDtypeStruct((b, n, nout), out_dtype),
        scratch_shapes=[pltpu.VMEM((tm, d), BF16)],
        compiler_params=_cparams("parallel", "parallel", "arbitrary"),
        name="modproj",
    )(*args)


def rope_tables(n):
    rows = n // GRID_W
    row = jnp.repeat(jnp.arange(rows), GRID_W).astype(F32)
    col = jnp.tile(jnp.arange(GRID_W), rows).astype(F32)
    quarter = 16
    half = 32
    inv = ROPE_THETA ** (-(2.0 * jnp.arange(quarter, dtype=F32)) / half)
    ang_r = row[:, None] * inv
    ang_c = col[:, None] * inv
    zero = jnp.zeros_like(ang_r)
    cr, sr, cc, sc = jnp.cos(ang_r), jnp.sin(ang_r), jnp.cos(ang_c), jnp.sin(ang_c)
    cos64 = jnp.concatenate([cr, cr, cc, cc], axis=1)
    sa64 = jnp.concatenate([-sr, zero, -sc, zero], axis=1)
    sb64 = jnp.concatenate([zero, sr, zero, sc], axis=1)
    tile2 = lambda t: jnp.concatenate([t, t], axis=1)
    return tile2(cos64), tile2(sa64), tile2(sb64)


V_ROWS = LANES + BF16_SUBLANES


def _attn_kernel(q_ref, k_ref, v_ref, lam_ref, g_ref, o_ref, m_scr, acc_scr, *, nkt, lam_init, half):
    q = q_ref[0].astype(F32) * (half ** -0.5)
    lane = lax.broadcasted_iota(jnp.int32, q.shape, 1)
    qm = [jnp.where(lane < half, q, 0.0).astype(BF16), jnp.where(lane >= half, q, 0.0).astype(BF16)]
    m_scr[...] = jnp.full(m_scr.shape, -jnp.inf, F32)
    acc_scr[...] = jnp.zeros_like(acc_scr)

    def body(t, carry):
        k = k_ref[0, 0, t]
        vt = v_ref[0, 0, t]
        for mi in range(2):
            s = lax.dot_general(k, qm[mi], (((1,), (1,)), ((), ())), preferred_element_type=F32)
            m_old = m_scr[mi]
            m_new = jnp.maximum(m_old, jnp.max(s, axis=0, keepdims=True))
            alpha = jnp.exp(m_old - m_new)
            p = jnp.exp((s - m_new).astype(BF16))
            acc_scr[mi] = alpha * acc_scr[mi] + _dot(vt, p)
            m_scr[mi] = m_new
        return carry

    lax.fori_loop(0, nkt, body, 0)

    lv = lam_ref[...]
    lam = (jnp.exp(jnp.sum(lv[0:1] * lv[1:2], axis=1, keepdims=True))
           - jnp.exp(jnp.sum(lv[2:3] * lv[3:4], axis=1, keepdims=True)) + lam_init)
    a1 = acc_scr[0]
    a2 = acc_scr[1]
    ot = a1[:LANES] / a1[LANES:LANES + 1] - lam * (a2[:LANES] / a2[LANES:LANES + 1])
    o = ot.T
    o = _rmsnorm(o, g_ref[...]) * (1.0 - lam_init)
    o_ref[0] = o.astype(o_ref.dtype)


def diff_attention(q, kt, vt, lam_vecs, subln_g, lam_init, tq=256):
    b, n, dq = q.shape
    _, h, nkt, tk, _ = kt.shape
    tq = _row_tile(n, tq)
    return pl.pallas_call(
        functools.partial(_attn_kernel, nkt=nkt, lam_init=lam_init, half=LANES // 2),
        grid=(b, h, n // tq),
        in_specs=[pl.BlockSpec((1, tq, LANES), lambda bi, hi, i: (bi, i, hi)),
                  pl.BlockSpec((1, 1, nkt, tk, LANES), lambda bi, hi, i: (bi, hi, 0, 0, 0)),
                  pl.BlockSpec((1, 1, nkt, V_ROWS, tk), lambda bi, hi, i: (bi, hi, 0, 0, 0)),
                  pl.BlockSpec((4, LANES // 2), lambda bi, hi, i: (0, 0)),
                  pl.BlockSpec((1, LANES), lambda bi, hi, i: (0, 0))],
        out_specs=pl.BlockSpec((1, tq, LANES), lambda bi, hi, i: (bi, i, hi)),
        out_shape=jax.ShapeDtypeStruct((b, n, dq), BF16),
        scratch_shapes=[pltpu.VMEM((2, 1, tq), F32), pltpu.VMEM((2, V_ROWS, tq), F32)],
        compiler_params=_cparams("parallel", "parallel", "arbitrary"),
        name="diff_attention",
    )(q, kt, vt, lam_vecs, subln_g.reshape(1, LANES))


def _key_value_tiles(k, v, tk):
    b, nk, _ = k.shape
    h = ATTN_HEADS
    nkt = nk // tk
    kt = k.reshape(b, nkt, tk, h, LANES).transpose(0, 3, 1, 2, 4)
    v5 = v.reshape(b, nkt, tk, h, LANES).transpose(0, 3, 1, 4, 2)
    ones = jnp.ones((b, h, nkt, 1, tk), BF16)
    zeros = jnp.zeros((b, h, nkt, V_ROWS - LANES - 1, tk), BF16)
    return kt, jnp.concatenate([v5, ones, zeros], axis=3)


def _resid_out(x_ref, mod_ref, y, o_ref):
    o_ref[0] = x_ref[0] + mod_ref[0, 2:3, :] * y


def _attn_out_kernel(a_ref, w_ref, x_ref, mod_ref, o_ref):
    _resid_out(x_ref, mod_ref, _dot(a_ref[0], w_ref[...]), o_ref)


def attn_out(a, w, x, mod, tm=512):
    b, n, d = x.shape
    tm = _row_tile(n, tm)
    row = pl.BlockSpec((1, tm, d), lambda bi, i: (bi, i, 0))
    return pl.pallas_call(
        _attn_out_kernel,
        grid=(b, n // tm),
        in_specs=[row, pl.BlockSpec((d, d), lambda bi, i: (0, 0)), row,
                  pl.BlockSpec((1, MOD_ROWS, d), lambda bi, i: (bi, 0, 0))],
        out_specs=row,
        out_shape=jax.ShapeDtypeStruct((b, n, d), F32),
        compiler_params=_cparams("parallel", "parallel"),
        name="attn_out",
    )(a, w, x, mod)


def _hyena_out_kernel(y_ref, vg_ref, x0_ref, skip_ref, w_ref, b_ref, x_ref, mod_ref, o_ref):
    a = ((y_ref[0] + vg_ref[0] * skip_ref[...]) * x0_ref[0]).astype(BF16)
    _resid_out(x_ref, mod_ref, _dot(a, w_ref[...]) + b_ref[...], o_ref)


def hyena_out(y, vg, x0, skip, w, bias, x, mod, tm=512):
    b, n, d = x.shape
    tm = _row_tile(n, tm)
    row = pl.BlockSpec((1, tm, d), lambda bi, i: (bi, i, 0))
    vec = pl.BlockSpec((1, d), lambda bi, i: (0, 0))
    return pl.pallas_call(
        _hyena_out_kernel,
        grid=(b, n // tm),
        in_specs=[row, row, row, vec, pl.BlockSpec((d, d), lambda bi, i: (0, 0)), vec, row,
                  pl.BlockSpec((1, MOD_ROWS, d), lambda bi, i: (bi, 0, 0))],
        out_specs=row,
        out_shape=jax.ShapeDtypeStruct((b, n, d), F32),
        compiler_params=_cparams("parallel", "parallel"),
        name="hyena_out",
    )(y, vg, x0, skip.reshape(1, d), w, bias.reshape(1, d), x, mod)


def _fill_window(ext_ref, prev_ref, cur_ref, next_ref, halo, tm, i, last):
    ext_ref[0:halo, :] = jnp.where(i == 0, 0.0, prev_ref[0])
    ext_ref[halo:halo + tm, :] = cur_ref[0]
    ext_ref[halo + tm:halo + tm + halo, :] = jnp.where(i == last, 0.0, next_ref[0])


def _dwconv(ext_ref, w_ref, halo, tm, c0, c1):
    width = w_ref.shape[0]
    pad = (width - 1) // 2
    acc = None
    for j in range(width):
        term = w_ref[j:j + 1, c0:c1] * ext_ref[pl.ds(halo - pad + j, tm), c0:c1]
        acc = term if acc is None else acc + term
    return acc


def _halo_specs(tm, halo, n, c):
    per = tm // halo
    nblk = n // halo
    prev = pl.BlockSpec((1, halo, c), lambda bi, i: (bi, jnp.maximum(i * per - 1, 0), 0))
    cur = pl.BlockSpec((1, tm, c), lambda bi, i: (bi, i, 0))
    nxt = pl.BlockSpec((1, halo, c), lambda bi, i: (bi, jnp.minimum((i + 1) * per, nblk - 1), 0))
    return [prev, cur, nxt]


def _hyena_gate_kernel(prev_ref, cur_ref, next_ref, w_ref, b_ref, x0_ref, vg_ref, ext_ref, *, halo, tm, d, last):
    i = pl.program_id(1)
    _fill_window(ext_ref, prev_ref, cur_ref, next_ref, halo, tm, i, last)
    x0_ref[0] = _dwconv(ext_ref, w_ref, halo, tm, 0, d) + b_ref[:, 0:d]
    x1 = _dwconv(ext_ref, w_ref, halo, tm, d, 2 * d) + b_ref[:, d:2 * d]
    v = _dwconv(ext_ref, w_ref, halo, tm, 2 * d, 3 * d) + b_ref[:, 2 * d:3 * d]
    vg_ref[0] = v * x1


def hyena_gate(u, w_short, b_short, tm=256):
    b, n, c = u.shape
    d = c // 3
    tm = _row_tile(n, tm)
    halo = SUBLANES
    row = pl.BlockSpec((1, tm, d), lambda bi, i: (bi, i, 0))
    out = jax.ShapeDtypeStruct((b, n, d), F32)
    return pl.pallas_call(
        functools.partial(_hyena_gate_kernel, halo=halo, tm=tm, d=d, last=n // tm - 1),
        grid=(b, n // tm),
        in_specs=_halo_specs(tm, halo, n, c) + [pl.BlockSpec(w_short.shape, lambda bi, i: (0, 0)),
                                                pl.BlockSpec((1, c), lambda bi, i: (0, 0))],
        out_specs=[row, row],
        out_shape=[out, out],
        scratch_shapes=[pltpu.VMEM((tm + 2 * halo, c), F32)],
        compiler_params=_cparams("parallel", "parallel"),
        name="hyena_gate",
    )(u, u, u, w_short, b_short.reshape(1, c))


def _conformer_out_kernel(prev_ref, cur_ref, next_ref, wdw_ref, bdw_ref, lg_ref, lb_ref, w_ref, b_ref,
                          x_ref, mod_ref, o_ref, ext_ref, *, halo, tm, d, last):
    i = pl.program_id(1)
    _fill_window(ext_ref, prev_ref, cur_ref, next_ref, halo, tm, i, last)
    u = _dwconv(ext_ref, wdw_ref, halo, tm, 0, d) + bdw_ref[...]
    mu = jnp.mean(u, axis=-1, keepdims=True)
    uc = u - mu
    var = jnp.mean(uc * uc, axis=-1, keepdims=True)
    z = _silu(uc * lax.rsqrt(var + LN_EPS) * lg_ref[...] + lb_ref[...]).astype(BF16)
    _resid_out(x_ref, mod_ref, _dot(z, w_ref[...]) + b_ref[...], o_ref)


def conformer_out(u, w_dw, b_dw, ln_g, ln_b, w, bias, x, mod, tm=256):
    b, n, d = x.shape
    tm = _row_tile(n, tm)
    halo = 2 * SUBLANES
    assert (w_dw.shape[0] - 1) // 2 <= halo
    row = pl.BlockSpec((1, tm, d), lambda bi, i: (bi, i, 0))
    vec = pl.BlockSpec((1, d), lambda bi, i: (0, 0))
    return pl.pallas_call(
        functools.partial(_conformer_out_kernel, halo=halo, tm=tm, d=d, last=n // tm - 1),
        grid=(b, n // tm),
        in_specs=_halo_specs(tm, halo, n, d) + [pl.BlockSpec(w_dw.shape, lambda bi, i: (0, 0)), vec, vec, vec,
                                                pl.BlockSpec((d, d), lambda bi, i: (0, 0)), vec, row,
                                                pl.BlockSpec((1, MOD_ROWS, d), lambda bi, i: (bi, 0, 0))],
        out_specs=row,
        out_shape=jax.ShapeDtypeStruct((b, n, d), F32),
        scratch_shapes=[pltpu.VMEM((tm + 2 * halo, d), F32)],
        compiler_params=_cparams("parallel", "parallel"),
        name="conformer_out",
    )(u, u, u, w_dw, b_dw.reshape(1, d), ln_g.reshape(1, d), ln_b.reshape(1, d), w, bias.reshape(1, d), x, mod)


def _modglu_kernel(x_ref, mod_ref, g_ref, wa_ref, wg_ref, ba_ref, bg_ref, o_ref):
    h = _modulate(x_ref[0], g_ref[...], mod_ref[0, 0:1, :], mod_ref[0, 1:2, :]).astype(BF16)
    a = _dot(h, wa_ref[...]) + ba_ref[...]
    gt = _dot(h, wg_ref[...]) + bg_ref[...]
    o_ref[0] = a * jax.nn.sigmoid(gt)


def modglu(x, mod, g, w, bias, tm=512):
    b, n, d = x.shape
    tm = _row_tile(n, tm)
    row = pl.BlockSpec((1, tm, d), lambda bi, i: (bi, i, 0))
    bias2 = bias.reshape(1, 2 * d)
    return pl.pallas_call(
        _modglu_kernel,
        grid=(b, n // tm),
        in_specs=[row, pl.BlockSpec((1, MOD_ROWS, d), lambda bi, i: (bi, 0, 0)),
                  pl.BlockSpec((1, d), lambda bi, i: (0, 0)),
                  pl.BlockSpec((d, d), lambda bi, i: (0, 0)), pl.BlockSpec((d, d), lambda bi, i: (0, 1)),
                  pl.BlockSpec((1, d), lambda bi, i: (0, 0)), pl.BlockSpec((1, d), lambda bi, i: (0, 1))],
        out_specs=row,
        out_shape=jax.ShapeDtypeStruct((b, n, d), F32),
        compiler_params=_cparams("parallel", "parallel"),
        name="modglu",
    )(x, mod, g.reshape(1, d), w, w, bias2, bias2)


def _filter_kernel(emb_ref, t_ref, w1_ref, b1_ref, w2_ref, b2_ref, w3_ref, b3_ref, fr_ref, w4_ref, dl_ref, o_ref):
    fr = fr_ref[...]
    hdn = jnp.sin(fr * (_dot32(emb_ref[...], w1_ref[...]) + b1_ref[...]))
    hdn = jnp.sin(fr * (_dot32(hdn, w2_ref[...]) + b2_ref[...]))
    hdn = jnp.sin(fr * (_dot32(hdn, w3_ref[...]) + b3_ref[...]))
    h = _dot32(hdn, w4_ref[...])
    o_ref[...] = h * jnp.exp(-t_ref[...] * dl_ref[...])


def hyena_filter(L, w1, b1, w2, b2, w3, b3, freq, w4, d_model, tl=256):
    bands = (w1.shape[0] - 1) // 2
    hid = w1.shape[1]
    t = jnp.linspace(0.0, 1.0, L, dtype=F32)[:, None]
    wpos = (2.0 * math.pi / L) * jnp.arange(L, dtype=F32)
    bnd = jnp.linspace(1e-4, bands - 1, bands, dtype=F32)
    fw = wpos[:, None] * bnd[None, :]
    emb = jnp.concatenate([t, jnp.cos(fw), -jnp.sin(fw)], axis=-1)
    max_decay = math.log(HY_DECAY_TARGET) / HY_FAST_DECAY
    min_decay = math.log(HY_DECAY_TARGET) / HY_SLOW_DECAY
    deltas = jnp.abs(jnp.linspace(min_decay, max_decay, d_model, dtype=F32))
    dl2 = jnp.concatenate([deltas, deltas])[None, :]
    tl = _row_tile(L, tl)
    td = d_model
    full = lambda a: pl.BlockSpec(a.shape, lambda i, j: (0, 0))
    vec = lambda a: a.reshape(1, -1)
    args = [emb, t, w1, vec(b1), w2, vec(b2), w3, vec(b3), vec(freq)]
    return pl.pallas_call(
        _filter_kernel,
        grid=(L // tl, 2 * d_model // td),
        in_specs=[pl.BlockSpec((tl, emb.shape[1]), lambda i, j: (i, 0)), pl.BlockSpec((tl, 1), lambda i, j: (i, 0))]
                 + [full(a) for a in args[2:]]
                 + [pl.BlockSpec((hid, td), lambda i, j: (0, j)), pl.BlockSpec((1, td), lambda i, j: (0, j))],
        out_specs=pl.BlockSpec((tl, td), lambda i, j: (i, j)),
        out_shape=jax.ShapeDtypeStruct((L, 2 * d_model), F32),
        compiler_params=_cparams("parallel", "parallel"),
        name="hyena_filter",
    )(*args, w4, dl2)


FFT_N2 = 128


def _leftmat_kernel(m_ref, x_ref, o_ref):
    o_ref[0] = _dot32(m_ref[...], x_ref[0])


def leftmat(mat, x, tc=2048):
    bx, r, cols = x.shape
    m = mat.shape[0]
    tc = _row_tile(cols, tc)
    return pl.pallas_call(
        _leftmat_kernel,
        grid=(bx, cols // tc),
        in_specs=[pl.BlockSpec((m, r), lambda bi, j: (0, 0)), pl.BlockSpec((1, r, tc), lambda bi, j: (bi, 0, j))],
        out_specs=pl.BlockSpec((1, m, tc), lambda bi, j: (bi, 0, j)),
        out_shape=jax.ShapeDtypeStruct((bx, m, cols), F32),
        compiler_params=_cparams("parallel", "parallel"),
        name="dft_major",
    )(mat, x)


def _twiddled(f_ref, tw_ref):
    fr, fi = f_ref[0], f_ref[1]
    twr, twi = tw_ref[0, 0:1, :], tw_ref[0, 1:2, :]
    return fr * twr - fi * twi, fr * twi + fi * twr


def _spec_fwd_kernel(f_ref, tw_ref, a_ref, o_ref):
    gr, gi = _twiddled(f_ref, tw_ref)
    ar, ai = a_ref[0, 0, 0], a_ref[0, 1, 0]
    o_ref[0, 0, 0] = _dot32(gr, ar) - _dot32(gi, ai)
    o_ref[0, 1, 0] = _dot32(gi, ar) + _dot32(gr, ai)


def _spec_mid_kernel(f_ref, tw_ref, a_ref, h_ref, o_ref):
    gr, gi = _twiddled(f_ref, tw_ref)
    ar, ai = a_ref[0, 0, 0], a_ref[0, 1, 0]
    xr = _dot32(gr, ar) - _dot32(gi, ai)
    xi = _dot32(gi, ar) + _dot32(gr, ai)
    hr, hi = h_ref[0, 0, 0], h_ref[0, 1, 0]
    zr = xr * hr - xi * hi
    zi = xr * hi + xi * hr
    grt, git = gr.T, gi.T
    o_ref[0, 0, 0] = _dot32(grt, zr) + _dot32(git, zi)
    o_ref[0, 1, 0] = _dot32(grt, zi) - _dot32(git, zr)


def spec_stage(fmat, tw, a, h=None, tc=1024):
    bx, _, n1, n2, c = a.shape
    tc = _row_tile(c, tc)
    blk = lambda: pl.BlockSpec((1, 2, 1, n2, tc), lambda k, j, bi: (bi, 0, k, 0, j))
    in_specs = [pl.BlockSpec((2, n2, n2), lambda k, j, bi: (0, 0, 0)),
                pl.BlockSpec((1, 2, n2), lambda k, j, bi: (k, 0, 0)), blk()]
    args = [fmat, tw, a]
    if h is not None:
        in_specs.append(pl.BlockSpec((1, 2, 1, n2, tc), lambda k, j, bi: (0, 0, k, 0, j)))
        args.append(h)
    return pl.pallas_call(
        _spec_fwd_kernel if h is None else _spec_mid_kernel,
        grid=(n1, c // tc, bx),
        in_specs=in_specs,
        out_specs=blk(),
        out_shape=jax.ShapeDtypeStruct(a.shape, F32),
        compiler_params=_cparams("parallel", "parallel", "arbitrary"),
        name="dft_minor",
    )(*args)


def _dft_constants(n1, n2):
    n = n1 * n2
    k1 = np.arange(n1, dtype=np.float64)
    ang1 = -2.0 * np.pi * np.outer(k1, k1) / n1
    w1r, w1i = np.cos(ang1), np.sin(ang1)
    k2 = np.arange(n2, dtype=np.float64)
    ang2 = -2.0 * np.pi * np.outer(k2, k2) / n2
    fmat = np.stack([np.cos(ang2), np.sin(ang2)])
    angt = -2.0 * np.pi * np.outer(k1, k2) / n
    tw = np.stack([np.cos(angt), np.sin(angt)], axis=1)
    fwd_half = np.concatenate([w1r[:, :n1 // 2], w1i[:, :n1 // 2]], axis=0)
    fwd_full = np.concatenate([w1r, w1i], axis=0)
    inv_half = np.concatenate([w1r[:n1 // 2], w1i[:n1 // 2]], axis=1) / n
    f = lambda a: jnp.asarray(a, F32)
    return f(fwd_half), f(fwd_full), f(inv_half), f(fmat), f(tw)


def long_conv(v, h_fwd, h_bwd, n1):
    b, L, c = v.shape
    n2 = FFT_N2
    n = n1 * n2
    assert n == 2 * L
    lf = h_fwd.shape[0]
    taps = jnp.concatenate([h_fwd, jnp.zeros((n - 2 * lf + 1, c), F32), h_bwd[:0:-1]], axis=0)
    fwd_half, fwd_full, inv_half, fmat, tw = _dft_constants(n1, n2)
    ha = leftmat(fwd_full, taps.reshape(1, n1, n2 * c))
    hs = spec_stage(fmat, tw, ha.reshape(1, 2, n1, n2, c))
    a = leftmat(fwd_half, v.reshape(b, n1 // 2, n2 * c))
    bb = spec_stage(fmat, tw, a.reshape(b, 2, n1, n2, c), hs)
    y = leftmat(inv_half, bb.reshape(b, 2 * n1, n2 * c))
    return y.reshape(b, L, c)


def _pad_mod(m3):
    return jnp.pad(m3, ((0, 0), (0, MOD_ROWS - 3), (0, 0)))


def kernel(x, c, ctx, c_ctx, w_mod, b_mod, norm_g, w_ffn_in, w_ffn_out, attn_w_qkv, attn_w_o, attn_lambda,
           attn_subln_g, hy_w_in, hy_b_in, hy_w_short, hy_b_short, hy_f_w1, hy_f_b1, hy_f_w2, hy_f_b2, hy_f_w3,
           hy_f_b3, hy_f_freq, hy_f_w4, hy_skip, hy_w_out, hy_b_out, cv_w_pw1, cv_b_pw1, cv_w_dw, cv_b_dw,
           cv_ln_g, cv_ln_b, cv_w_pw2, cv_b_pw2, final_g):
    bsz, n_lat, d = x.shape
    n_ctx = ctx.shape[1]
    depth = w_mod.shape[0]
    assert bsz + 1 <= MOD_ROWS

    rows = jnp.concatenate([c, c_ctx[None, :], jnp.zeros((MOD_ROWS - bsz - 1, d), F32)], axis=0)
    table = mod_table(rows, w_mod, b_mod).reshape(depth, MOD_ROWS, N_MOD, d)

    def mods(i, s, latent):
        r = table[i, :bsz] if latent else jnp.broadcast_to(table[i, bsz:bsz + 1], (bsz, N_MOD, d))
        return _pad_mod(r[:, 3 * s:3 * s + 3])

    bf = lambda w: w.astype(BF16)
    xc = ctx
    tk_lat = 1280 if (n_lat + n_ctx) % 1280 == 0 else n_ctx
    rope = rope_tables(n_lat)

    for i in range(depth):
        kind = i % N_MIXERS
        j = i // N_MIXERS
        last = i == depth - 1
        ctx_in_use = (not last) or kind == 0
        ctx_advance = not last
        w_in0, w_out0 = bf(w_ffn_in[i, 0]), bf(w_ffn_out[i, 0])
        w_in1, w_out1 = bf(w_ffn_in[i, 1]), bf(w_ffn_out[i, 1])

        x = ffn(x, mods(i, 0, True), norm_g[i, 0], w_in0, w_out0)
        if ctx_in_use:
            xc = ffn(xc, mods(i, 0, False), norm_g[i, 0], w_in0, w_out0)

        ml, mc = mods(i, 1, True), mods(i, 1, False)
        if kind == 0:
            lam_init = 0.8 - 0.6 * math.exp(-0.3 * i)
            wqkv = bf(attn_w_qkv[j])
            wo = bf(attn_w_o[j])
            qkv_l = modproj(x, ml, norm_g[i, 1], wqkv, None, BF16, rope=rope, rope_blocks=2)
            qkv_c = modproj(xc, mc, norm_g[i, 1], wqkv, None, BF16)
            k_all = jnp.concatenate([qkv_l[:, :, d:2 * d], qkv_c[:, :, d:2 * d]], axis=1)
            v_all = jnp.concatenate([qkv_l[:, :, 2 * d:], qkv_c[:, :, 2 * d:]], axis=1)
            kt, vt = _key_value_tiles(k_all, v_all, tk_lat)
            o_l = diff_attention(qkv_l[:, :, :d], kt, vt, attn_lambda[j], attn_subln_g[j], lam_init)
            x = attn_out(o_l, wo, x, ml)
            if ctx_advance:
                ktc, vtc = _key_value_tiles(qkv_c[:, :, d:2 * d], qkv_c[:, :, 2 * d:], n_ctx)
                o_c = diff_attention(qkv_c[:, :, :d], ktc, vtc, attn_lambda[j], attn_subln_g[j], lam_init)
                yc_fn = lambda xcur: attn_out(o_c, wo, xcur, mc)
        elif kind == 1:
            w_in, w_o = bf(hy_w_in[j]), bf(hy_w_out[j])
            filt = (hy_f_w1[j], hy_f_b1[j], hy_f_w2[j], hy_f_b2[j], hy_f_w3[j], hy_f_b3[j], hy_f_freq[j], hy_f_w4[j])
            n1 = 2 * n_lat // FFT_N2

            def hyena(xs, mod, L):
                u = modproj(xs, mod, norm_g[i, 1], w_in, hy_b_in[j], F32)
                x0, vg = hyena_gate(u, hy_w_short[j], hy_b_short[j])
                h = hyena_filter(L, *filt, d)
                if L < n_lat:
                    vg_p = jnp.pad(vg, ((0, 0), (0, n_lat - L), (0, 0)))
                else:
                    vg_p = vg
                y = long_conv(vg_p, h[:, :d], h[:, d:], n1)[:, :L]
                return y, vg, x0

            y_l, vg_l, x0_l = hyena(x, ml, n_lat)
            x = hyena_out(y_l, vg_l, x0_l, hy_skip[j], w_o, hy_b_out[j], x, ml)
            if ctx_advance:
                y_c, vg_c, x0_c = hyena(xc, mc, n_ctx)
                yc_fn = lambda xcur: hyena_out(y_c, vg_c, x0_c, hy_skip[j], w_o, hy_b_out[j], xcur, mc)
        else:
            w1, w2 = bf(cv_w_pw1[j]), bf(cv_w_pw2[j])
            u_l = modglu(x, ml, norm_g[i, 1], w1, cv_b_pw1[j])
            x = conformer_out(u_l, cv_w_dw[j], cv_b_dw[j], cv_ln_g[j], cv_ln_b[j], w2, cv_b_pw2[j], x, ml)
            if ctx_advance:
                u_c = modglu(xc, mc, norm_g[i, 1], w1, cv_b_pw1[j])
                yc_fn = lambda xcur: conformer_out(u_c, cv_w_dw[j], cv_b_dw[j], cv_ln_g[j], cv_ln_b[j], w2,
                                                   cv_b_pw2[j], xcur, mc)

        x = ffn(x, mods(i, 2, True), norm_g[i, 2], w_in1, w_out1, final_g=final_g if last else None)
        if ctx_advance:
            xc = yc_fn(xc)
            xc = ffn(xc, mods(i, 2, False), norm_g[i, 2], w_in1, w_out1)
    return x
```

```python
import functools
import math

import numpy as np
import jax
import jax.numpy as jnp
from jax import lax
from jax.experimental import pallas as pl
from jax.experimental.pallas import tpu as pltpu

F32 = jnp.float32
BF16 = jnp.bfloat16

GRID_W = 64
ATTN_HEADS = 8
ROPE_THETA = 10000.0
HY_FAST_DECAY = 0.3
HY_SLOW_DECAY = 1.5
HY_DECAY_TARGET = 1e-2
EPS = 1e-6
LN_EPS = 1e-5
N_MIXERS = 3
N_MOD = 9

LANES = 128
SUBLANES = 8
BF16_SUBLANES = 16
VMEM_LIMIT_BYTES = 56 * 1024 * 1024

MOD_ROWS = SUBLANES
HIGHEST = lax.Precision.HIGHEST


def _cparams(*sem):
    return pltpu.CompilerParams(dimension_semantics=sem, vmem_limit_bytes=VMEM_LIMIT_BYTES)


def _row_tile(n, want):
    t = min(n, want)
    assert n % t == 0, (n, t)
    return t


def _dot(a, b):
    return jnp.dot(a, b, preferred_element_type=F32)


def _dot32(a, b):
    return jnp.dot(a, b, preferred_element_type=F32, precision=HIGHEST)


def _rmsnorm(x, g):
    return x * lax.rsqrt(jnp.mean(x * x, axis=-1, keepdims=True) + EPS) * g


def _modulate(x, g, shift, scale):
    return _rmsnorm(x, g) * (1.0 + scale) + shift


def _silu(x):
    return x * jax.nn.sigmoid(x)


def _mod_kernel(r_ref, w_ref, b_ref, o_ref):
    r = _silu(r_ref[...]).astype(BF16)
    o_ref[0] = _dot(r, w_ref[0].astype(BF16)) + b_ref[0]


def mod_table(rows, w_mod, b_mod):
    depth, d, nm = w_mod.shape
    tn = nm // N_MOD
    return pl.pallas_call(
        _mod_kernel,
        grid=(depth, nm // tn),
        in_specs=[pl.BlockSpec((MOD_ROWS, d), lambda i, j: (0, 0)),
                  pl.BlockSpec((1, d, tn), lambda i, j: (i, 0, j)),
                  pl.BlockSpec((1, 1, tn), lambda i, j: (i, 0, j))],
        out_specs=pl.BlockSpec((1, MOD_ROWS, tn), lambda i, j: (i, 0, j)),
        out_shape=jax.ShapeDtypeStruct((depth, MOD_ROWS, nm), F32),
        compiler_params=_cparams("parallel", "parallel"),
        name="mod_table",
    )(rows, w_mod, b_mod.reshape(depth, 1, nm))


def _ffn_kernel(x_ref, mod_ref, g_ref, wg_ref, wu_ref, wo_ref, *rest, nf, final):
    if final:
        fg_ref, o_ref, h_scr, acc_scr = rest
    else:
        o_ref, h_scr, acc_scr = rest
    f = pl.program_id(2)

    @pl.when(f == 0)
    def _():
        h = _modulate(x_ref[0], g_ref[...], mod_ref[0, 0:1, :], mod_ref[0, 1:2, :])
        h_scr[...] = h.astype(BF16)
        acc_scr[...] = jnp.zeros_like(acc_scr)

    h = h_scr[...]
    gate = _dot(h, wg_ref[...])
    up = _dot(h, wu_ref[...])
    act = (_silu(gate) * up).astype(BF16)
    acc_scr[...] += _dot(act, wo_ref[...])

    @pl.when(f == nf - 1)
    def _():
        xn = x_ref[0] + 0.5 * mod_ref[0, 2:3, :] * acc_scr[...]
        if final:
            xn = _rmsnorm(xn, fg_ref[...])
        o_ref[0] = xn


def ffn(x, mod, g, w_in, w_out, final_g=None, tm=512):
    b, n, d = x.shape
    ff = w_out.shape[0]
    tm = _row_tile(n, tm)
    nf = 2
    tf = ff // nf
    assert tf % LANES == 0
    final = final_g is not None
    in_specs = [pl.BlockSpec((1, tm, d), lambda bi, i, f: (bi, i, 0)),
                pl.BlockSpec((1, MOD_ROWS, d), lambda bi, i, f: (bi, 0, 0)),
                pl.BlockSpec((1, d), lambda bi, i, f: (0, 0)),
                pl.BlockSpec((d, tf), lambda bi, i, f: (0, f)),
                pl.BlockSpec((d, tf), lambda bi, i, f: (0, nf + f)),
                pl.BlockSpec((tf, d), lambda bi, i, f: (f, 0))]
    args = [x, mod, g.reshape(1, d), w_in, w_in, w_out]
    if final:
        in_specs.append(pl.BlockSpec((1, d), lambda bi, i, f: (0, 0)))
        args.append(final_g.reshape(1, d))
    return pl.pallas_call(
        functools.partial(_ffn_kernel, nf=nf, final=final),
        grid=(b, n // tm, nf),
        in_specs=in_specs,
        out_specs=pl.BlockSpec((1, tm, d), lambda bi, i, f: (bi, i, 0)),
        out_shape=jax.ShapeDtypeStruct((b, n, d), F32),
        scratch_shapes=[pltpu.VMEM((tm, d), BF16), pltpu.VMEM((tm, d), F32)],
        compiler_params=_cparams("parallel", "parallel", "arbitrary"),
        name="ffn",
    )(*args)


def _rope_cols(y, cos, sa, sb):
    outs = []
    for c in range(y.shape[1] // LANES):
        yc = y[:, c * LANES:(c + 1) * LANES]
        outs.append(yc * cos + pltpu.roll(yc, LANES - 16, 1) * sa + pltpu.roll(yc, 16, 1) * sb)
    return jnp.concatenate(outs, axis=1)


def _modproj_kernel(*refs, has_bias, rope_blocks):
    x_ref, mod_ref, g_ref, w_ref = refs[:4]
    rest = refs[4:]
    if has_bias:
        b_ref, rest = rest[0], rest[1:]
    if rope_blocks:
        cos_ref, sa_ref, sb_ref, o_ref, h_scr = rest
    else:
        o_ref, h_scr = rest
    j = pl.program_id(2)

    @pl.when(j == 0)
    def _():
        h = _modulate(x_ref[0], g_ref[...], mod_ref[0, 0:1, :], mod_ref[0, 1:2, :])
        h_scr[...] = h.astype(BF16)

    y = _dot(h_scr[...], w_ref[...])
    if has_bias:
        y = y + b_ref[...]
    if rope_blocks:
        @pl.when(j < rope_blocks)
        def _():
            o_ref[0] = _rope_cols(y, cos_ref[...], sa_ref[...], sb_ref[...]).astype(o_ref.dtype)

        @pl.when(j >= rope_blocks)
        def _():
            o_ref[0] = y.astype(o_ref.dtype)
    else:
        o_ref[0] = y.astype(o_ref.dtype)


def modproj(x, mod, g, w, bias, out_dtype, rope=None, rope_blocks=0, tm=512, tn=1024):
    b, n, d = x.shape
    nout = w.shape[1]
    tm = _row_tile(n, tm)
    in_specs = [pl.BlockSpec((1, tm, d), lambda bi, i, j: (bi, i, 0)),
                pl.BlockSpec((1, MOD_ROWS, d), lambda bi, i, j: (bi, 0, 0)),
                pl.BlockSpec((1, d), lambda bi, i, j: (0, 0)),
                pl.BlockSpec((d, tn), lambda bi, i, j: (0, j))]
    args = [x, mod, g.reshape(1, d), w]
    if bias is not None:
        in_specs.append(pl.BlockSpec((1, tn), lambda bi, i, j: (0, j)))
        args.append(bias.reshape(1, nout))
    if rope_blocks:
        in_specs += [pl.BlockSpec((tm, LANES), lambda bi, i, j: (i, 0))] * 3
        args += list(rope)
    return pl.pallas_call(
        functools.partial(_modproj_kernel, has_bias=bias is not None, rope_blocks=rope_blocks),
        grid=(b, n // tm, nout // tn),
        in_specs=in_specs,
        out_specs=pl.BlockSpec((1, tm, tn), lambda bi, i, j: (bi, i, j)),
        out_shape=jax.ShapeDtypeStruct((b, n, nout), out_dtype),
        scratch_shapes=[pltpu.VMEM((tm, d), BF16)],
        compiler_params=_cparams("parallel", "parallel", "arbitrary"),
        name="modproj",
    )(*args)


def rope_tables(n):
    rows = n // GRID_W
    row = jnp.repeat(jnp.arange(rows), GRID_W).astype(F32)
    col = jnp.tile(jnp.arange(GRID_W), rows).astype(F32)
    quarter = 16
    half = 32
    inv = ROPE_THETA ** (-(2.0 * jnp.arange(quarter, dtype=F32)) / half)
    ang_r = row[:, None] * inv
    ang_c = col[:, None] * inv
    zero = jnp.zeros_like(ang_r)
    cr, sr, cc, sc = jnp.cos(ang_r), jnp.sin(ang_r), jnp.cos(ang_c), jnp.sin(ang_c)
    cos64 = jnp.concatenate([cr, cr, cc, cc], axis=1)
    sa64 = jnp.concatenate([-sr, zero, -sc, zero], axis=1)
    sb64 = jnp.concatenate([zero, sr, zero, sc], axis=1)
    tile2 = lambda t: jnp.concatenate([t, t], axis=1)
    return tile2(cos64), tile2(sa64), tile2(sb64)


V_ROWS = LANES + BF16_SUBLANES


def _attn_kernel(q_ref, k_ref, v_ref, lam_ref, g_ref, o_ref, qm_scr, sa_scr, sb_scr, m_scr, acc_scr, *,
                 nkt, nq, tq, lam_init, half):
    lane = lax.broadcasted_iota(jnp.int32, (tq, LANES), 1)
    for qi in range(nq):
        q = q_ref[0, qi * tq:(qi + 1) * tq, :].astype(F32) * (half ** -0.5 * math.log2(math.e))
        qm_scr[qi, 0] = jnp.where(lane < half, q, 0.0).astype(BF16)
        qm_scr[qi, 1] = jnp.where(lane >= half, q, 0.0).astype(BF16)
    m_scr[...] = jnp.full(m_scr.shape, -jnp.inf, F32)
    acc_scr[...] = jnp.zeros_like(acc_scr)

    def scores(u, s_ref):
        qi, t = u // nkt, u % nkt
        k = k_ref[0, 0, t]
        for mi in range(2):
            s_ref[mi] = lax.dot_general(k, qm_scr[qi, mi], (((1,), (1,)), ((), ())),
                                        preferred_element_type=F32)

    def softmax_pv(u, s_ref):
        qi, t = u // nkt, u % nkt
        vt = v_ref[0, 0, t]
        for mi in range(2):
            s = s_ref[mi]
            m_old = m_scr[qi, mi]
            m_new = jnp.maximum(m_old, jnp.max(s, axis=0, keepdims=True))
            alpha = jnp.exp2(m_old - m_new)
            p = jnp.exp2((s - m_new).astype(BF16))
            acc_scr[qi, mi] = alpha * acc_scr[qi, mi] + _dot(vt, p)
            m_scr[qi, mi] = m_new

    units = nq * nkt
    scores(0, sa_scr)

    def body(j, carry):
        scores(2 * j + 1, sb_scr)
        softmax_pv(2 * j, sa_scr)
        scores(2 * j + 2, sa_scr)
        softmax_pv(2 * j + 1, sb_scr)
        return carry

    lax.fori_loop(0, (units - 1) // 2, body, 0)
    if (units - 1) % 2:
        scores(units - 1, sb_scr)
        softmax_pv(units - 2, sa_scr)
        softmax_pv(units - 1, sb_scr)
    else:
        softmax_pv(units - 1, sa_scr)

    lv = lam_ref[...]
    lam = (jnp.exp(jnp.sum(lv[0:1] * lv[1:2], axis=1, keepdims=True))
           - jnp.exp(jnp.sum(lv[2:3] * lv[3:4], axis=1, keepdims=True)) + lam_init)
    for qi in range(nq):
        a1 = acc_scr[qi, 0]
        a2 = acc_scr[qi, 1]
        ot = a1[:LANES] / a1[LANES:LANES + 1] - lam * (a2[:LANES] / a2[LANES:LANES + 1])
        o = _rmsnorm(ot.T, g_ref[...]) * (1.0 - lam_init)
        o_ref[0, qi * tq:(qi + 1) * tq, :] = o.astype(o_ref.dtype)


def diff_attention(q, kt, vt, lam_vecs, subln_g, lam_init, tq=256, nq=4):
    b, n, dq = q.shape
    _, h, nkt, tk, _ = kt.shape
    tq = _row_tile(n, tq)
    nq = min(nq, n // tq)
    tb = nq * tq
    assert n % tb == 0
    return pl.pallas_call(
        functools.partial(_attn_kernel, nkt=nkt, nq=nq, tq=tq, lam_init=lam_init, half=LANES // 2),
        grid=(b, h, n // tb),
        in_specs=[pl.BlockSpec((1, tb, LANES), lambda bi, hi, i: (bi, i, hi)),
                  pl.BlockSpec((1, 1, nkt, tk, LANES), lambda bi, hi, i: (bi, hi, 0, 0, 0)),
                  pl.BlockSpec((1, 1, nkt, V_ROWS, tk), lambda bi, hi, i: (bi, hi, 0, 0, 0)),
                  pl.BlockSpec((4, LANES // 2), lambda bi, hi, i: (0, 0)),
                  pl.BlockSpec((1, LANES), lambda bi, hi, i: (0, 0))],
        out_specs=pl.BlockSpec((1, tb, LANES), lambda bi, hi, i: (bi, i, hi)),
        out_shape=jax.ShapeDtypeStruct((b, n, dq), BF16),
        scratch_shapes=[pltpu.VMEM((nq, 2, tq, LANES), BF16),
                        pltpu.VMEM((2, tk, tq), F32), pltpu.VMEM((2, tk, tq), F32),
                        pltpu.VMEM((nq, 2, 1, tq), F32), pltpu.VMEM((nq, 2, V_ROWS, tq), F32)],
        compiler_params=_cparams("parallel", "parallel", "arbitrary"),
        name="diff_attention",
    )(q, kt, vt, lam_vecs, subln_g.reshape(1, LANES))


def _key_value_tiles(k, v, tk):
    b, nk, _ = k.shape
    h = ATTN_HEADS
    nkt = nk // tk
    kt = k.reshape(b, nkt, tk, h, LANES).transpose(0, 3, 1, 2, 4)
    v5 = v.reshape(b, nkt, tk, h, LANES).transpose(0, 3, 1, 4, 2)
    ones = jnp.ones((b, h, nkt, 1, tk), BF16)
    zeros = jnp.zeros((b, h, nkt, V_ROWS - LANES - 1, tk), BF16)
    return kt, jnp.concatenate([v5, ones, zeros], axis=3)


def _resid_out(x_ref, mod_ref, y, o_ref):
    o_ref[0] = x_ref[0] + mod_ref[0, 2:3, :] * y


def _attn_out_kernel(a_ref, w_ref, x_ref, mod_ref, o_ref):
    _resid_out(x_ref, mod_ref, _dot(a_ref[0], w_ref[...]), o_ref)


def attn_out(a, w, x, mod, tm=512):
    b, n, d = x.shape
    tm = _row_tile(n, tm)
    row = pl.BlockSpec((1, tm, d), lambda bi, i: (bi, i, 0))
    return pl.pallas_call(
        _attn_out_kernel,
        grid=(b, n // tm),
        in_specs=[row, pl.BlockSpec((d, d), lambda bi, i: (0, 0)), row,
                  pl.BlockSpec((1, MOD_ROWS, d), lambda bi, i: (bi, 0, 0))],
        out_specs=row,
        out_shape=jax.ShapeDtypeStruct((b, n, d), F32),
        compiler_params=_cparams("parallel", "parallel"),
        name="attn_out",
    )(a, w, x, mod)


def _hyena_out_kernel(y_ref, vg_ref, x0_ref, skip_ref, w_ref, b_ref, x_ref, mod_ref, o_ref):
    a = ((y_ref[0] + vg_ref[0] * skip_ref[...]) * x0_ref[0]).astype(BF16)
    _resid_out(x_ref, mod_ref, _dot(a, w_ref[...]) + b_ref[...], o_ref)


def hyena_out(y, vg, x0, skip, w, bias, x, mod, tm=512):
    b, n, d = x.shape
    tm = _row_tile(n, tm)
    row = pl.BlockSpec((1, tm, d), lambda bi, i: (bi, i, 0))
    vec = pl.BlockSpec((1, d), lambda bi, i: (0, 0))
    return pl.pallas_call(
        _hyena_out_kernel,
        grid=(b, n // tm),
        in_specs=[row, row, row, vec, pl.BlockSpec((d, d), lambda bi, i: (0, 0)), vec, row,
                  pl.BlockSpec((1, MOD_ROWS, d), lambda bi, i: (bi, 0, 0))],
        out_specs=row,
        out_shape=jax.ShapeDtypeStruct((b, n, d), F32),
        compiler_params=_cparams("parallel", "parallel"),
        name="hyena_out",
    )(y, vg, x0, skip.reshape(1, d), w, bias.reshape(1, d), x, mod)


def _fill_window(ext_ref, prev_ref, cur_ref, next_ref, halo, tm, i, last):
    ext_ref[0:halo, :] = jnp.where(i == 0, 0.0, prev_ref[0])
    ext_ref[halo:halo + tm, :] = cur_ref[0]
    ext_ref[halo + tm:halo + tm + halo, :] = jnp.where(i == last, 0.0, next_ref[0])


def _dwconv(ext_ref, w_ref, halo, tm, c0, c1):
    width = w_ref.shape[0]
    pad = (width - 1) // 2
    acc = None
    for j in range(width):
        term = w_ref[j:j + 1, c0:c1] * ext_ref[pl.ds(halo - pad + j, tm), c0:c1]
        acc = term if acc is None else acc + term
    return acc


def _halo_specs(tm, halo, n, c):
    per = tm // halo
    nblk = n // halo
    prev = pl.BlockSpec((1, halo, c), lambda bi, i: (bi, jnp.maximum(i * per - 1, 0), 0))
    cur = pl.BlockSpec((1, tm, c), lambda bi, i: (bi, i, 0))
    nxt = pl.BlockSpec((1, halo, c), lambda bi, i: (bi, jnp.minimum((i + 1) * per, nblk - 1), 0))
    return [prev, cur, nxt]


def _hyena_gate_kernel(prev_ref, cur_ref, next_ref, w_ref, b_ref, x0_ref, vg_ref, ext_ref, *, halo, tm, d, last):
    i = pl.program_id(1)
    _fill_window(ext_ref, prev_ref, cur_ref, next_ref, halo, tm, i, last)
    x0_ref[0] = _dwconv(ext_ref, w_ref, halo, tm, 0, d) + b_ref[:, 0:d]
    x1 = _dwconv(ext_ref, w_ref, halo, tm, d, 2 * d) + b_ref[:, d:2 * d]
    v = _dwconv(ext_ref, w_ref, halo, tm, 2 * d, 3 * d) + b_ref[:, 2 * d:3 * d]
    vg_ref[0] = v * x1


def hyena_gate(u, w_short, b_short, tm=256):
    b, n, c = u.shape
    d = c // 3
    tm = _row_tile(n, tm)
    halo = SUBLANES
    row = pl.BlockSpec((1, tm, d), lambda bi, i: (bi, i, 0))
    out = jax.ShapeDtypeStruct((b, n, d), F32)
    return pl.pallas_call(
        functools.partial(_hyena_gate_kernel, halo=halo, tm=tm, d=d, last=n // tm - 1),
        grid=(b, n // tm),
        in_specs=_halo_specs(tm, halo, n, c) + [pl.BlockSpec(w_short.shape, lambda bi, i: (0, 0)),
                                                pl.BlockSpec((1, c), lambda bi, i: (0, 0))],
        out_specs=[row, row],
        out_shape=[out, out],
        scratch_shapes=[pltpu.VMEM((tm + 2 * halo, c), F32)],
        compiler_params=_cparams("parallel", "parallel"),
        name="hyena_gate",
    )(u, u, u, w_short, b_short.reshape(1, c))


def _conformer_out_kernel(prev_ref, cur_ref, next_ref, wdw_ref, bdw_ref, lg_ref, lb_ref, w_ref, b_ref,
                          x_ref, mod_ref, o_ref, ext_ref, *, halo, tm, d, last):
    i = pl.program_id(1)
    _fill_window(ext_ref, prev_ref, cur_ref, next_ref, halo, tm, i, last)
    u = _dwconv(ext_ref, wdw_ref, halo, tm, 0, d) + bdw_ref[...]
    mu = jnp.mean(u, axis=-1, keepdims=True)
    uc = u - mu
    var = jnp.mean(uc * uc, axis=-1, keepdims=True)
    z = _silu(uc * lax.rsqrt(var + LN_EPS) * lg_ref[...] + lb_ref[...]).astype(BF16)
    _resid_out(x_ref, mod_ref, _dot(z, w_ref[...]) + b_ref[...], o_ref)


def conformer_out(u, w_dw, b_dw, ln_g, ln_b, w, bias, x, mod, tm=256):
    b, n, d = x.shape
    tm = _row_tile(n, tm)
    halo = 2 * SUBLANES
    assert (w_dw.shape[0] - 1) // 2 <= halo
    row = pl.BlockSpec((1, tm, d), lambda bi, i: (bi, i, 0))
    vec = pl.BlockSpec((1, d), lambda bi, i: (0, 0))
    return pl.pallas_call(
        functools.partial(_conformer_out_kernel, halo=halo, tm=tm, d=d, last=n // tm - 1),
        grid=(b, n // tm),
        in_specs=_halo_specs(tm, halo, n, d) + [pl.BlockSpec(w_dw.shape, lambda bi, i: (0, 0)), vec, vec, vec,
                                                pl.BlockSpec((d, d), lambda bi, i: (0, 0)), vec, row,
                                                pl.BlockSpec((1, MOD_ROWS, d), lambda bi, i: (bi, 0, 0))],
        out_specs=row,
        out_shape=jax.ShapeDtypeStruct((b, n, d), F32),
        scratch_shapes=[pltpu.VMEM((tm + 2 * halo, d), F32)],
        compiler_params=_cparams("parallel", "parallel"),
        name="conformer_out",
    )(u, u, u, w_dw, b_dw.reshape(1, d), ln_g.reshape(1, d), ln_b.reshape(1, d), w, bias.reshape(1, d), x, mod)


def _modglu_kernel(x_ref, mod_ref, g_ref, wa_ref, wg_ref, ba_ref, bg_ref, o_ref):
    h = _modulate(x_ref[0], g_ref[...], mod_ref[0, 0:1, :], mod_ref[0, 1:2, :]).astype(BF16)
    a = _dot(h, wa_ref[...]) + ba_ref[...]
    gt = _dot(h, wg_ref[...]) + bg_ref[...]
    o_ref[0] = a * jax.nn.sigmoid(gt)


def modglu(x, mod, g, w, bias, tm=512):
    b, n, d = x.shape
    tm = _row_tile(n, tm)
    row = pl.BlockSpec((1, tm, d), lambda bi, i: (bi, i, 0))
    bias2 = bias.reshape(1, 2 * d)
    return pl.pallas_call(
        _modglu_kernel,
        grid=(b, n // tm),
        in_specs=[row, pl.BlockSpec((1, MOD_ROWS, d), lambda bi, i: (bi, 0, 0)),
                  pl.BlockSpec((1, d), lambda bi, i: (0, 0)),
                  pl.BlockSpec((d, d), lambda bi, i: (0, 0)), pl.BlockSpec((d, d), lambda bi, i: (0, 1)),
                  pl.BlockSpec((1, d), lambda bi, i: (0, 0)), pl.BlockSpec((1, d), lambda bi, i: (0, 1))],
        out_specs=row,
        out_shape=jax.ShapeDtypeStruct((b, n, d), F32),
        compiler_params=_cparams("parallel", "parallel"),
        name="modglu",
    )(x, mod, g.reshape(1, d), w, w, bias2, bias2)


def _filter_kernel(emb_ref, t_ref, w1_ref, b1_ref, w2_ref, b2_ref, w3_ref, b3_ref, fr_ref, w4_ref, dl_ref, o_ref):
    fr = fr_ref[...]
    hdn = jnp.sin(fr * (_dot32(emb_ref[...], w1_ref[...]) + b1_ref[...]))
    hdn = jnp.sin(fr * (_dot32(hdn, w2_ref[...]) + b2_ref[...]))
    hdn = jnp.sin(fr * (_dot32(hdn, w3_ref[...]) + b3_ref[...]))
    h = _dot32(hdn, w4_ref[...]) * jnp.exp(-t_ref[...] * dl_ref[...])
    row = lax.broadcasted_iota(jnp.int32, h.shape, 0)
    drop = (row == 0) & (pl.program_id(0) == 0) & (pl.program_id(1) == 1)
    o_ref[...] = jnp.where(drop, 0.0, h)


def hyena_filter(L, w1, b1, w2, b2, w3, b3, freq, w4, d_model, tl=256):
    bands = (w1.shape[0] - 1) // 2
    hid = w1.shape[1]
    t = jnp.linspace(0.0, 1.0, L, dtype=F32)[:, None]
    wpos = (2.0 * math.pi / L) * jnp.arange(L, dtype=F32)
    bnd = jnp.linspace(1e-4, bands - 1, bands, dtype=F32)
    fw = wpos[:, None] * bnd[None, :]
    emb = jnp.concatenate([t, jnp.cos(fw), -jnp.sin(fw)], axis=-1)
    max_decay = math.log(HY_DECAY_TARGET) / HY_FAST_DECAY
    min_decay = math.log(HY_DECAY_TARGET) / HY_SLOW_DECAY
    deltas = jnp.abs(jnp.linspace(min_decay, max_decay, d_model, dtype=F32))
    dl2 = jnp.concatenate([deltas, deltas])[None, :]
    tl = _row_tile(L, tl)
    td = d_model
    full = lambda a: pl.BlockSpec(a.shape, lambda i, j: (0, 0))
    vec = lambda a: a.reshape(1, -1)
    args = [emb, t, w1, vec(b1), w2, vec(b2), w3, vec(b3), vec(freq)]
    return pl.pallas_call(
        _filter_kernel,
        grid=(L // tl, 2 * d_model // td),
        in_specs=[pl.BlockSpec((tl, emb.shape[1]), lambda i, j: (i, 0)), pl.BlockSpec((tl, 1), lambda i, j: (i, 0))]
                 + [full(a) for a in args[2:]]
                 + [pl.BlockSpec((hid, td), lambda i, j: (0, j)), pl.BlockSpec((1, td), lambda i, j: (0, j))],
        out_specs=pl.BlockSpec((tl, td), lambda i, j: (i, j)),
        out_shape=jax.ShapeDtypeStruct((L, 2 * d_model), F32),
        compiler_params=_cparams("parallel", "parallel"),
        name="hyena_filter",
    )(*args, w4, dl2)


FFT_N2 = 128


def _split_bf16(a):
    hi = a.astype(BF16)
    return hi, (a - hi.astype(F32)).astype(BF16)


def _dotp_split(ah, al, b):
    bh, bl = _split_bf16(b)
    return _dot(ah, bh) + (_dot(ah, bl) + _dot(al, bh))


def _dotp(a, b):
    return _dotp_split(*_split_bf16(a), b)


def _fwd_major_kernel(m_ref, x_ref, o_ref, *, n1):
    mh, ml = _split_bf16(m_ref[...])
    for s in range(SUBLANES):
        y = _dotp_split(mh, ml, x_ref[0, :, s, :])
        o_ref[0, 0, :, s, :] = y[:n1]
        o_ref[0, 1, :, s, :] = y[n1:]


def dft_fwd_major(mat, x, tc=256):
    bx, r, n2, c = x.shape
    n1 = mat.shape[0] // 2
    tc = _row_tile(c, tc)
    return pl.pallas_call(
        functools.partial(_fwd_major_kernel, n1=n1),
        grid=(bx, n2 // SUBLANES, c // tc),
        in_specs=[pl.BlockSpec(mat.shape, lambda bi, a, j: (0, 0)),
                  pl.BlockSpec((1, r, SUBLANES, tc), lambda bi, a, j: (bi, 0, a, j))],
        out_specs=pl.BlockSpec((1, 2, n1, SUBLANES, tc), lambda bi, a, j: (bi, 0, 0, a, j)),
        out_shape=jax.ShapeDtypeStruct((bx, 2, n1, n2, c), F32),
        compiler_params=_cparams("parallel", "parallel", "parallel"),
        name="dft_fwd_major",
    )(mat, x)


def _inv_major_kernel(m_ref, b_ref, o_ref):
    mh, ml = _split_bf16(m_ref[...])
    for s in range(SUBLANES):
        z = jnp.concatenate([b_ref[0, 0, :, s, :], b_ref[0, 1, :, s, :]], axis=0)
        o_ref[0, :, s, :] = _dotp_split(mh, ml, z)


def dft_inv_major(mat, bb, tc=256):
    bx, _, n1, n2, c = bb.shape
    tc = _row_tile(c, tc)
    return pl.pallas_call(
        _inv_major_kernel,
        grid=(bx, n2 // SUBLANES, c // tc),
        in_specs=[pl.BlockSpec(mat.shape, lambda bi, a, j: (0, 0)),
                  pl.BlockSpec((1, 2, n1, SUBLANES, tc), lambda bi, a, j: (bi, 0, 0, a, j))],
        out_specs=pl.BlockSpec((1, n1 // 2, SUBLANES, tc), lambda bi, a, j: (bi, 0, a, j)),
        out_shape=jax.ShapeDtypeStruct((bx, n1 // 2, n2, c), F32),
        compiler_params=_cparams("parallel", "parallel", "parallel"),
        name="dft_inv_major",
    )(mat, bb)


def _twiddled_block(f_ref, tw_ref):
    fr, fi = f_ref[0], f_ref[1]
    twr, twi = tw_ref[0, 0:1, :], tw_ref[0, 1:2, :]
    gr = fr * twr - fi * twi
    gi = fr * twi + fi * twr
    return jnp.concatenate([jnp.concatenate([gr, -gi], axis=1), jnp.concatenate([gi, gr], axis=1)], axis=0)


def _spec_fwd_kernel(f_ref, tw_ref, a_ref, o_ref):
    n2 = f_ref.shape[1]
    g = _twiddled_block(f_ref, tw_ref)
    x = _dotp(g, a_ref[0, :, 0].reshape(2 * n2, -1))
    o_ref[0, :, 0] = x.reshape(2, n2, -1)


def _spec_mid_kernel(f_ref, tw_ref, a_ref, hf_ref, hb_ref, o_ref):
    n2 = f_ref.shape[1]
    g = _twiddled_block(f_ref, tw_ref)
    x = _dotp(g, a_ref[0, :, 0].reshape(2 * n2, -1))
    xr, xi = x[:n2], x[n2:]
    hr = hf_ref[0, 0, 0] + hb_ref[0, 0, 0]
    hi = hf_ref[0, 1, 0] - hb_ref[0, 1, 0]
    z = jnp.concatenate([xr * hr - xi * hi, xr * hi + xi * hr], axis=0)
    o_ref[0, :, 0] = _dotp(g.T, z).reshape(2, n2, -1)


def spec_stage(fmat, tw, a, h=None, tc=1024):
    bx, _, n1, n2, c = a.shape
    tc = _row_tile(c, tc)
    blk = lambda: pl.BlockSpec((1, 2, 1, n2, tc), lambda k, j, bi: (bi, 0, k, 0, j))
    in_specs = [pl.BlockSpec((2, n2, n2), lambda k, j, bi: (0, 0, 0)),
                pl.BlockSpec((1, 2, n2), lambda k, j, bi: (k, 0, 0)), blk()]
    args = [fmat, tw, a]
    if h is not None:
        nj = c // tc
        in_specs += [pl.BlockSpec((1, 2, 1, n2, tc), lambda k, j, bi: (0, 0, k, 0, j)),
                     pl.BlockSpec((1, 2, 1, n2, tc), lambda k, j, bi: (0, 0, k, 0, nj + j))]
        args += [h, h]
    return pl.pallas_call(
        _spec_fwd_kernel if h is None else _spec_mid_kernel,
        grid=(n1, c // tc, bx),
        in_specs=in_specs,
        out_specs=blk(),
        out_shape=jax.ShapeDtypeStruct(a.shape, F32),
        compiler_params=_cparams("parallel", "parallel", "arbitrary"),
        name="dft_minor",
    )(*args)


def _dft_constants(n1, n2):
    n = n1 * n2
    k1 = np.arange(n1, dtype=np.float64)
    ang1 = -2.0 * np.pi * np.outer(k1, k1) / n1
    w1r, w1i = np.cos(ang1), np.sin(ang1)
    k2 = np.arange(n2, dtype=np.float64)
    ang2 = -2.0 * np.pi * np.outer(k2, k2) / n2
    fmat = np.stack([np.cos(ang2), np.sin(ang2)])
    angt = -2.0 * np.pi * np.outer(k1, k2) / n
    tw = np.stack([np.cos(angt), np.sin(angt)], axis=1)
    fwd_half = np.concatenate([w1r[:, :n1 // 2], w1i[:, :n1 // 2]], axis=0)
    inv_half = np.concatenate([w1r[:n1 // 2], w1i[:n1 // 2]], axis=1) / n
    f = lambda a: jnp.asarray(a, F32)
    return f(fwd_half), f(inv_half), f(fmat), f(tw)


def long_conv(v, taps):
    b, L, c = v.shape
    n2 = FFT_N2
    n1 = 2 * L // n2
    fwd_half, inv_half, fmat, tw = _dft_constants(n1, n2)
    hs = spec_stage(fmat, tw, dft_fwd_major(fwd_half, taps.reshape(1, n1 // 2, n2, 2 * c)))
    a = dft_fwd_major(fwd_half, v.reshape(b, n1 // 2, n2, c))
    y = dft_inv_major(inv_half, spec_stage(fmat, tw, a, hs))
    return y.reshape(b, L, c)


def _dense_conv_kernel(f_ref, g_ref, v_ref, hf_ref, hb_ref, o_ref):
    n = f_ref.shape[0] // 2
    f = f_ref[...]
    x = _dotp(f, v_ref[0])
    hf = _dotp(f, hf_ref[...])
    hb = _dotp(f, hb_ref[...])
    xr, xi = x[:n], x[n:]
    hr = hf[:n] + hb[:n]
    hi = hf[n:] - hb[n:]
    z = jnp.concatenate([xr * hr - xi * hi, xr * hi + xi * hr], axis=0)
    o_ref[0] = _dotp(g_ref[...], z)


def dense_long_conv(v, taps, tc=512):
    b, L, c = v.shape
    n = 2 * L
    ang = 2.0 * np.pi * np.outer(np.arange(n, dtype=np.float64), np.arange(L, dtype=np.float64)) / n
    fmat = jnp.asarray(np.concatenate([np.cos(ang), -np.sin(ang)], axis=0), F32)
    gmat = jnp.asarray(np.concatenate([np.cos(ang.T), -np.sin(ang.T)], axis=1) / n, F32)
    tc = _row_tile(c, tc)
    nj = c // tc
    return pl.pallas_call(
        _dense_conv_kernel,
        grid=(b, nj),
        in_specs=[pl.BlockSpec((2 * n, L), lambda bi, j: (0, 0)), pl.BlockSpec((L, 2 * n), lambda bi, j: (0, 0)),
                  pl.BlockSpec((1, L, tc), lambda bi, j: (bi, 0, j)),
                  pl.BlockSpec((L, tc), lambda bi, j: (0, j)), pl.BlockSpec((L, tc), lambda bi, j: (0, nj + j))],
        out_specs=pl.BlockSpec((1, L, tc), lambda bi, j: (bi, 0, j)),
        out_shape=jax.ShapeDtypeStruct((b, L, c), F32),
        compiler_params=_cparams("parallel", "parallel"),
        name="dense_long_conv",
    )(fmat, gmat, v, taps, taps)


def _pad_mod(m3):
    return jnp.pad(m3, ((0, 0), (0, MOD_ROWS - 3), (0, 0)))


def kernel(x, c, ctx, c_ctx, w_mod, b_mod, norm_g, w_ffn_in, w_ffn_out, attn_w_qkv, attn_w_o, attn_lambda,
           attn_subln_g, hy_w_in, hy_b_in, hy_w_short, hy_b_short, hy_f_w1, hy_f_b1, hy_f_w2, hy_f_b2, hy_f_w3,
           hy_f_b3, hy_f_freq, hy_f_w4, hy_skip, hy_w_out, hy_b_out, cv_w_pw1, cv_b_pw1, cv_w_dw, cv_b_dw,
           cv_ln_g, cv_ln_b, cv_w_pw2, cv_b_pw2, final_g):
    bsz, n_lat, d = x.shape
    n_ctx = ctx.shape[1]
    depth = w_mod.shape[0]
    assert bsz + 1 <= MOD_ROWS

    rows = jnp.concatenate([c, c_ctx[None, :], jnp.zeros((MOD_ROWS - bsz - 1, d), F32)], axis=0)
    table = mod_table(rows, w_mod, b_mod).reshape(depth, MOD_ROWS, N_MOD, d)

    def mods(i, s, latent):
        r = table[i, :bsz] if latent else jnp.broadcast_to(table[i, bsz:bsz + 1], (bsz, N_MOD, d))
        return _pad_mod(r[:, 3 * s:3 * s + 3])

    bf = lambda w: w.astype(BF16)
    xc = ctx
    tk_lat = 1280 if (n_lat + n_ctx) % 1280 == 0 else n_ctx
    rope = rope_tables(n_lat)

    for i in range(depth):
        kind = i % N_MIXERS
        j = i // N_MIXERS
        last = i == depth - 1
        ctx_in_use = (not last) or kind == 0
        ctx_advance = not last
        w_in0, w_out0 = bf(w_ffn_in[i, 0]), bf(w_ffn_out[i, 0])
        w_in1, w_out1 = bf(w_ffn_in[i, 1]), bf(w_ffn_out[i, 1])

        x = ffn(x, mods(i, 0, True), norm_g[i, 0], w_in0, w_out0)
        if ctx_in_use:
            xc = ffn(xc, mods(i, 0, False), norm_g[i, 0], w_in0, w_out0)

        ml, mc = mods(i, 1, True), mods(i, 1, False)
        if kind == 0:
            lam_init = 0.8 - 0.6 * math.exp(-0.3 * i)
            wqkv = bf(attn_w_qkv[j])
            wo = bf(attn_w_o[j])
            qkv_l = modproj(x, ml, norm_g[i, 1], wqkv, None, BF16, rope=rope, rope_blocks=2)
            qkv_c = modproj(xc, mc, norm_g[i, 1], wqkv, None, BF16)
            k_all = jnp.concatenate([qkv_l[:, :, d:2 * d], qkv_c[:, :, d:2 * d]], axis=1)
            v_all = jnp.concatenate([qkv_l[:, :, 2 * d:], qkv_c[:, :, 2 * d:]], axis=1)
            kt, vt = _key_value_tiles(k_all, v_all, tk_lat)
            o_l = diff_attention(qkv_l[:, :, :d], kt, vt, attn_lambda[j], attn_subln_g[j], lam_init)
            x = attn_out(o_l, wo, x, ml)
            if ctx_advance:
                ktc, vtc = _key_value_tiles(qkv_c[:, :, d:2 * d], qkv_c[:, :, 2 * d:], n_ctx)
                o_c = diff_attention(qkv_c[:, :, :d], ktc, vtc, attn_lambda[j], attn_subln_g[j], lam_init)
                yc_fn = lambda xcur: attn_out(o_c, wo, xcur, mc)
        elif kind == 1:
            w_in, w_o = bf(hy_w_in[j]), bf(hy_w_out[j])
            filt = (hy_f_w1[j], hy_f_b1[j], hy_f_w2[j], hy_f_b2[j], hy_f_w3[j], hy_f_b3[j], hy_f_freq[j], hy_f_w4[j])

            def hyena(xs, mod, L):
                u = modproj(xs, mod, norm_g[i, 1], w_in, hy_b_in[j], F32)
                x0, vg = hyena_gate(u, hy_w_short[j], hy_b_short[j])
                taps = hyena_filter(L, *filt, d)
                y = long_conv(vg, taps) if L == n_lat else dense_long_conv(vg, taps)
                return y, vg, x0

            y_l, vg_l, x0_l = hyena(x, ml, n_lat)
            x = hyena_out(y_l, vg_l, x0_l, hy_skip[j], w_o, hy_b_out[j], x, ml)
            if ctx_advance:
                y_c, vg_c, x0_c = hyena(xc, mc, n_ctx)
                yc_fn = lambda xcur: hyena_out(y_c, vg_c, x0_c, hy_skip[j], w_o, hy_b_out[j], xcur, mc)
        else:
            w1, w2 = bf(cv_w_pw1[j]), bf(cv_w_pw2[j])
            u_l = modglu(x, ml, norm_g[i, 1], w1, cv_b_pw1[j])
            x = conformer_out(u_l, cv_w_dw[j], cv_b_dw[j], cv_ln_g[j], cv_ln_b[j], w2, cv_b_pw2[j], x, ml)
            if ctx_advance:
                u_c = modglu(xc, mc, norm_g[i, 1], w1, cv_b_pw1[j])
                yc_fn = lambda xcur: conformer_out(u_c, cv_w_dw[j], cv_b_dw[j], cv_ln_g[j], cv_ln_b[j], w2,
                                                   cv_b_pw2[j], xcur, mc)

        x = ffn(x, mods(i, 2, True), norm_g[i, 2], w_in1, w_out1, final_g=final_g if last else None)
        if ctx_advance:
            xc = yc_fn(xc)
            xc = ffn(xc, mods(i, 2, False), norm_g[i, 2], w_in1, w_out1)
    return x
```

```python
import functools
import math

import numpy as np
import jax
import jax.numpy as jnp
from jax import lax
from jax.experimental import pallas as pl
from jax.experimental.pallas import tpu as pltpu

F32 = jnp.float32
BF16 = jnp.bfloat16

GRID_W = 64
ATTN_HEADS = 8
ROPE_THETA = 10000.0
HY_FAST_DECAY = 0.3
HY_SLOW_DECAY = 1.5
HY_DECAY_TARGET = 1e-2
EPS = 1e-6
LN_EPS = 1e-5
N_MIXERS = 3
N_MOD = 9

LANES = 128
SUBLANES = 8
BF16_SUBLANES = 16
VMEM_LIMIT_BYTES = 56 * 1024 * 1024

MOD_ROWS = SUBLANES
HIGHEST = lax.Precision.HIGHEST


def _cparams(*sem):
    return pltpu.CompilerParams(dimension_semantics=sem, vmem_limit_bytes=VMEM_LIMIT_BYTES)


def _row_tile(n, want):
    t = min(n, want)
    assert n % t == 0, (n, t)
    return t


def _dot(a, b):
    return jnp.dot(a, b, preferred_element_type=F32)


def _dot32(a, b):
    return jnp.dot(a, b, preferred_element_type=F32, precision=HIGHEST)


def _rmsnorm(x, g):
    return x * lax.rsqrt(jnp.mean(x * x, axis=-1, keepdims=True) + EPS) * g


def _modulate(x, g, shift, scale):
    return _rmsnorm(x, g) * (1.0 + scale) + shift


def _silu(x):
    return x * jax.nn.sigmoid(x)


def _mod_kernel(r_ref, w_ref, b_ref, o_ref):
    r = _silu(r_ref[...]).astype(BF16)
    o_ref[0] = _dot(r, w_ref[0].astype(BF16)) + b_ref[0]


def mod_table(rows, w_mod, b_mod):
    depth, d, nm = w_mod.shape
    tn = nm // N_MOD
    return pl.pallas_call(
        _mod_kernel,
        grid=(depth, nm // tn),
        in_specs=[pl.BlockSpec((MOD_ROWS, d), lambda i, j: (0, 0)),
                  pl.BlockSpec((1, d, tn), lambda i, j: (i, 0, j)),
                  pl.BlockSpec((1, 1, tn), lambda i, j: (i, 0, j))],
        out_specs=pl.BlockSpec((1, MOD_ROWS, tn), lambda i, j: (i, 0, j)),
        out_shape=jax.ShapeDtypeStruct((depth, MOD_ROWS, nm), F32),
        compiler_params=_cparams("parallel", "parallel"),
        name="mod_table",
    )(rows, w_mod, b_mod.reshape(depth, 1, nm))


def _ffn_kernel(x_ref, mod_ref, g_ref, wg_ref, wu_ref, wo_ref, *rest, nf, final):
    if final:
        fg_ref, o_ref, h_scr, acc_scr = rest
    else:
        o_ref, h_scr, acc_scr = rest
    f = pl.program_id(2)

    @pl.when(f == 0)
    def _():
        h = _modulate(x_ref[0], g_ref[...], mod_ref[0, 0:1, :], mod_ref[0, 1:2, :])
        h_scr[...] = h.astype(BF16)
        acc_scr[...] = jnp.zeros_like(acc_scr)

    h = h_scr[...]
    gate = _dot(h, wg_ref[...])
    up = _dot(h, wu_ref[...])
    act = (_silu(gate) * up).astype(BF16)
    acc_scr[...] += _dot(act, wo_ref[...])

    @pl.when(f == nf - 1)
    def _():
        xn = x_ref[0] + 0.5 * mod_ref[0, 2:3, :] * acc_scr[...]
        if final:
            xn = _rmsnorm(xn, fg_ref[...])
        o_ref[0] = xn


def ffn(x, mod, g, w_in, w_out, final_g=None, tm=512):
    b, n, d = x.shape
    ff = w_out.shape[0]
    tm = _row_tile(n, tm)
    nf = 2
    tf = ff // nf
    assert tf % LANES == 0
    final = final_g is not None
    in_specs = [pl.BlockSpec((1, tm, d), lambda bi, i, f: (bi, i, 0)),
                pl.BlockSpec((1, MOD_ROWS, d), lambda bi, i, f: (bi, 0, 0)),
                pl.BlockSpec((1, d), lambda bi, i, f: (0, 0)),
                pl.BlockSpec((d, tf), lambda bi, i, f: (0, f)),
                pl.BlockSpec((d, tf), lambda bi, i, f: (0, nf + f)),
                pl.BlockSpec((tf, d), lambda bi, i, f: (f, 0))]
    args = [x, mod, g.reshape(1, d), w_in, w_in, w_out]
    if final:
        in_specs.append(pl.BlockSpec((1, d), lambda bi, i, f: (0, 0)))
        args.append(final_g.reshape(1, d))
    return pl.pallas_call(
        functools.partial(_ffn_kernel, nf=nf, final=final),
        grid=(b, n // tm, nf),
        in_specs=in_specs,
        out_specs=pl.BlockSpec((1, tm, d), lambda bi, i, f: (bi, i, 0)),
        out_shape=jax.ShapeDtypeStruct((b, n, d), F32),
        scratch_shapes=[pltpu.VMEM((tm, d), BF16), pltpu.VMEM((tm, d), F32)],
        compiler_params=_cparams("parallel", "parallel", "arbitrary"),
        name="ffn",
    )(*args)


def _rope_cols(y, cos, sa, sb):
    outs = []
    for c in range(y.shape[1] // LANES):
        yc = y[:, c * LANES:(c + 1) * LANES]
        outs.append(yc * cos + pltpu.roll(yc, LANES - 16, 1) * sa + pltpu.roll(yc, 16, 1) * sb)
    return jnp.concatenate(outs, axis=1)


def _modproj_kernel(*refs, has_bias, rope_blocks):
    x_ref, mod_ref, g_ref, w_ref = refs[:4]
    rest = refs[4:]
    if has_bias:
        b_ref, rest = rest[0], rest[1:]
    if rope_blocks:
        cos_ref, sa_ref, sb_ref, o_ref, h_scr = rest
    else:
        o_ref, h_scr = rest
    j = pl.program_id(2)

    @pl.when(j == 0)
    def _():
        h = _modulate(x_ref[0], g_ref[...], mod_ref[0, 0:1, :], mod_ref[0, 1:2, :])
        h_scr[...] = h.astype(BF16)

    y = _dot(h_scr[...], w_ref[...])
    if has_bias:
        y = y + b_ref[...]
    if rope_blocks:
        @pl.when(j < rope_blocks)
        def _():
            o_ref[0] = _rope_cols(y, cos_ref[...], sa_ref[...], sb_ref[...]).astype(o_ref.dtype)

        @pl.when(j >= rope_blocks)
        def _():
            o_ref[0] = y.astype(o_ref.dtype)
    else:
        o_ref[0] = y.astype(o_ref.dtype)


def modproj(x, mod, g, w, bias, out_dtype, rope=None, rope_blocks=0, tm=512, tn=1024):
    b, n, d = x.shape
    nout = w.shape[1]
    tm = _row_tile(n, tm)
    in_specs = [pl.BlockSpec((1, tm, d), lambda bi, i, j: (bi, i, 0)),
                pl.BlockSpec((1, MOD_ROWS, d), lambda bi, i, j: (bi, 0, 0)),
                pl.BlockSpec((1, d), lambda bi, i, j: (0, 0)),
                pl.BlockSpec((d, tn), lambda bi, i, j: (0, j))]
    args = [x, mod, g.reshape(1, d), w]
    if bias is not None:
        in_specs.append(pl.BlockSpec((1, tn), lambda bi, i, j: (0, j)))
        args.append(bias.reshape(1, nout))
    if rope_blocks:
        in_specs += [pl.BlockSpec((tm, LANES), lambda bi, i, j: (i, 0))] * 3
        args += list(rope)
    return pl.pallas_call(
        functools.partial(_modproj_kernel, has_bias=bias is not None, rope_blocks=rope_blocks),
        grid=(b, n // tm, nout // tn),
        in_specs=in_specs,
        out_specs=pl.BlockSpec((1, tm, tn), lambda bi, i, j: (bi, i, j)),
        out_shape=jax.ShapeDtypeStruct((b, n, nout), out_dtype),
        scratch_shapes=[pltpu.VMEM((tm, d), BF16)],
        compiler_params=_cparams("parallel", "parallel", "arbitrary"),
        name="modproj",
    )(*args)


def rope_tables(n):
    rows = n // GRID_W
    row = jnp.repeat(jnp.arange(rows), GRID_W).astype(F32)
    col = jnp.tile(jnp.arange(GRID_W), rows).astype(F32)
    quarter = 16
    half = 32
    inv = ROPE_THETA ** (-(2.0 * jnp.arange(quarter, dtype=F32)) / half)
    ang_r = row[:, None] * inv
    ang_c = col[:, None] * inv
    zero = jnp.zeros_like(ang_r)
    cr, sr, cc, sc = jnp.cos(ang_r), jnp.sin(ang_r), jnp.cos(ang_c), jnp.sin(ang_c)
    cos64 = jnp.concatenate([cr, cr, cc, cc], axis=1)
    sa64 = jnp.concatenate([-sr, zero, -sc, zero], axis=1)
    sb64 = jnp.concatenate([zero, sr, zero, sc], axis=1)
    tile2 = lambda t: jnp.concatenate([t, t], axis=1)
    return tile2(cos64), tile2(sa64), tile2(sb64)


V_ROWS = LANES + BF16_SUBLANES
PIPE_UNROLL = 8


def _attn_kernel(q_ref, k_ref, v_ref, lam_ref, g_ref, o_ref, qm_scr, sa_scr, sb_scr, ma_scr, mb_scr, m_scr, acc_scr, *,
                 nkt, nq, tq, lam_init, half):
    lane = lax.broadcasted_iota(jnp.int32, (tq, LANES), 1)
    for qi in range(nq):
        q = q_ref[0, qi * tq:(qi + 1) * tq, :].astype(F32) * (half ** -0.5 * math.log2(math.e))
        qm_scr[qi, 0] = jnp.where(lane < half, q, 0.0).astype(BF16)
        qm_scr[qi, 1] = jnp.where(lane >= half, q, 0.0).astype(BF16)
    m_scr[...] = jnp.full(m_scr.shape, -jnp.inf, F32)
    acc_scr[...] = jnp.zeros_like(acc_scr)

    def scores(u, bufs):
        s_ref, mt_ref = bufs
        qi, t = u // nkt, u % nkt
        k = k_ref[0, 0, t]
        for mi in range(2):
            s = lax.dot_general(k, qm_scr[qi, mi], (((1,), (1,)), ((), ())),
                                preferred_element_type=F32)
            s_ref[mi] = s
            mt_ref[mi] = jnp.max(s, axis=0, keepdims=True)

    def softmax_pv(u, bufs):
        s_ref, mt_ref = bufs
        qi, t = u // nkt, u % nkt
        vt = v_ref[0, 0, t]
        for mi in range(2):
            s = s_ref[mi]
            m_old = m_scr[qi, mi]
            m_new = jnp.maximum(m_old, mt_ref[mi])
            alpha = jnp.exp2(m_old - m_new)
            p = jnp.exp2((s - m_new).astype(BF16))
            acc_scr[qi, mi] = alpha * acc_scr[qi, mi] + _dot(vt, p)
            m_scr[qi, mi] = m_new

    units = nq * nkt
    bufs = ((sa_scr, ma_scr), (sb_scr, mb_scr))
    scores(0, bufs[0])

    def steps(first, count):
        for i in range(count):
            scores(first + i + 1, bufs[(i + 1) % 2])
            softmax_pv(first + i, bufs[i % 2])

    n_steps = units - 1
    n_loop = n_steps // PIPE_UNROLL

    def body(j, carry):
        steps(PIPE_UNROLL * j, PIPE_UNROLL)
        return carry

    lax.fori_loop(0, n_loop, body, 0)
    steps(n_loop * PIPE_UNROLL, n_steps - n_loop * PIPE_UNROLL)
    softmax_pv(units - 1, bufs[(units - 1) % 2])

    lv = lam_ref[...]
    lam = (jnp.exp(jnp.sum(lv[0:1] * lv[1:2], axis=1, keepdims=True))
           - jnp.exp(jnp.sum(lv[2:3] * lv[3:4], axis=1, keepdims=True)) + lam_init)
    for qi in range(nq):
        a1 = acc_scr[qi, 0]
        a2 = acc_scr[qi, 1]
        ot = a1[:LANES] / a1[LANES:LANES + 1] - lam * (a2[:LANES] / a2[LANES:LANES + 1])
        o = _rmsnorm(ot.T, g_ref[...]) * (1.0 - lam_init)
        o_ref[0, qi * tq:(qi + 1) * tq, :] = o.astype(o_ref.dtype)


def diff_attention(q, kt, vt, lam_vecs, subln_g, lam_init, tq=256, nq=4):
    b, n, dq = q.shape
    _, h, nkt, tk, _ = kt.shape
    tq = _row_tile(n, tq)
    nq = min(nq, n // tq)
    tb = nq * tq
    assert n % tb == 0
    return pl.pallas_call(
        functools.partial(_attn_kernel, nkt=nkt, nq=nq, tq=tq, lam_init=lam_init, half=LANES // 2),
        grid=(b, h, n // tb),
        in_specs=[pl.BlockSpec((1, tb, LANES), lambda bi, hi, i: (bi, i, hi)),
                  pl.BlockSpec((1, 1, nkt, tk, LANES), lambda bi, hi, i: (bi, hi, 0, 0, 0)),
                  pl.BlockSpec((1, 1, nkt, V_ROWS, tk), lambda bi, hi, i: (bi, hi, 0, 0, 0)),
                  pl.BlockSpec((4, LANES // 2), lambda bi, hi, i: (0, 0)),
                  pl.BlockSpec((1, LANES), lambda bi, hi, i: (0, 0))],
        out_specs=pl.BlockSpec((1, tb, LANES), lambda bi, hi, i: (bi, i, hi)),
        out_shape=jax.ShapeDtypeStruct((b, n, dq), BF16),
        scratch_shapes=[pltpu.VMEM((nq, 2, tq, LANES), BF16),
                        pltpu.VMEM((2, tk, tq), F32), pltpu.VMEM((2, tk, tq), F32),
                        pltpu.VMEM((2, 1, tq), F32), pltpu.VMEM((2, 1, tq), F32),
                        pltpu.VMEM((nq, 2, 1, tq), F32), pltpu.VMEM((nq, 2, V_ROWS, tq), F32)],
        compiler_params=_cparams("parallel", "parallel", "arbitrary"),
        name="diff_attention",
    )(q, kt, vt, lam_vecs, subln_g.reshape(1, LANES))


def _key_value_tiles(k, v, tk):
    b, nk, _ = k.shape
    h = ATTN_HEADS
    nkt = nk // tk
    kt = k.reshape(b, nkt, tk, h, LANES).transpose(0, 3, 1, 2, 4)
    v5 = v.reshape(b, nkt, tk, h, LANES).transpose(0, 3, 1, 4, 2)
    ones = jnp.ones((b, h, nkt, 1, tk), BF16)
    zeros = jnp.zeros((b, h, nkt, V_ROWS - LANES - 1, tk), BF16)
    return kt, jnp.concatenate([v5, ones, zeros], axis=3)


def _resid_out(x_ref, mod_ref, y, o_ref):
    o_ref[0] = x_ref[0] + mod_ref[0, 2:3, :] * y


def _attn_out_kernel(a_ref, w_ref, x_ref, mod_ref, o_ref):
    _resid_out(x_ref, mod_ref, _dot(a_ref[0], w_ref[...]), o_ref)


def attn_out(a, w, x, mod, tm=512):
    b, n, d = x.shape
    tm = _row_tile(n, tm)
    row = pl.BlockSpec((1, tm, d), lambda bi, i: (bi, i, 0))
    return pl.pallas_call(
        _attn_out_kernel,
        grid=(b, n // tm),
        in_specs=[row, pl.BlockSpec((d, d), lambda bi, i: (0, 0)), row,
                  pl.BlockSpec((1, MOD_ROWS, d), lambda bi, i: (bi, 0, 0))],
        out_specs=row,
        out_shape=jax.ShapeDtypeStruct((b, n, d), F32),
        compiler_params=_cparams("parallel", "parallel"),
        name="attn_out",
    )(a, w, x, mod)


def _hyena_out_kernel(y_ref, vg_ref, x0_ref, skip_ref, w_ref, b_ref, x_ref, mod_ref, o_ref):
    a = ((y_ref[0] + vg_ref[0] * skip_ref[...]) * x0_ref[0]).astype(BF16)
    _resid_out(x_ref, mod_ref, _dot(a, w_ref[...]) + b_ref[...], o_ref)


def hyena_out(y, vg, x0, skip, w, bias, x, mod, tm=512):
    b, n, d = x.shape
    tm = _row_tile(n, tm)
    row = pl.BlockSpec((1, tm, d), lambda bi, i: (bi, i, 0))
    vec = pl.BlockSpec((1, d), lambda bi, i: (0, 0))
    return pl.pallas_call(
        _hyena_out_kernel,
        grid=(b, n // tm),
        in_specs=[row, row, row, vec, pl.BlockSpec((d, d), lambda bi, i: (0, 0)), vec, row,
                  pl.BlockSpec((1, MOD_ROWS, d), lambda bi, i: (bi, 0, 0))],
        out_specs=row,
        out_shape=jax.ShapeDtypeStruct((b, n, d), F32),
        compiler_params=_cparams("parallel", "parallel"),
        name="hyena_out",
    )(y, vg, x0, skip.reshape(1, d), w, bias.reshape(1, d), x, mod)


def _fill_window(ext_ref, prev_ref, cur_ref, next_ref, halo, tm, i, last):
    ext_ref[0:halo, :] = jnp.where(i == 0, 0.0, prev_ref[0])
    ext_ref[halo:halo + tm, :] = cur_ref[0]
    ext_ref[halo + tm:halo + tm + halo, :] = jnp.where(i == last, 0.0, next_ref[0])


def _dwconv(ext_ref, w_ref, halo, tm, c0, c1):
    width = w_ref.shape[0]
    pad = (width - 1) // 2
    acc = None
    for j in range(width):
        term = w_ref[j:j + 1, c0:c1] * ext_ref[pl.ds(halo - pad + j, tm), c0:c1]
        acc = term if acc is None else acc + term
    return acc


def _halo_specs(tm, halo, n, c):
    per = tm // halo
    nblk = n // halo
    prev = pl.BlockSpec((1, halo, c), lambda bi, i: (bi, jnp.maximum(i * per - 1, 0), 0))
    cur = pl.BlockSpec((1, tm, c), lambda bi, i: (bi, i, 0))
    nxt = pl.BlockSpec((1, halo, c), lambda bi, i: (bi, jnp.minimum((i + 1) * per, nblk - 1), 0))
    return [prev, cur, nxt]


def _hyena_gate_kernel(prev_ref, cur_ref, next_ref, w_ref, b_ref, x0_ref, vg_ref, ext_ref, *, halo, tm, d, last):
    i = pl.program_id(1)
    _fill_window(ext_ref, prev_ref, cur_ref, next_ref, halo, tm, i, last)
    x0_ref[0] = _dwconv(ext_ref, w_ref, halo, tm, 0, d) + b_ref[:, 0:d]
    x1 = _dwconv(ext_ref, w_ref, halo, tm, d, 2 * d) + b_ref[:, d:2 * d]
    v = _dwconv(ext_ref, w_ref, halo, tm, 2 * d, 3 * d) + b_ref[:, 2 * d:3 * d]
    vg_ref[0] = v * x1


def hyena_gate(u, w_short, b_short, tm=256):
    b, n, c = u.shape
    d = c // 3
    tm = _row_tile(n, tm)
    halo = SUBLANES
    row = pl.BlockSpec((1, tm, d), lambda bi, i: (bi, i, 0))
    out = jax.ShapeDtypeStruct((b, n, d), F32)
    return pl.pallas_call(
        functools.partial(_hyena_gate_kernel, halo=halo, tm=tm, d=d, last=n // tm - 1),
        grid=(b, n // tm),
        in_specs=_halo_specs(tm, halo, n, c) + [pl.BlockSpec(w_short.shape, lambda bi, i: (0, 0)),
                                                pl.BlockSpec((1, c), lambda bi, i: (0, 0))],
        out_specs=[row, row],
        out_shape=[out, out],
        scratch_shapes=[pltpu.VMEM((tm + 2 * halo, c), F32)],
        compiler_params=_cparams("parallel", "parallel"),
        name="hyena_gate",
    )(u, u, u, w_short, b_short.reshape(1, c))


def _conformer_out_kernel(prev_ref, cur_ref, next_ref, wdw_ref, bdw_ref, lg_ref, lb_ref, w_ref, b_ref,
                          x_ref, mod_ref, o_ref, ext_ref, *, halo, tm, d, last):
    i = pl.program_id(1)
    _fill_window(ext_ref, prev_ref, cur_ref, next_ref, halo, tm, i, last)
    u = _dwconv(ext_ref, wdw_ref, halo, tm, 0, d) + bdw_ref[...]
    mu = jnp.mean(u, axis=-1, keepdims=True)
    uc = u - mu
    var = jnp.mean(uc * uc, axis=-1, keepdims=True)
    z = _silu(uc * lax.rsqrt(var + LN_EPS) * lg_ref[...] + lb_ref[...]).astype(BF16)
    _resid_out(x_ref, mod_ref, _dot(z, w_ref[...]) + b_ref[...], o_ref)


def conformer_out(u, w_dw, b_dw, ln_g, ln_b, w, bias, x, mod, tm=256):
    b, n, d = x.shape
    tm = _row_tile(n, tm)
    halo = 2 * SUBLANES
    assert (w_dw.shape[0] - 1) // 2 <= halo
    row = pl.BlockSpec((1, tm, d), lambda bi, i: (bi, i, 0))
    vec = pl.BlockSpec((1, d), lambda bi, i: (0, 0))
    return pl.pallas_call(
        functools.partial(_conformer_out_kernel, halo=halo, tm=tm, d=d, last=n // tm - 1),
        grid=(b, n // tm),
        in_specs=_halo_specs(tm, halo, n, d) + [pl.BlockSpec(w_dw.shape, lambda bi, i: (0, 0)), vec, vec, vec,
                                                pl.BlockSpec((d, d), lambda bi, i: (0, 0)), vec, row,
                                                pl.BlockSpec((1, MOD_ROWS, d), lambda bi, i: (bi, 0, 0))],
        out_specs=row,
        out_shape=jax.ShapeDtypeStruct((b, n, d), F32),
        scratch_shapes=[pltpu.VMEM((tm + 2 * halo, d), F32)],
        compiler_params=_cparams("parallel", "parallel"),
        name="conformer_out",
    )(u, u, u, w_dw, b_dw.reshape(1, d), ln_g.reshape(1, d), ln_b.reshape(1, d), w, bias.reshape(1, d), x, mod)


def _modglu_kernel(x_ref, mod_ref, g_ref, wa_ref, wg_ref, ba_ref, bg_ref, o_ref):
    h = _modulate(x_ref[0], g_ref[...], mod_ref[0, 0:1, :], mod_ref[0, 1:2, :]).astype(BF16)
    a = _dot(h, wa_ref[...]) + ba_ref[...]
    gt = _dot(h, wg_ref[...]) + bg_ref[...]
    o_ref[0] = a * jax.nn.sigmoid(gt)


def modglu(x, mod, g, w, bias, tm=512):
    b, n, d = x.shape
    tm = _row_tile(n, tm)
    row = pl.BlockSpec((1, tm, d), lambda bi, i: (bi, i, 0))
    bias2 = bias.reshape(1, 2 * d)
    return pl.pallas_call(
        _modglu_kernel,
        grid=(b, n // tm),
        in_specs=[row, pl.BlockSpec((1, MOD_ROWS, d), lambda bi, i: (bi, 0, 0)),
                  pl.BlockSpec((1, d), lambda bi, i: (0, 0)),
                  pl.BlockSpec((d, d), lambda bi, i: (0, 0)), pl.BlockSpec((d, d), lambda bi, i: (0, 1)),
                  pl.BlockSpec((1, d), lambda bi, i: (0, 0)), pl.BlockSpec((1, d), lambda bi, i: (0, 1))],
        out_specs=row,
        out_shape=jax.ShapeDtypeStruct((b, n, d), F32),
        compiler_params=_cparams("parallel", "parallel"),
        name="modglu",
    )(x, mod, g.reshape(1, d), w, w, bias2, bias2)


def _filter_kernel(emb_ref, t_ref, w1_ref, b1_ref, w2_ref, b2_ref, w3_ref, b3_ref, fr_ref, w4_ref, dl_ref, o_ref):
    fr = fr_ref[...]
    hdn = jnp.sin(fr * (_dot32(emb_ref[...], w1_ref[...]) + b1_ref[...]))
    hdn = jnp.sin(fr * (_dot32(hdn, w2_ref[...]) + b2_ref[...]))
    hdn = jnp.sin(fr * (_dot32(hdn, w3_ref[...]) + b3_ref[...]))
    h = _dot32(hdn, w4_ref[...]) * jnp.exp(-t_ref[...] * dl_ref[...])
    row = lax.broadcasted_iota(jnp.int32, h.shape, 0)
    col = lax.broadcasted_iota(jnp.int32, h.shape, 1)
    drop = (row == 0) & (pl.program_id(0) == 0) & (col >= h.shape[1] // 2)
    o_ref[...] = jnp.where(drop, 0.0, h)


def hyena_filter(L, w1, b1, w2, b2, w3, b3, freq, w4, d_model, tl=256):
    bands = (w1.shape[0] - 1) // 2
    hid = w1.shape[1]
    t = jnp.linspace(0.0, 1.0, L, dtype=F32)[:, None]
    wpos = (2.0 * math.pi / L) * jnp.arange(L, dtype=F32)
    bnd = jnp.linspace(1e-4, bands - 1, bands, dtype=F32)
    fw = wpos[:, None] * bnd[None, :]
    emb = jnp.concatenate([t, jnp.cos(fw), -jnp.sin(fw)], axis=-1)
    max_decay = math.log(HY_DECAY_TARGET) / HY_FAST_DECAY
    min_decay = math.log(HY_DECAY_TARGET) / HY_SLOW_DECAY
    deltas = jnp.abs(jnp.linspace(min_decay, max_decay, d_model, dtype=F32))
    dl2 = jnp.concatenate([deltas, deltas])[None, :]
    tl = _row_tile(L, tl)
    td = 2 * d_model
    full = lambda a: pl.BlockSpec(a.shape, lambda i, j: (0, 0))
    vec = lambda a: a.reshape(1, -1)
    args = [emb, t, w1, vec(b1), w2, vec(b2), w3, vec(b3), vec(freq)]
    return pl.pallas_call(
        _filter_kernel,
        grid=(L // tl, 2 * d_model // td),
        in_specs=[pl.BlockSpec((tl, emb.shape[1]), lambda i, j: (i, 0)), pl.BlockSpec((tl, 1), lambda i, j: (i, 0))]
                 + [full(a) for a in args[2:]]
                 + [pl.BlockSpec((hid, td), lambda i, j: (0, j)), pl.BlockSpec((1, td), lambda i, j: (0, j))],
        out_specs=pl.BlockSpec((tl, td), lambda i, j: (i, j)),
        out_shape=jax.ShapeDtypeStruct((L, 2 * d_model), F32),
        compiler_params=_cparams("parallel", "parallel"),
        name="hyena_filter",
    )(*args, w4, dl2)


FFT_N2 = 128


def _split_bf16(a):
    hi = a.astype(BF16)
    return hi, (a - hi.astype(F32)).astype(BF16)


def _dotp_split(ah, al, b):
    bh, bl = _split_bf16(b)
    return _dot(ah, bh) + (_dot(ah, bl) + _dot(al, bh))


def _dotp(a, b):
    return _dotp_split(*_split_bf16(a), b)


def _to_block_major(x, n2):
    bx, L, c = x.shape
    r = L // n2
    return x.reshape(bx, r, n2 // SUBLANES, SUBLANES, c).transpose(0, 2, 1, 3, 4).reshape(
        bx, n2 // SUBLANES, r * SUBLANES, c)


def _from_block_major(y, n2):
    bx, na, rows8, c = y.shape
    r = rows8 // SUBLANES
    return y.reshape(bx, na, r, SUBLANES, c).transpose(0, 2, 1, 3, 4).reshape(bx, r * n2, c)


def _major_kernel(m_ref, x_ref, o_ref):
    mh, ml = _split_bf16(m_ref[...])
    rows_out, rows_in = m_ref.shape
    x2 = x_ref.at[0, 0]
    o2 = o_ref.at[0, 0]
    for s in range(SUBLANES):
        y = _dotp_split(mh, ml, x2[pl.ds(s, rows_in, stride=SUBLANES), :])
        o2[pl.ds(s, rows_out, stride=SUBLANES), :] = y


def dft_major(mat, x):
    bx, na, rows8, c = x.shape
    m, r = mat.shape
    assert rows8 == r * SUBLANES
    tc = LANES
    return pl.pallas_call(
        _major_kernel,
        grid=(bx, na, c // tc),
        in_specs=[pl.BlockSpec((m, r), lambda bi, a, j: (0, 0)),
                  pl.BlockSpec((1, 1, rows8, tc), lambda bi, a, j: (bi, a, 0, j))],
        out_specs=pl.BlockSpec((1, 1, m * SUBLANES, tc), lambda bi, a, j: (bi, a, 0, j)),
        out_shape=jax.ShapeDtypeStruct((bx, na, m * SUBLANES, c), F32),
        compiler_params=_cparams("parallel", "parallel", "parallel"),
        name="dft_major",
    )(mat, x)


def _twiddled_block(f_ref, tw_ref):
    fr, fi = f_ref[0], f_ref[1]
    twr, twi = tw_ref[0, 0:1, :], tw_ref[0, 1:2, :]
    gr = fr * twr - fi * twi
    gi = fr * twi + fi * twr
    return jnp.concatenate([jnp.concatenate([gr, -gi], axis=1), jnp.concatenate([gi, gr], axis=1)], axis=0)


def _gather_k1(ref, n2):
    return jnp.concatenate([ref[0, :, p, 0].reshape(n2, -1) for p in range(2)], axis=0)


def _scatter_k1(ref, x, n2):
    for p in range(2):
        ref[0, :, p, 0] = x[p * n2:(p + 1) * n2].reshape(n2 // SUBLANES, SUBLANES, -1)


def _spec_fwd_kernel(f_ref, tw_ref, a_ref, o_ref):
    n2 = f_ref.shape[1]
    g = _twiddled_block(f_ref, tw_ref)
    _scatter_k1(o_ref, _dotp(g, _gather_k1(a_ref, n2)), n2)


def _spec_mid_kernel(f_ref, tw_ref, a_ref, hf_ref, hb_ref, o_ref):
    n2 = f_ref.shape[1]
    g = _twiddled_block(f_ref, tw_ref)
    x = _dotp(g, _gather_k1(a_ref, n2))
    hf = _gather_k1(hf_ref, n2)
    hb = _gather_k1(hb_ref, n2)
    xr, xi = x[:n2], x[n2:]
    hr = hf[:n2] + hb[:n2]
    hi = hf[n2:] - hb[n2:]
    z = jnp.concatenate([xr * hr - xi * hi, xr * hi + xi * hr], axis=0)
    _scatter_k1(o_ref, _dotp(g.T, z), n2)


def spec_stage(fmat, tw, a, h=None, tc=1024):
    bx, na, rows8, c = a.shape
    n2 = na * SUBLANES
    n1 = rows8 // (2 * SUBLANES)
    tc = _row_tile(c, tc)
    six = lambda t: t.reshape(t.shape[0], na, 2, n1, SUBLANES, t.shape[-1])
    blk = lambda: pl.BlockSpec((1, na, 2, 1, SUBLANES, tc), lambda k, j, bi: (bi, 0, 0, k, 0, j))
    in_specs = [pl.BlockSpec((2, n2, n2), lambda k, j, bi: (0, 0, 0)),
                pl.BlockSpec((1, 2, n2), lambda k, j, bi: (k, 0, 0)), blk()]
    args = [fmat, tw, six(a)]
    if h is not None:
        nj = c // tc
        in_specs += [pl.BlockSpec((1, na, 2, 1, SUBLANES, tc), lambda k, j, bi: (0, 0, 0, k, 0, j)),
                     pl.BlockSpec((1, na, 2, 1, SUBLANES, tc), lambda k, j, bi: (0, 0, 0, k, 0, nj + j))]
        args += [six(h), six(h)]
    out = pl.pallas_call(
        _spec_fwd_kernel if h is None else _spec_mid_kernel,
        grid=(n1, c // tc, bx),
        in_specs=in_specs,
        out_specs=blk(),
        out_shape=jax.ShapeDtypeStruct((bx, na, 2, n1, SUBLANES, c), F32),
        compiler_params=_cparams("parallel", "parallel", "arbitrary"),
        name="dft_minor",
    )(*args)
    return out.reshape(a.shape)


def _dft_constants(n1, n2):
    n = n1 * n2
    k1 = np.arange(n1, dtype=np.float64)
    ang1 = -2.0 * np.pi * np.outer(k1, k1) / n1
    w1r, w1i = np.cos(ang1), np.sin(ang1)
    k2 = np.arange(n2, dtype=np.float64)
    ang2 = -2.0 * np.pi * np.outer(k2, k2) / n2
    fmat = np.stack([np.cos(ang2), np.sin(ang2)])
    angt = -2.0 * np.pi * np.outer(k1, k2) / n
    tw = np.stack([np.cos(angt), np.sin(angt)], axis=1)
    fwd_half = np.concatenate([w1r[:, :n1 // 2], w1i[:, :n1 // 2]], axis=0)
    inv_half = np.concatenate([w1r[:n1 // 2], w1i[:n1 // 2]], axis=1) / n
    f = lambda a: jnp.asarray(a, F32)
    return f(fwd_half), f(inv_half), f(fmat), f(tw)


def long_conv(v, taps):
    b, L, c = v.shape
    n2 = FFT_N2
    n1 = 2 * L // n2
    fwd_half, inv_half, fmat, tw = _dft_constants(n1, n2)
    hs = spec_stage(fmat, tw, dft_major(fwd_half, _to_block_major(taps[None], n2)))
    a = dft_major(fwd_half, _to_block_major(v, n2))
    y = dft_major(inv_half, spec_stage(fmat, tw, a, hs))
    return _from_block_major(y, n2)


def _dense_conv_kernel(f_ref, g_ref, v_ref, hf_ref, hb_ref, o_ref):
    n = f_ref.shape[0] // 2
    f = f_ref[...]
    x = _dotp(f, v_ref[0])
    hf = _dotp(f, hf_ref[...])
    hb = _dotp(f, hb_ref[...])
    xr, xi = x[:n], x[n:]
    hr = hf[:n] + hb[:n]
    hi = hf[n:] - hb[n:]
    z = jnp.concatenate([xr * hr - xi * hi, xr * hi + xi * hr], axis=0)
    o_ref[0] = _dotp(g_ref[...], z)


def dense_long_conv(v, taps, tc=512):
    b, L, c = v.shape
    n = 2 * L
    ang = 2.0 * np.pi * np.outer(np.arange(n, dtype=np.float64), np.arange(L, dtype=np.float64)) / n
    fmat = jnp.asarray(np.concatenate([np.cos(ang), -np.sin(ang)], axis=0), F32)
    gmat = jnp.asarray(np.concatenate([np.cos(ang.T), -np.sin(ang.T)], axis=1) / n, F32)
    tc = _row_tile(c, tc)
    nj = c // tc
    return pl.pallas_call(
        _dense_conv_kernel,
        grid=(b, nj),
        in_specs=[pl.BlockSpec((2 * n, L), lambda bi, j: (0, 0)), pl.BlockSpec((L, 2 * n), lambda bi, j: (0, 0)),
                  pl.BlockSpec((1, L, tc), lambda bi, j: (bi, 0, j)),
                  pl.BlockSpec((L, tc), lambda bi, j: (0, j)), pl.BlockSpec((L, tc), lambda bi, j: (0, nj + j))],
        out_specs=pl.BlockSpec((1, L, tc), lambda bi, j: (bi, 0, j)),
        out_shape=jax.ShapeDtypeStruct((b, L, c), F32),
        compiler_params=_cparams("parallel", "parallel"),
        name="dense_long_conv",
    )(fmat, gmat, v, taps, taps)


def _pad_mod(m3):
    return jnp.pad(m3, ((0, 0), (0, MOD_ROWS - 3), (0, 0)))


def kernel(x, c, ctx, c_ctx, w_mod, b_mod, norm_g, w_ffn_in, w_ffn_out, attn_w_qkv, attn_w_o, attn_lambda,
           attn_subln_g, hy_w_in, hy_b_in, hy_w_short, hy_b_short, hy_f_w1, hy_f_b1, hy_f_w2, hy_f_b2, hy_f_w3,
           hy_f_b3, hy_f_freq, hy_f_w4, hy_skip, hy_w_out, hy_b_out, cv_w_pw1, cv_b_pw1, cv_w_dw, cv_b_dw,
           cv_ln_g, cv_ln_b, cv_w_pw2, cv_b_pw2, final_g):
    bsz, n_lat, d = x.shape
    n_ctx = ctx.shape[1]
    depth = w_mod.shape[0]
    assert bsz + 1 <= MOD_ROWS

    rows = jnp.concatenate([c, c_ctx[None, :], jnp.zeros((MOD_ROWS - bsz - 1, d), F32)], axis=0)
    table = mod_table(rows, w_mod, b_mod).reshape(depth, MOD_ROWS, N_MOD, d)

    def mods(i, s, latent):
        r = table[i, :bsz] if latent else jnp.broadcast_to(table[i, bsz:bsz + 1], (bsz, N_MOD, d))
        return _pad_mod(r[:, 3 * s:3 * s + 3])

    bf = lambda w: w.astype(BF16)
    xc = ctx
    tk_lat = 640 if (n_lat + n_ctx) % 640 == 0 else n_ctx
    rope = rope_tables(n_lat)

    for i in range(depth):
        kind = i % N_MIXERS
        j = i // N_MIXERS
        last = i == depth - 1
        ctx_in_use = (not last) or kind == 0
        ctx_advance = not last
        w_in0, w_out0 = bf(w_ffn_in[i, 0]), bf(w_ffn_out[i, 0])
        w_in1, w_out1 = bf(w_ffn_in[i, 1]), bf(w_ffn_out[i, 1])

        x = ffn(x, mods(i, 0, True), norm_g[i, 0], w_in0, w_out0)
        if ctx_in_use:
            xc = ffn(xc, mods(i, 0, False), norm_g[i, 0], w_in0, w_out0)

        ml, mc = mods(i, 1, True), mods(i, 1, False)
        if kind == 0:
            lam_init = 0.8 - 0.6 * math.exp(-0.3 * i)
            wqkv = bf(attn_w_qkv[j])
            wo = bf(attn_w_o[j])
            qkv_l = modproj(x, ml, norm_g[i, 1], wqkv, None, BF16, rope=rope, rope_blocks=2)
            qkv_c = modproj(xc, mc, norm_g[i, 1], wqkv, None, BF16)
            k_all = jnp.concatenate([qkv_l[:, :, d:2 * d], qkv_c[:, :, d:2 * d]], axis=1)
            v_all = jnp.concatenate([qkv_l[:, :, 2 * d:], qkv_c[:, :, 2 * d:]], axis=1)
            kt, vt = _key_value_tiles(k_all, v_all, tk_lat)
            o_l = diff_attention(qkv_l[:, :, :d], kt, vt, attn_lambda[j], attn_subln_g[j], lam_init)
            x = attn_out(o_l, wo, x, ml)
            if ctx_advance:
                ktc, vtc = _key_value_tiles(qkv_c[:, :, d:2 * d], qkv_c[:, :, 2 * d:], n_ctx)
                o_c = diff_attention(qkv_c[:, :, :d], ktc, vtc, attn_lambda[j], attn_subln_g[j], lam_init)
                yc_fn = lambda xcur: attn_out(o_c, wo, xcur, mc)
        elif kind == 1:
            w_in, w_o = bf(hy_w_in[j]), bf(hy_w_out[j])
            filt = (hy_f_w1[j], hy_f_b1[j], hy_f_w2[j], hy_f_b2[j], hy_f_w3[j], hy_f_b3[j], hy_f_freq[j], hy_f_w4[j])

            def hyena(xs, mod, L):
                u = modproj(xs, mod, norm_g[i, 1], w_in, hy_b_in[j], F32)
                x0, vg = hyena_gate(u, hy_w_short[j], hy_b_short[j])
                taps = hyena_filter(L, *filt, d)
                y = long_conv(vg, taps) if L == n_lat else dense_long_conv(vg, taps)
                return y, vg, x0

            y_l, vg_l, x0_l = hyena(x, ml, n_lat)
            x = hyena_out(y_l, vg_l, x0_l, hy_skip[j], w_o, hy_b_out[j], x, ml)
            if ctx_advance:
                y_c, vg_c, x0_c = hyena(xc, mc, n_ctx)
                yc_fn = lambda xcur: hyena_out(y_c, vg_c, x0_c, hy_skip[j], w_o, hy_b_out[j], xcur, mc)
        else:
            w1, w2 = bf(cv_w_pw1[j]), bf(cv_w_pw2[j])
            u_l = modglu(x, ml, norm_g[i, 1], w1, cv_b_pw1[j])
            x = conformer_out(u_l, cv_w_dw[j], cv_b_dw[j], cv_ln_g[j], cv_ln_b[j], w2, cv_b_pw2[j], x, ml)
            if ctx_advance:
                u_c = modglu(xc, mc, norm_g[i, 1], w1, cv_b_pw1[j])
                yc_fn = lambda xcur: conformer_out(u_c, cv_w_dw[j], cv_b_dw[j], cv_ln_g[j], cv_ln_b[j], w2,
                                                   cv_b_pw2[j], xcur, mc)

        x = ffn(x, mods(i, 2, True), norm_g[i, 2], w_in1, w_out1, final_g=final_g if last else None)
        if ctx_advance:
            xc = yc_fn(xc)
            xc = ffn(xc, mods(i, 2, False), norm_g[i, 2], w_in1, w_out1)
    return x
```

```python
import functools
import math

import numpy as np
import jax
import jax.numpy as jnp
from jax import lax
from jax.experimental import pallas as pl
from jax.experimental.pallas import tpu as pltpu

F32 = jnp.float32
BF16 = jnp.bfloat16

GRID_W = 64
ATTN_HEADS = 8
ROPE_THETA = 10000.0
HY_FAST_DECAY = 0.3
HY_SLOW_DECAY = 1.5
HY_DECAY_TARGET = 1e-2
EPS = 1e-6
LN_EPS = 1e-5
N_MIXERS = 3
N_MOD = 9

LANES = 128
SUBLANES = 8
BF16_SUBLANES = 16
VMEM_LIMIT_BYTES = 56 * 1024 * 1024

MOD_ROWS = SUBLANES
HIGHEST = lax.Precision.HIGHEST


def _cparams(*sem):
    return pltpu.CompilerParams(dimension_semantics=sem, vmem_limit_bytes=VMEM_LIMIT_BYTES)


def _row_tile(n, want):
    t = min(n, want)
    assert n % t == 0, (n, t)
    return t


def _dot(a, b):
    return jnp.dot(a, b, preferred_element_type=F32)


def _dot32(a, b):
    return jnp.dot(a, b, preferred_element_type=F32, precision=HIGHEST)


def _rmsnorm(x, g):
    return x * lax.rsqrt(jnp.mean(x * x, axis=-1, keepdims=True) + EPS) * g


def _modulate(x, g, shift, scale):
    return _rmsnorm(x, g) * (1.0 + scale) + shift


def _silu(x):
    return x * jax.nn.sigmoid(x)


def _mod_kernel(r_ref, w_ref, b_ref, o_ref):
    r = _silu(r_ref[...]).astype(BF16)
    o_ref[0] = _dot(r, w_ref[0].astype(BF16)) + b_ref[0]


def mod_table(rows, w_mod, b_mod):
    depth, d, nm = w_mod.shape
    tn = nm // N_MOD
    return pl.pallas_call(
        _mod_kernel,
        grid=(depth, nm // tn),
        in_specs=[pl.BlockSpec((MOD_ROWS, d), lambda i, j: (0, 0)),
                  pl.BlockSpec((1, d, tn), lambda i, j: (i, 0, j)),
                  pl.BlockSpec((1, 1, tn), lambda i, j: (i, 0, j))],
        out_specs=pl.BlockSpec((1, MOD_ROWS, tn), lambda i, j: (i, 0, j)),
        out_shape=jax.ShapeDtypeStruct((depth, MOD_ROWS, nm), F32),
        compiler_params=_cparams("parallel", "parallel"),
        name="mod_table",
    )(rows, w_mod, b_mod.reshape(depth, 1, nm))


FFN_CHUNKS = 2


def _ffn_kernel(x_ref, mod_ref, g_ref, wg_ref, wu_ref, wo_ref, *rest, final):
    if final:
        fg_ref, o_ref = rest
    else:
        (o_ref,) = rest
    x = x_ref[0]
    h = _modulate(x, g_ref[...], mod_ref[0, 0:1, :], mod_ref[0, 1:2, :]).astype(BF16)
    tf = wo_ref.shape[0] // FFN_CHUNKS
    acc = None
    for c in range(FFN_CHUNKS):
        gate = _dot(h, wg_ref[:, c * tf:(c + 1) * tf])
        up = _dot(h, wu_ref[:, c * tf:(c + 1) * tf])
        act = (_silu(gate) * up).astype(BF16)
        part = _dot(act, wo_ref[c * tf:(c + 1) * tf, :])
        acc = part if acc is None else acc + part
    xn = x + 0.5 * mod_ref[0, 2:3, :] * acc
    if final:
        xn = _rmsnorm(xn, fg_ref[...])
    o_ref[0] = xn


def ffn(x, mod, g, w_in, w_out, final_g=None, tm=1024):
    b, n, d = x.shape
    ff = w_out.shape[0]
    tm = _row_tile(n, tm)
    assert (ff // FFN_CHUNKS) % LANES == 0
    final = final_g is not None
    resident = dict(pipeline_mode=pl.Buffered(1))
    in_specs = [pl.BlockSpec((1, tm, d), lambda bi, i: (bi, i, 0)),
                pl.BlockSpec((1, MOD_ROWS, d), lambda bi, i: (bi, 0, 0)),
                pl.BlockSpec((1, d), lambda bi, i: (0, 0)),
                pl.BlockSpec((d, ff), lambda bi, i: (0, 0), **resident),
                pl.BlockSpec((d, ff), lambda bi, i: (0, 1), **resident),
                pl.BlockSpec((ff, d), lambda bi, i: (0, 0), **resident)]
    args = [x, mod, g.reshape(1, d), w_in, w_in, w_out]
    if final:
        in_specs.append(pl.BlockSpec((1, d), lambda bi, i: (0, 0)))
        args.append(final_g.reshape(1, d))
    return pl.pallas_call(
        functools.partial(_ffn_kernel, final=final),
        grid=(b, n // tm),
        in_specs=in_specs,
        out_specs=pl.BlockSpec((1, tm, d), lambda bi, i: (bi, i, 0)),
        out_shape=jax.ShapeDtypeStruct((b, n, d), F32),
        compiler_params=_cparams("parallel", "parallel"),
        name="ffn",
    )(*args)


def _rope_cols(y, cos, sa, sb):
    outs = []
    for c in range(y.shape[1] // LANES):
        yc = y[:, c * LANES:(c + 1) * LANES]
        outs.append(yc * cos + pltpu.roll(yc, LANES - 16, 1) * sa + pltpu.roll(yc, 16, 1) * sb)
    return jnp.concatenate(outs, axis=1)


def _modproj_kernel(*refs, has_bias, rope_blocks):
    x_ref, mod_ref, g_ref, w_ref = refs[:4]
    rest = refs[4:]
    if has_bias:
        b_ref, rest = rest[0], rest[1:]
    if rope_blocks:
        cos_ref, sa_ref, sb_ref, o_ref, h_scr = rest
    else:
        o_ref, h_scr = rest
    j = pl.program_id(2)

    @pl.when(j == 0)
    def _():
        h = _modulate(x_ref[0], g_ref[...], mod_ref[0, 0:1, :], mod_ref[0, 1:2, :])
        h_scr[...] = h.astype(BF16)

    y = _dot(h_scr[...], w_ref[...])
    if has_bias:
        y = y + b_ref[...]
    if rope_blocks:
        @pl.when(j < rope_blocks)
        def _():
            o_ref[0] = _rope_cols(y, cos_ref[...], sa_ref[...], sb_ref[...]).astype(o_ref.dtype)

        @pl.when(j >= rope_blocks)
        def _():
            o_ref[0] = y.astype(o_ref.dtype)
    else:
        o_ref[0] = y.astype(o_ref.dtype)


def modproj(x, mod, g, w, bias, out_dtype, rope=None, rope_blocks=0, tm=512, tn=1024):
    b, n, d = x.shape
    nout = w.shape[1]
    tm = _row_tile(n, tm)
    in_specs = [pl.BlockSpec((1, tm, d), lambda bi, i, j: (bi, i, 0)),
                pl.BlockSpec((1, MOD_ROWS, d), lambda bi, i, j: (bi, 0, 0)),
                pl.BlockSpec((1, d), lambda bi, i, j: (0, 0)),
                pl.BlockSpec((d, tn), lambda bi, i, j: (0, j))]
    args = [x, mod, g.reshape(1, d), w]
    if bias is not None:
        in_specs.append(pl.BlockSpec((1, tn), lambda bi, i, j: (0, j)))
        args.append(bias.reshape(1, nout))
    if rope_blocks:
        in_specs += [pl.BlockSpec((tm, LANES), lambda bi, i, j: (i, 0))] * 3
        args += list(rope)
    return pl.pallas_call(
        functools.partial(_modproj_kernel, has_bias=bias is not None, rope_blocks=rope_blocks),
        grid=(b, n // tm, nout // tn),
        in_specs=in_specs,
        out_specs=pl.BlockSpec((1, tm, tn), lambda bi, i, j: (bi, i, j)),
        out_shape=jax.ShapeDtypeStruct((b, n, nout), out_dtype),
        scratch_shapes=[pltpu.VMEM((tm, d), BF16)],
        compiler_params=_cparams("parallel", "parallel", "arbitrary"),
        name="modproj",
    )(*args)


def rope_tables(n):
    rows = n // GRID_W
    row = jnp.repeat(jnp.arange(rows), GRID_W).astype(F32)
    col = jnp.tile(jnp.arange(GRID_W), rows).astype(F32)
    quarter = 16
    half = 32
    inv = ROPE_THETA ** (-(2.0 * jnp.arange(quarter, dtype=F32)) / half)
    ang_r = row[:, None] * inv
    ang_c = col[:, None] * inv
    zero = jnp.zeros_like(ang_r)
    cr, sr, cc, sc = jnp.cos(ang_r), jnp.sin(ang_r), jnp.cos(ang_c), jnp.sin(ang_c)
    cos64 = jnp.concatenate([cr, cr, cc, cc], axis=1)
    sa64 = jnp.concatenate([-sr, zero, -sc, zero], axis=1)
    sb64 = jnp.concatenate([zero, sr, zero, sc], axis=1)
    tile2 = lambda t: jnp.concatenate([t, t], axis=1)
    return tile2(cos64), tile2(sa64), tile2(sb64)


V_ROWS = LANES + BF16_SUBLANES
PIPE_UNROLL = 8


def _attn_kernel(q_ref, k_ref, v_ref, lam_ref, g_ref, o_ref, qm_scr, sa_scr, sb_scr, ma_scr, mb_scr, m_scr, acc_scr, *,
                 nkt, nq, tq, lam_init, half):
    lane = lax.broadcasted_iota(jnp.int32, (tq, LANES), 1)
    for qi in range(nq):
        q = q_ref[0, qi * tq:(qi + 1) * tq, :].astype(F32) * (half ** -0.5 * math.log2(math.e))
        qm_scr[qi, 0] = jnp.where(lane < half, q, 0.0).astype(BF16)
        qm_scr[qi, 1] = jnp.where(lane >= half, q, 0.0).astype(BF16)
    m_scr[...] = jnp.full(m_scr.shape, -jnp.inf, F32)
    acc_scr[...] = jnp.zeros_like(acc_scr)

    def scores(u, bufs):
        s_ref, mt_ref = bufs
        qi, t = u // nkt, u % nkt
        k = k_ref[0, 0, t]
        for mi in range(2):
            s = lax.dot_general(k, qm_scr[qi, mi], (((1,), (1,)), ((), ())),
                                preferred_element_type=F32)
            s_ref[mi] = s
            mt_ref[mi] = jnp.max(s, axis=0, keepdims=True)

    def softmax_pv(u, bufs):
        s_ref, mt_ref = bufs
        qi, t = u // nkt, u % nkt
        vt = v_ref[0, 0, t]
        for mi in range(2):
            s = s_ref[mi]
            m_old = m_scr[qi, mi]
            m_new = jnp.maximum(m_old, mt_ref[mi])
            alpha = jnp.exp2(m_old - m_new)
            p = jnp.exp2((s - m_new).astype(BF16))
            acc_scr[qi, mi] = alpha * acc_scr[qi, mi] + _dot(vt, p)
            m_scr[qi, mi] = m_new

    units = nq * nkt
    bufs = ((sa_scr, ma_scr), (sb_scr, mb_scr))
    scores(0, bufs[0])

    def steps(first, count):
        for i in range(count):
            scores(first + i + 1, bufs[(i + 1) % 2])
            softmax_pv(first + i, bufs[i % 2])

    n_steps = units - 1
    n_loop = n_steps // PIPE_UNROLL

    def body(j, carry):
        steps(PIPE_UNROLL * j, PIPE_UNROLL)
        return carry

    lax.fori_loop(0, n_loop, body, 0)
    steps(n_loop * PIPE_UNROLL, n_steps - n_loop * PIPE_UNROLL)
    softmax_pv(units - 1, bufs[(units - 1) % 2])

    lv = lam_ref[...]
    lam = (jnp.exp(jnp.sum(lv[0:1] * lv[1:2], axis=1, keepdims=True))
           - jnp.exp(jnp.sum(lv[2:3] * lv[3:4], axis=1, keepdims=True)) + lam_init)
    for qi in range(nq):
        a1 = acc_scr[qi, 0]
        a2 = acc_scr[qi, 1]
        ot = a1[:LANES] / a1[LANES:LANES + 1] - lam * (a2[:LANES] / a2[LANES:LANES + 1])
        o = _rmsnorm(ot.T, g_ref[...]) * (1.0 - lam_init)
        o_ref[0, qi * tq:(qi + 1) * tq, :] = o.astype(o_ref.dtype)


def diff_attention(q, kt, vt, lam_vecs, subln_g, lam_init, tq=256, nq=4):
    b, n, dq = q.shape
    _, h, nkt, tk, _ = kt.shape
    tq = _row_tile(n, tq)
    nq = min(nq, n // tq)
    tb = nq * tq
    assert n % tb == 0
    return pl.pallas_call(
        functools.partial(_attn_kernel, nkt=nkt, nq=nq, tq=tq, lam_init=lam_init, half=LANES // 2),
        grid=(b, h, n // tb),
        in_specs=[pl.BlockSpec((1, tb, LANES), lambda bi, hi, i: (bi, i, hi)),
                  pl.BlockSpec((1, 1, nkt, tk, LANES), lambda bi, hi, i: (bi, hi, 0, 0, 0)),
                  pl.BlockSpec((1, 1, nkt, V_ROWS, tk), lambda bi, hi, i: (bi, hi, 0, 0, 0)),
                  pl.BlockSpec((4, LANES // 2), lambda bi, hi, i: (0, 0)),
                  pl.BlockSpec((1, LANES), lambda bi, hi, i: (0, 0))],
        out_specs=pl.BlockSpec((1, tb, LANES), lambda bi, hi, i: (bi, i, hi)),
        out_shape=jax.ShapeDtypeStruct((b, n, dq), BF16),
        scratch_shapes=[pltpu.VMEM((nq, 2, tq, LANES), BF16),
                        pltpu.VMEM((2, tk, tq), F32), pltpu.VMEM((2, tk, tq), F32),
                        pltpu.VMEM((2, 1, tq), F32), pltpu.VMEM((2, 1, tq), F32),
                        pltpu.VMEM((nq, 2, 1, tq), F32), pltpu.VMEM((nq, 2, V_ROWS, tq), F32)],
        compiler_params=_cparams("parallel", "parallel", "arbitrary"),
        name="diff_attention",
    )(q, kt, vt, lam_vecs, subln_g.reshape(1, LANES))


def _key_value_tiles(k, v, tk):
    b, nk, _ = k.shape
    h = ATTN_HEADS
    nkt = nk // tk
    kt = k.reshape(b, nkt, tk, h, LANES).transpose(0, 3, 1, 2, 4)
    v5 = v.reshape(b, nkt, tk, h, LANES).transpose(0, 3, 1, 4, 2)
    ones = jnp.ones((b, h, nkt, 1, tk), BF16)
    zeros = jnp.zeros((b, h, nkt, V_ROWS - LANES - 1, tk), BF16)
    return kt, jnp.concatenate([v5, ones, zeros], axis=3)


def _resid_out(x_ref, mod_ref, y, o_ref):
    o_ref[0] = x_ref[0] + mod_ref[0, 2:3, :] * y


def _attn_out_kernel(a_ref, w_ref, x_ref, mod_ref, o_ref):
    _resid_out(x_ref, mod_ref, _dot(a_ref[0], w_ref[...]), o_ref)


def attn_out(a, w, x, mod, tm=512):
    b, n, d = x.shape
    tm = _row_tile(n, tm)
    row = pl.BlockSpec((1, tm, d), lambda bi, i: (bi, i, 0))
    return pl.pallas_call(
        _attn_out_kernel,
        grid=(b, n // tm),
        in_specs=[row, pl.BlockSpec((d, d), lambda bi, i: (0, 0)), row,
                  pl.BlockSpec((1, MOD_ROWS, d), lambda bi, i: (bi, 0, 0))],
        out_specs=row,
        out_shape=jax.ShapeDtypeStruct((b, n, d), F32),
        compiler_params=_cparams("parallel", "parallel"),
        name="attn_out",
    )(a, w, x, mod)


def _hyena_out_kernel(y_ref, vg_ref, x0_ref, skip_ref, w_ref, b_ref, x_ref, mod_ref, o_ref):
    a = ((y_ref[0] + vg_ref[0] * skip_ref[...]) * x0_ref[0]).astype(BF16)
    _resid_out(x_ref, mod_ref, _dot(a, w_ref[...]) + b_ref[...], o_ref)


def hyena_out(y, vg, x0, skip, w, bias, x, mod, tm=512):
    b, n, d = x.shape
    tm = _row_tile(n, tm)
    row = pl.BlockSpec((1, tm, d), lambda bi, i: (bi, i, 0))
    vec = pl.BlockSpec((1, d), lambda bi, i: (0, 0))
    return pl.pallas_call(
        _hyena_out_kernel,
        grid=(b, n // tm),
        in_specs=[row, row, row, vec, pl.BlockSpec((d, d), lambda bi, i: (0, 0)), vec, row,
                  pl.BlockSpec((1, MOD_ROWS, d), lambda bi, i: (bi, 0, 0))],
        out_specs=row,
        out_shape=jax.ShapeDtypeStruct((b, n, d), F32),
        compiler_params=_cparams("parallel", "parallel"),
        name="hyena_out",
    )(y, vg, x0, skip.reshape(1, d), w, bias.reshape(1, d), x, mod)


def _fill_window(ext_ref, prev_ref, cur_ref, next_ref, halo, tm, i, last):
    ext_ref[0:halo, :] = jnp.where(i == 0, 0.0, prev_ref[0])
    ext_ref[halo:halo + tm, :] = cur_ref[0]
    ext_ref[halo + tm:halo + tm + halo, :] = jnp.where(i == last, 0.0, next_ref[0])


def _dwconv(ext_ref, w_ref, halo, tm, c0, c1, z_scr=None):
    width = w_ref.shape[0]
    pad = (width - 1) // 2
    offs = [halo - pad + j for j in range(width)]
    acc = None
    if z_scr is None:
        for j, o in enumerate(offs):
            term = w_ref[j:j + 1, c0:c1] * ext_ref[pl.ds(o, tm), c0:c1]
            acc = term if acc is None else acc + term
        return acc
    rows = z_scr.shape[0]
    for r in sorted({o % SUBLANES for o in offs}):
        z_scr[...] = ext_ref[pl.ds(r, rows), c0:c1]
        for j, o in enumerate(offs):
            if o % SUBLANES == r:
                term = w_ref[j:j + 1, c0:c1] * z_scr[pl.ds(o - r, tm), :]
                acc = term if acc is None else acc + term
    return acc


def _halo_specs(tm, halo, n, c):
    per = tm // halo
    nblk = n // halo
    prev = pl.BlockSpec((1, halo, c), lambda bi, i: (bi, jnp.maximum(i * per - 1, 0), 0))
    cur = pl.BlockSpec((1, tm, c), lambda bi, i: (bi, i, 0))
    nxt = pl.BlockSpec((1, halo, c), lambda bi, i: (bi, jnp.minimum((i + 1) * per, nblk - 1), 0))
    return [prev, cur, nxt]


def _hyena_gate_kernel(prev_ref, cur_ref, next_ref, w_ref, b_ref, x0_ref, vg_ref, ext_ref, *, halo, tm, d, last):
    i = pl.program_id(1)
    _fill_window(ext_ref, prev_ref, cur_ref, next_ref, halo, tm, i, last)
    x0_ref[0] = _dwconv(ext_ref, w_ref, halo, tm, 0, d) + b_ref[:, 0:d]
    x1 = _dwconv(ext_ref, w_ref, halo, tm, d, 2 * d) + b_ref[:, d:2 * d]
    v = _dwconv(ext_ref, w_ref, halo, tm, 2 * d, 3 * d) + b_ref[:, 2 * d:3 * d]
    vg_ref[0] = v * x1


def hyena_gate(u, w_short, b_short, tm=256):
    b, n, c = u.shape
    d = c // 3
    tm = _row_tile(n, tm)
    halo = SUBLANES
    row = pl.BlockSpec((1, tm, d), lambda bi, i: (bi, i, 0))
    out = jax.ShapeDtypeStruct((b, n, d), F32)
    return pl.pallas_call(
        functools.partial(_hyena_gate_kernel, halo=halo, tm=tm, d=d, last=n // tm - 1),
        grid=(b, n // tm),
        in_specs=_halo_specs(tm, halo, n, c) + [pl.BlockSpec(w_short.shape, lambda bi, i: (0, 0)),
                                                pl.BlockSpec((1, c), lambda bi, i: (0, 0))],
        out_specs=[row, row],
        out_shape=[out, out],
        scratch_shapes=[pltpu.VMEM((tm + 2 * halo, c), F32)],
        compiler_params=_cparams("parallel", "parallel"),
        name="hyena_gate",
    )(u, u, u, w_short, b_short.reshape(1, c))


def _conformer_out_kernel(prev_ref, cur_ref, next_ref, wdw_ref, bdw_ref, lg_ref, lb_ref, w_ref, b_ref,
                          x_ref, mod_ref, o_ref, ext_ref, z_scr, *, halo, tm, d, last):
    i = pl.program_id(1)
    _fill_window(ext_ref, prev_ref, cur_ref, next_ref, halo, tm, i, last)
    u = _dwconv(ext_ref, wdw_ref, halo, tm, 0, d, z_scr) + bdw_ref[...]
    mu = jnp.mean(u, axis=-1, keepdims=True)
    uc = u - mu
    var = jnp.mean(uc * uc, axis=-1, keepdims=True)
    z = _silu(uc * lax.rsqrt(var + LN_EPS) * lg_ref[...] + lb_ref[...]).astype(BF16)
    _resid_out(x_ref, mod_ref, _dot(z, w_ref[...]) + b_ref[...], o_ref)


def conformer_out(u, w_dw, b_dw, ln_g, ln_b, w, bias, x, mod, tm=256):
    b, n, d = x.shape
    tm = _row_tile(n, tm)
    halo = 2 * SUBLANES
    assert (w_dw.shape[0] - 1) // 2 <= halo
    row = pl.BlockSpec((1, tm, d), lambda bi, i: (bi, i, 0))
    vec = pl.BlockSpec((1, d), lambda bi, i: (0, 0))
    return pl.pallas_call(
        functools.partial(_conformer_out_kernel, halo=halo, tm=tm, d=d, last=n // tm - 1),
        grid=(b, n // tm),
        in_specs=_halo_specs(tm, halo, n, d) + [pl.BlockSpec(w_dw.shape, lambda bi, i: (0, 0)), vec, vec, vec,
                                                pl.BlockSpec((d, d), lambda bi, i: (0, 0)), vec, row,
                                                pl.BlockSpec((1, MOD_ROWS, d), lambda bi, i: (bi, 0, 0))],
        out_specs=row,
        out_shape=jax.ShapeDtypeStruct((b, n, d), F32),
        scratch_shapes=[pltpu.VMEM((tm + 2 * halo, d), F32), pltpu.VMEM((tm + 2 * halo - SUBLANES, d), F32)],
        compiler_params=_cparams("parallel", "parallel"),
        name="conformer_out",
    )(u, u, u, w_dw, b_dw.reshape(1, d), ln_g.reshape(1, d), ln_b.reshape(1, d), w, bias.reshape(1, d), x, mod)


def _modglu_kernel(x_ref, mod_ref, g_ref, wa_ref, wg_ref, ba_ref, bg_ref, o_ref):
    h = _modulate(x_ref[0], g_ref[...], mod_ref[0, 0:1, :], mod_ref[0, 1:2, :]).astype(BF16)
    a = _dot(h, wa_ref[...]) + ba_ref[...]
    gt = _dot(h, wg_ref[...]) + bg_ref[...]
    o_ref[0] = a * jax.nn.sigmoid(gt)


def modglu(x, mod, g, w, bias, tm=512):
    b, n, d = x.shape
    tm = _row_tile(n, tm)
    row = pl.BlockSpec((1, tm, d), lambda bi, i: (bi, i, 0))
    bias2 = bias.reshape(1, 2 * d)
    return pl.pallas_call(
        _modglu_kernel,
        grid=(b, n // tm),
        in_specs=[row, pl.BlockSpec((1, MOD_ROWS, d), lambda bi, i: (bi, 0, 0)),
                  pl.BlockSpec((1, d), lambda bi, i: (0, 0)),
                  pl.BlockSpec((d, d), lambda bi, i: (0, 0)), pl.BlockSpec((d, d), lambda bi, i: (0, 1)),
                  pl.BlockSpec((1, d), lambda bi, i: (0, 0)), pl.BlockSpec((1, d), lambda bi, i: (0, 1))],
        out_specs=row,
        out_shape=jax.ShapeDtypeStruct((b, n, d), F32),
        compiler_params=_cparams("parallel", "parallel"),
        name="modglu",
    )(x, mod, g.reshape(1, d), w, w, bias2, bias2)


def _filter_kernel(emb_ref, t_ref, w1_ref, b1_ref, w2_ref, b2_ref, w3_ref, b3_ref, fr_ref, w4_ref, dl_ref, o_ref):
    fr = fr_ref[...]
    hdn = jnp.sin(fr * (_dot32(emb_ref[...], w1_ref[...]) + b1_ref[...]))
    hdn = jnp.sin(fr * (_dot32(hdn, w2_ref[...]) + b2_ref[...]))
    hdn = jnp.sin(fr * (_dot32(hdn, w3_ref[...]) + b3_ref[...]))
    h = _dot32(hdn, w4_ref[...]) * jnp.exp(-t_ref[...] * dl_ref[...])
    row = lax.broadcasted_iota(jnp.int32, h.shape, 0)
    col = lax.broadcasted_iota(jnp.int32, h.shape, 1)
    drop = (row == 0) & (pl.program_id(0) == 0) & (col >= h.shape[1] // 2)
    o_ref[...] = jnp.where(drop, 0.0, h)


def hyena_filter(L, w1, b1, w2, b2, w3, b3, freq, w4, d_model, tl=256):
    bands = (w1.shape[0] - 1) // 2
    hid = w1.shape[1]
    t = jnp.linspace(0.0, 1.0, L, dtype=F32)[:, None]
    wpos = (2.0 * math.pi / L) * jnp.arange(L, dtype=F32)
    bnd = jnp.linspace(1e-4, bands - 1, bands, dtype=F32)
    fw = wpos[:, None] * bnd[None, :]
    emb = jnp.concatenate([t, jnp.cos(fw), -jnp.sin(fw)], axis=-1)
    max_decay = math.log(HY_DECAY_TARGET) / HY_FAST_DECAY
    min_decay = math.log(HY_DECAY_TARGET) / HY_SLOW_DECAY
    deltas = jnp.abs(jnp.linspace(min_decay, max_decay, d_model, dtype=F32))
    dl2 = jnp.concatenate([deltas, deltas])[None, :]
    tl = _row_tile(L, tl)
    td = 2 * d_model
    full = lambda a: pl.BlockSpec(a.shape, lambda i, j: (0, 0))
    vec = lambda a: a.reshape(1, -1)
    args = [emb, t, w1, vec(b1), w2, vec(b2), w3, vec(b3), vec(freq)]
    return pl.pallas_call(
        _filter_kernel,
        grid=(L // tl, 2 * d_model // td),
        in_specs=[pl.BlockSpec((tl, emb.shape[1]), lambda i, j: (i, 0)), pl.BlockSpec((tl, 1), lambda i, j: (i, 0))]
                 + [full(a) for a in args[2:]]
                 + [pl.BlockSpec((hid, td), lambda i, j: (0, j)), pl.BlockSpec((1, td), lambda i, j: (0, j))],
        out_specs=pl.BlockSpec((tl, td), lambda i, j: (i, j)),
        out_shape=jax.ShapeDtypeStruct((L, 2 * d_model), F32),
        compiler_params=_cparams("parallel", "parallel"),
        name="hyena_filter",
    )(*args, w4, dl2)


FFT_N2 = 128


def _split_bf16(a):
    hi = a.astype(BF16)
    return hi, (a - hi.astype(F32)).astype(BF16)


def _dotp_split(ah, al, b):
    bh, bl = _split_bf16(b)
    return _dot(ah, bh) + (_dot(ah, bl) + _dot(al, bh))


def _dotp(a, b):
    return _dotp_split(*_split_bf16(a), b)


def _to_block_major(x, n2):
    bx, L, c = x.shape
    r = L // n2
    return x.reshape(bx, r, n2 // SUBLANES, SUBLANES, c).transpose(0, 2, 1, 3, 4).reshape(
        bx, n2 // SUBLANES, r * SUBLANES, c)


def _from_block_major(y, n2):
    bx, na, rows8, c = y.shape
    r = rows8 // SUBLANES
    return y.reshape(bx, na, r, SUBLANES, c).transpose(0, 2, 1, 3, 4).reshape(bx, r * n2, c)


def _major_kernel(m_ref, x_ref, o_ref):
    mh, ml = _split_bf16(m_ref[...])
    rows_out, rows_in = m_ref.shape
    x2 = x_ref.at[0, 0]
    o2 = o_ref.at[0, 0]
    for s in range(SUBLANES):
        y = _dotp_split(mh, ml, x2[pl.ds(s, rows_in, stride=SUBLANES), :])
        o2[pl.ds(s, rows_out, stride=SUBLANES), :] = y


def dft_major(mat, x):
    bx, na, rows8, c = x.shape
    m, r = mat.shape
    assert rows8 == r * SUBLANES
    tc = LANES
    return pl.pallas_call(
        _major_kernel,
        grid=(bx, na, c // tc),
        in_specs=[pl.BlockSpec((m, r), lambda bi, a, j: (0, 0)),
                  pl.BlockSpec((1, 1, rows8, tc), lambda bi, a, j: (bi, a, 0, j))],
        out_specs=pl.BlockSpec((1, 1, m * SUBLANES, tc), lambda bi, a, j: (bi, a, 0, j)),
        out_shape=jax.ShapeDtypeStruct((bx, na, m * SUBLANES, c), F32),
        compiler_params=_cparams("parallel", "parallel", "parallel"),
        name="dft_major",
    )(mat, x)


def _twiddled_block(f_ref, tw_ref):
    fr, fi = f_ref[0], f_ref[1]
    twr, twi = tw_ref[0, 0:1, :], tw_ref[0, 1:2, :]
    gr = fr * twr - fi * twi
    gi = fr * twi + fi * twr
    return jnp.concatenate([jnp.concatenate([gr, -gi], axis=1), jnp.concatenate([gi, gr], axis=1)], axis=0)


def _gather_k1(ref, n2):
    return jnp.concatenate([ref[0, :, p, 0].reshape(n2, -1) for p in range(2)], axis=0)


def _scatter_k1(ref, x, n2):
    for p in range(2):
        ref[0, :, p, 0] = x[p * n2:(p + 1) * n2].reshape(n2 // SUBLANES, SUBLANES, -1)


def _spec_taps_kernel(f_ref, tw_ref, af_ref, ab_ref, o_ref):
    n2 = f_ref.shape[1]
    g = _twiddled_block(f_ref, tw_ref)
    xf = _dotp(g, _gather_k1(af_ref, n2))
    xb = _dotp(g, _gather_k1(ab_ref, n2))
    _scatter_k1(o_ref, jnp.concatenate([xf[:n2] + xb[:n2], xf[n2:] - xb[n2:]], axis=0), n2)


def _spec_mid_kernel(f_ref, tw_ref, a_ref, h_ref, o_ref):
    n2 = f_ref.shape[1]
    g = _twiddled_block(f_ref, tw_ref)
    x = _dotp(g, _gather_k1(a_ref, n2))
    h = _gather_k1(h_ref, n2)
    xr, xi = x[:n2], x[n2:]
    hr, hi = h[:n2], h[n2:]
    z = jnp.concatenate([xr * hr - xi * hi, xr * hi + xi * hr], axis=0)
    _scatter_k1(o_ref, _dotp(g.T, z), n2)


def spec_stage(fmat, tw, a, h=None, tc=1024):
    bx, na, rows8, ca = a.shape
    n2 = na * SUBLANES
    n1 = rows8 // (2 * SUBLANES)
    c = ca // 2 if h is None else ca
    tc = _row_tile(c, tc)
    nj = c // tc
    six = lambda t: t.reshape(t.shape[0], na, 2, n1, SUBLANES, t.shape[-1])
    blk = lambda off: pl.BlockSpec((1, na, 2, 1, SUBLANES, tc), lambda k, j, bi: (bi, 0, 0, k, 0, off + j))
    in_specs = [pl.BlockSpec((2, n2, n2), lambda k, j, bi: (0, 0, 0)),
                pl.BlockSpec((1, 2, n2), lambda k, j, bi: (k, 0, 0)), blk(0)]
    if h is None:
        kern, args = _spec_taps_kernel, [fmat, tw, six(a), six(a)]
        in_specs.append(blk(nj))
    else:
        kern, args = _spec_mid_kernel, [fmat, tw, six(a), six(h)]
        in_specs.append(pl.BlockSpec((1, na, 2, 1, SUBLANES, tc), lambda k, j, bi: (0, 0, 0, k, 0, j)))
    out = pl.pallas_call(
        kern,
        grid=(n1, nj, bx),
        in_specs=in_specs,
        out_specs=blk(0),
        out_shape=jax.ShapeDtypeStruct((bx, na, 2, n1, SUBLANES, c), F32),
        compiler_params=_cparams("parallel", "parallel", "arbitrary"),
        name="dft_minor",
    )(*args)
    return out.reshape(bx, na, rows8, c)


def _dft_constants(n1, n2):
    n = n1 * n2
    k1 = np.arange(n1, dtype=np.float64)
    ang1 = -2.0 * np.pi * np.outer(k1, k1) / n1
    w1r, w1i = np.cos(ang1), np.sin(ang1)
    k2 = np.arange(n2, dtype=np.float64)
    ang2 = -2.0 * np.pi * np.outer(k2, k2) / n2
    fmat = np.stack([np.cos(ang2), np.sin(ang2)])
    angt = -2.0 * np.pi * np.outer(k1, k2) / n
    tw = np.stack([np.cos(angt), np.sin(angt)], axis=1)
    fwd_half = np.concatenate([w1r[:, :n1 // 2], w1i[:, :n1 // 2]], axis=0)
    inv_half = np.concatenate([w1r[:n1 // 2], w1i[:n1 // 2]], axis=1) / n
    f = lambda a: jnp.asarray(a, F32)
    return f(fwd_half), f(inv_half), f(fmat), f(tw)


def long_conv(v, taps):
    b, L, c = v.shape
    n2 = FFT_N2
    n1 = 2 * L // n2
    fwd_half, inv_half, fmat, tw = _dft_constants(n1, n2)
    hs = spec_stage(fmat, tw, dft_major(fwd_half, _to_block_major(taps[None], n2)))
    a = dft_major(fwd_half, _to_block_major(v, n2))
    y = dft_major(inv_half, spec_stage(fmat, tw, a, hs))
    return _from_block_major(y, n2)


def _dense_conv_kernel(f_ref, g_ref, v_ref, hf_ref, hb_ref, o_ref):
    n = f_ref.shape[0] // 2
    f = f_ref[...]
    x = _dotp(f, v_ref[0])
    hf = _dotp(f, hf_ref[...])
    hb = _dotp(f, hb_ref[...])
    xr, xi = x[:n], x[n:]
    hr = hf[:n] + hb[:n]
    hi = hf[n:] - hb[n:]
    z = jnp.concatenate([xr * hr - xi * hi, xr * hi + xi * hr], axis=0)
    o_ref[0] = _dotp(g_ref[...], z)


def dense_long_conv(v, taps, tc=512):
    b, L, c = v.shape
    n = 2 * L
    ang = 2.0 * np.pi * np.outer(np.arange(n, dtype=np.float64), np.arange(L, dtype=np.float64)) / n
    fmat = jnp.asarray(np.concatenate([np.cos(ang), -np.sin(ang)], axis=0), F32)
    gmat = jnp.asarray(np.concatenate([np.cos(ang.T), -np.sin(ang.T)], axis=1) / n, F32)
    tc = _row_tile(c, tc)
    nj = c // tc
    return pl.pallas_call(
        _dense_conv_kernel,
        grid=(b, nj),
        in_specs=[pl.BlockSpec((2 * n, L), lambda bi, j: (0, 0)), pl.BlockSpec((L, 2 * n), lambda bi, j: (0, 0)),
                  pl.BlockSpec((1, L, tc), lambda bi, j: (bi, 0, j)),
                  pl.BlockSpec((L, tc), lambda bi, j: (0, j)), pl.BlockSpec((L, tc), lambda bi, j: (0, nj + j))],
        out_specs=pl.BlockSpec((1, L, tc), lambda bi, j: (bi, 0, j)),
        out_shape=jax.ShapeDtypeStruct((b, L, c), F32),
        compiler_params=_cparams("parallel", "parallel"),
        name="dense_long_conv",
    )(fmat, gmat, v, taps, taps)


def _pad_mod(m3):
    return jnp.pad(m3, ((0, 0), (0, MOD_ROWS - 3), (0, 0)))


def kernel(x, c, ctx, c_ctx, w_mod, b_mod, norm_g, w_ffn_in, w_ffn_out, attn_w_qkv, attn_w_o, attn_lambda,
           attn_subln_g, hy_w_in, hy_b_in, hy_w_short, hy_b_short, hy_f_w1, hy_f_b1, hy_f_w2, hy_f_b2, hy_f_w3,
           hy_f_b3, hy_f_freq, hy_f_w4, hy_skip, hy_w_out, hy_b_out, cv_w_pw1, cv_b_pw1, cv_w_dw, cv_b_dw,
           cv_ln_g, cv_ln_b, cv_w_pw2, cv_b_pw2, final_g):
    bsz, n_lat, d = x.shape
    n_ctx = ctx.shape[1]
    depth = w_mod.shape[0]
    assert bsz + 1 <= MOD_ROWS

    rows = jnp.concatenate([c, c_ctx[None, :], jnp.zeros((MOD_ROWS - bsz - 1, d), F32)], axis=0)
    table = mod_table(rows, w_mod, b_mod).reshape(depth, MOD_ROWS, N_MOD, d)

    def mods(i, s, latent):
        r = table[i, :bsz] if latent else jnp.broadcast_to(table[i, bsz:bsz + 1], (bsz, N_MOD, d))
        return _pad_mod(r[:, 3 * s:3 * s + 3])

    bf = lambda w: w.astype(BF16)
    xc = ctx
    tk_lat = 640 if (n_lat + n_ctx) % 640 == 0 else n_ctx
    rope = rope_tables(n_lat)

    for i in range(depth):
        kind = i % N_MIXERS
        j = i // N_MIXERS
        last = i == depth - 1
        ctx_in_use = (not last) or kind == 0
        ctx_advance = not last
        w_in0, w_out0 = bf(w_ffn_in[i, 0]), bf(w_ffn_out[i, 0])
        w_in1, w_out1 = bf(w_ffn_in[i, 1]), bf(w_ffn_out[i, 1])

        x = ffn(x, mods(i, 0, True), norm_g[i, 0], w_in0, w_out0)
        if ctx_in_use:
            xc = ffn(xc, mods(i, 0, False), norm_g[i, 0], w_in0, w_out0)

        ml, mc = mods(i, 1, True), mods(i, 1, False)
        if kind == 0:
            lam_init = 0.8 - 0.6 * math.exp(-0.3 * i)
            wqkv = bf(attn_w_qkv[j])
            wo = bf(attn_w_o[j])
            qkv_l = modproj(x, ml, norm_g[i, 1], wqkv, None, BF16, rope=rope, rope_blocks=2)
            qkv_c = modproj(xc, mc, norm_g[i, 1], wqkv, None, BF16)
            k_all = jnp.concatenate([qkv_l[:, :, d:2 * d], qkv_c[:, :, d:2 * d]], axis=1)
            v_all = jnp.concatenate([qkv_l[:, :, 2 * d:], qkv_c[:, :, 2 * d:]], axis=1)
            kt, vt = _key_value_tiles(k_all, v_all, tk_lat)
            o_l = diff_attention(qkv_l[:, :, :d], kt, vt, attn_lambda[j], attn_subln_g[j], lam_init)
            x = attn_out(o_l, wo, x, ml)
            if ctx_advance:
                ktc, vtc = _key_value_tiles(qkv_c[:, :, d:2 * d], qkv_c[:, :, 2 * d:], n_ctx)
                o_c = diff_attention(qkv_c[:, :, :d], ktc, vtc, attn_lambda[j], attn_subln_g[j], lam_init)
                yc_fn = lambda xcur: attn_out(o_c, wo, xcur, mc)
        elif kind == 1:
            w_in, w_o = bf(hy_w_in[j]), bf(hy_w_out[j])
            filt = (hy_f_w1[j], hy_f_b1[j], hy_f_w2[j], hy_f_b2[j], hy_f_w3[j], hy_f_b3[j], hy_f_freq[j], hy_f_w4[j])

            def hyena(xs, mod, L):
                u = modproj(xs, mod, norm_g[i, 1], w_in, hy_b_in[j], F32)
                x0, vg = hyena_gate(u, hy_w_short[j], hy_b_short[j])
                taps = hyena_filter(L, *filt, d)
                y = long_conv(vg, taps) if L == n_lat else dense_long_conv(vg, taps)
                return y, vg, x0

            y_l, vg_l, x0_l = hyena(x, ml, n_lat)
            x = hyena_out(y_l, vg_l, x0_l, hy_skip[j], w_o, hy_b_out[j], x, ml)
            if ctx_advance:
                y_c, vg_c, x0_c = hyena(xc, mc, n_ctx)
                yc_fn = lambda xcur: hyena_out(y_c, vg_c, x0_c, hy_skip[j], w_o, hy_b_out[j], xcur, mc)
        else:
            w1, w2 = bf(cv_w_pw1[j]), bf(cv_w_pw2[j])
            u_l = modglu(x, ml, norm_g[i, 1], w1, cv_b_pw1[j])
            x = conformer_out(u_l, cv_w_dw[j], cv_b_dw[j], cv_ln_g[j], cv_ln_b[j], w2, cv_b_pw2[j], x, ml)
            if ctx_advance:
                u_c = modglu(xc, mc, norm_g[i, 1], w1, cv_b_pw1[j])
                yc_fn = lambda xcur: conformer_out(u_c, cv_w_dw[j], cv_b_dw[j], cv_ln_g[j], cv_ln_b[j], w2,
                                                   cv_b_pw2[j], xcur, mc)

        x = ffn(x, mods(i, 2, True), norm_g[i, 2], w_in1, w_out1, final_g=final_g if last else None)
        if ctx_advance:
            xc = yc_fn(xc)
            xc = ffn(xc, mods(i, 2, False), norm_g[i, 2], w_in1, w_out1)
    return x
```

```python
import functools
import math

import numpy as np
import jax
import jax.numpy as jnp
from jax import lax
from jax.experimental import pallas as pl
from jax.experimental.pallas import tpu as pltpu

F32 = jnp.float32
BF16 = jnp.bfloat16

GRID_W = 64
ATTN_HEADS = 8
ROPE_THETA = 10000.0
HY_FAST_DECAY = 0.3
HY_SLOW_DECAY = 1.5
HY_DECAY_TARGET = 1e-2
EPS = 1e-6
LN_EPS = 1e-5
N_MIXERS = 3
N_MOD = 9

LANES = 128
SUBLANES = 8
BF16_SUBLANES = 16
VMEM_LIMIT_BYTES = 56 * 1024 * 1024

MOD_ROWS = SUBLANES
HIGHEST = lax.Precision.HIGHEST


def _cparams(*sem):
    return pltpu.CompilerParams(dimension_semantics=sem, vmem_limit_bytes=VMEM_LIMIT_BYTES)


def _row_tile(n, want):
    t = min(n, want)
    assert n % t == 0, (n, t)
    return t


def _dot(a, b):
    return jnp.dot(a, b, preferred_element_type=F32)


def _dot32(a, b):
    return jnp.dot(a, b, preferred_element_type=F32, precision=HIGHEST)


def _rmsnorm(x, g):
    return x * lax.rsqrt(jnp.mean(x * x, axis=-1, keepdims=True) + EPS) * g


def _modulate(x, g, shift, scale):
    return _rmsnorm(x, g) * (1.0 + scale) + shift


def _silu(x):
    return x * jax.nn.sigmoid(x)


def _mod_kernel(r_ref, w_ref, b_ref, o_ref):
    r = _silu(r_ref[...]).astype(BF16)
    o_ref[0] = _dot(r, w_ref[0].astype(BF16)) + b_ref[0]


def mod_table(rows, w_mod, b_mod):
    depth, d, nm = w_mod.shape
    tn = nm // N_MOD
    return pl.pallas_call(
        _mod_kernel,
        grid=(depth, nm // tn),
        in_specs=[pl.BlockSpec((MOD_ROWS, d), lambda i, j: (0, 0)),
                  pl.BlockSpec((1, d, tn), lambda i, j: (i, 0, j)),
                  pl.BlockSpec((1, 1, tn), lambda i, j: (i, 0, j))],
        out_specs=pl.BlockSpec((1, MOD_ROWS, tn), lambda i, j: (i, 0, j)),
        out_shape=jax.ShapeDtypeStruct((depth, MOD_ROWS, nm), F32),
        compiler_params=_cparams("parallel", "parallel"),
        name="mod_table",
    )(rows, w_mod, b_mod.reshape(depth, 1, nm))


FFN_CHUNKS = 2


def _ffn_kernel(x_ref, mod_ref, g_ref, wg_ref, wu_ref, wo_ref, *rest, final):
    if final:
        fg_ref, o_ref = rest
    else:
        (o_ref,) = rest
    x = x_ref[0]
    h = _modulate(x, g_ref[...], mod_ref[0, 0:1, :], mod_ref[0, 1:2, :]).astype(BF16)
    tf = wo_ref.shape[0] // FFN_CHUNKS
    acc = None
    for c in range(FFN_CHUNKS):
        gate = _dot(h, wg_ref[:, c * tf:(c + 1) * tf])
        up = _dot(h, wu_ref[:, c * tf:(c + 1) * tf])
        act = (_silu(gate) * up).astype(BF16)
        part = _dot(act, wo_ref[c * tf:(c + 1) * tf, :])
        acc = part if acc is None else acc + part
    xn = x + 0.5 * mod_ref[0, 2:3, :] * acc
    if final:
        xn = _rmsnorm(xn, fg_ref[...])
    o_ref[0] = xn


def ffn(x, mod, g, w_in, w_out, final_g=None, tm=1024):
    b, n, d = x.shape
    ff = w_out.shape[0]
    tm = _row_tile(n, tm)
    assert (ff // FFN_CHUNKS) % LANES == 0
    final = final_g is not None
    resident = dict(pipeline_mode=pl.Buffered(1))
    in_specs = [pl.BlockSpec((1, tm, d), lambda bi, i: (bi, i, 0)),
                pl.BlockSpec((1, MOD_ROWS, d), lambda bi, i: (bi, 0, 0)),
                pl.BlockSpec((1, d), lambda bi, i: (0, 0)),
                pl.BlockSpec((d, ff), lambda bi, i: (0, 0), **resident),
                pl.BlockSpec((d, ff), lambda bi, i: (0, 1), **resident),
                pl.BlockSpec((ff, d), lambda bi, i: (0, 0), **resident)]
    args = [x, mod, g.reshape(1, d), w_in, w_in, w_out]
    if final:
        in_specs.append(pl.BlockSpec((1, d), lambda bi, i: (0, 0)))
        args.append(final_g.reshape(1, d))
    return pl.pallas_call(
        functools.partial(_ffn_kernel, final=final),
        grid=(b, n // tm),
        in_specs=in_specs,
        out_specs=pl.BlockSpec((1, tm, d), lambda bi, i: (bi, i, 0)),
        out_shape=jax.ShapeDtypeStruct((b, n, d), F32),
        compiler_params=_cparams("parallel", "parallel"),
        name="ffn",
    )(*args)


def _rope_cols(y, cos, sa, sb):
    outs = []
    for c in range(y.shape[1] // LANES):
        yc = y[:, c * LANES:(c + 1) * LANES]
        outs.append(yc * cos + pltpu.roll(yc, LANES - 16, 1) * sa + pltpu.roll(yc, 16, 1) * sb)
    return jnp.concatenate(outs, axis=1)


def _modproj_kernel(*refs, has_bias, rope_blocks):
    x_ref, mod_ref, g_ref, w_ref = refs[:4]
    rest = refs[4:]
    if has_bias:
        b_ref, rest = rest[0], rest[1:]
    if rope_blocks:
        cos_ref, sa_ref, sb_ref, o_ref, h_scr = rest
    else:
        o_ref, h_scr = rest
    j = pl.program_id(2)

    @pl.when(j == 0)
    def _():
        h = _modulate(x_ref[0], g_ref[...], mod_ref[0, 0:1, :], mod_ref[0, 1:2, :])
        h_scr[...] = h.astype(BF16)

    y = _dot(h_scr[...], w_ref[...])
    if has_bias:
        y = y + b_ref[...]
    if rope_blocks:
        @pl.when(j < rope_blocks)
        def _():
            o_ref[0] = _rope_cols(y, cos_ref[...], sa_ref[...], sb_ref[...]).astype(o_ref.dtype)

        @pl.when(j >= rope_blocks)
        def _():
            o_ref[0] = y.astype(o_ref.dtype)
    else:
        o_ref[0] = y.astype(o_ref.dtype)


def modproj(x, mod, g, w, bias, out_dtype, rope=None, rope_blocks=0, tm=512, tn=1024):
    b, n, d = x.shape
    nout = w.shape[1]
    tm = _row_tile(n, tm)
    in_specs = [pl.BlockSpec((1, tm, d), lambda bi, i, j: (bi, i, 0)),
                pl.BlockSpec((1, MOD_ROWS, d), lambda bi, i, j: (bi, 0, 0)),
                pl.BlockSpec((1, d), lambda bi, i, j: (0, 0)),
                pl.BlockSpec((d, tn), lambda bi, i, j: (0, j))]
    args = [x, mod, g.reshape(1, d), w]
    if bias is not None:
        in_specs.append(pl.BlockSpec((1, tn), lambda bi, i, j: (0, j)))
        args.append(bias.reshape(1, nout))
    if rope_blocks:
        in_specs += [pl.BlockSpec((tm, LANES), lambda bi, i, j: (i, 0))] * 3
        args += list(rope)
    return pl.pallas_call(
        functools.partial(_modproj_kernel, has_bias=bias is not None, rope_blocks=rope_blocks),
        grid=(b, n // tm, nout // tn),
        in_specs=in_specs,
        out_specs=pl.BlockSpec((1, tm, tn), lambda bi, i, j: (bi, i, j)),
        out_shape=jax.ShapeDtypeStruct((b, n, nout), out_dtype),
        scratch_shapes=[pltpu.VMEM((tm, d), BF16)],
        compiler_params=_cparams("parallel", "parallel", "arbitrary"),
        name="modproj",
    )(*args)


def rope_tables(n):
    rows = n // GRID_W
    row = jnp.repeat(jnp.arange(rows), GRID_W).astype(F32)
    col = jnp.tile(jnp.arange(GRID_W), rows).astype(F32)
    quarter = 16
    half = 32
    inv = ROPE_THETA ** (-(2.0 * jnp.arange(quarter, dtype=F32)) / half)
    ang_r = row[:, None] * inv
    ang_c = col[:, None] * inv
    zero = jnp.zeros_like(ang_r)
    cr, sr, cc, sc = jnp.cos(ang_r), jnp.sin(ang_r), jnp.cos(ang_c), jnp.sin(ang_c)
    cos64 = jnp.concatenate([cr, cr, cc, cc], axis=1)
    sa64 = jnp.concatenate([-sr, zero, -sc, zero], axis=1)
    sb64 = jnp.concatenate([zero, sr, zero, sc], axis=1)
    tile2 = lambda t: jnp.concatenate([t, t], axis=1)
    return tile2(cos64), tile2(sa64), tile2(sb64)


V_ROWS = LANES + BF16_SUBLANES
PIPE_UNROLL = 8


def _attn_kernel(q_ref, k_ref, v_ref, lam_ref, g_ref, o_ref, qm_scr, sa_scr, sb_scr, ma_scr, mb_scr, m_scr, acc_scr, *,
                 nkt, nq, tq, lam_init, half):
    lane = lax.broadcasted_iota(jnp.int32, (tq, LANES), 1)
    for qi in range(nq):
        q = q_ref[0, qi * tq:(qi + 1) * tq, :].astype(F32) * (half ** -0.5 * math.log2(math.e))
        qm_scr[qi, 0] = jnp.where(lane < half, q, 0.0).astype(BF16)
        qm_scr[qi, 1] = jnp.where(lane >= half, q, 0.0).astype(BF16)
    m_scr[...] = jnp.full(m_scr.shape, -jnp.inf, F32)
    acc_scr[...] = jnp.zeros_like(acc_scr)

    def scores(u, bufs):
        s_ref, mt_ref = bufs
        qi, t = u // nkt, u % nkt
        k = k_ref[0, 0, t]
        for mi in range(2):
            s = lax.dot_general(k, qm_scr[qi, mi], (((1,), (1,)), ((), ())),
                                preferred_element_type=F32)
            s_ref[mi] = s
            mt_ref[mi] = jnp.max(s, axis=0, keepdims=True)

    def softmax_pv(u, bufs):
        s_ref, mt_ref = bufs
        qi, t = u // nkt, u % nkt
        vt = v_ref[0, 0, t]
        for mi in range(2):
            s = s_ref[mi]
            m_old = m_scr[qi, mi]
            m_new = jnp.maximum(m_old, mt_ref[mi])
            alpha = jnp.exp2(m_old - m_new)
            p = jnp.exp2(s - m_new).astype(BF16)
            acc_scr[qi, mi] = alpha * acc_scr[qi, mi] + _dot(vt, p)
            m_scr[qi, mi] = m_new

    units = nq * nkt
    bufs = ((sa_scr, ma_scr), (sb_scr, mb_scr))
    scores(0, bufs[0])

    def steps(first, count):
        for i in range(count):
            scores(first + i + 1, bufs[(i + 1) % 2])
            softmax_pv(first + i, bufs[i % 2])

    n_steps = units - 1
    n_loop = n_steps // PIPE_UNROLL

    def body(j, carry):
        steps(PIPE_UNROLL * j, PIPE_UNROLL)
        return carry

    lax.fori_loop(0, n_loop, body, 0)
    steps(n_loop * PIPE_UNROLL, n_steps - n_loop * PIPE_UNROLL)
    softmax_pv(units - 1, bufs[(units - 1) % 2])

    lv = lam_ref[...]
    lam = (jnp.exp(jnp.sum(lv[0:1] * lv[1:2], axis=1, keepdims=True))
           - jnp.exp(jnp.sum(lv[2:3] * lv[3:4], axis=1, keepdims=True)) + lam_init)
    for qi in range(nq):
        a1 = acc_scr[qi, 0]
        a2 = acc_scr[qi, 1]
        ot = a1[:LANES] / a1[LANES:LANES + 1] - lam * (a2[:LANES] / a2[LANES:LANES + 1])
        o = _rmsnorm(ot.T, g_ref[...]) * (1.0 - lam_init)
        o_ref[0, qi * tq:(qi + 1) * tq, :] = o.astype(o_ref.dtype)


def diff_attention(q, kt, vt, lam_vecs, subln_g, lam_init, tq=256, nq=4):
    b, n, dq = q.shape
    _, h, nkt, tk, _ = kt.shape
    tq = _row_tile(n, tq)
    nq = min(nq, n // tq)
    tb = nq * tq
    assert n % tb == 0
    return pl.pallas_call(
        functools.partial(_attn_kernel, nkt=nkt, nq=nq, tq=tq, lam_init=lam_init, half=LANES // 2),
        grid=(b, h, n // tb),
        in_specs=[pl.BlockSpec((1, tb, LANES), lambda bi, hi, i: (bi, i, hi)),
                  pl.BlockSpec((1, 1, nkt, tk, LANES), lambda bi, hi, i: (bi, hi, 0, 0, 0)),
                  pl.BlockSpec((1, 1, nkt, V_ROWS, tk), lambda bi, hi, i: (bi, hi, 0, 0, 0)),
                  pl.BlockSpec((4, LANES // 2), lambda bi, hi, i: (0, 0)),
                  pl.BlockSpec((1, LANES), lambda bi, hi, i: (0, 0))],
        out_specs=pl.BlockSpec((1, tb, LANES), lambda bi, hi, i: (bi, i, hi)),
        out_shape=jax.ShapeDtypeStruct((b, n, dq), BF16),
        scratch_shapes=[pltpu.VMEM((nq, 2, tq, LANES), BF16),
                        pltpu.VMEM((2, tk, tq), F32), pltpu.VMEM((2, tk, tq), F32),
                        pltpu.VMEM((2, 1, tq), F32), pltpu.VMEM((2, 1, tq), F32),
                        pltpu.VMEM((nq, 2, 1, tq), F32), pltpu.VMEM((nq, 2, V_ROWS, tq), F32)],
        compiler_params=_cparams("parallel", "parallel", "arbitrary"),
        name="diff_attention",
    )(q, kt, vt, lam_vecs, subln_g.reshape(1, LANES))


def _key_value_tiles(k, v, tk):
    b, nk, _ = k.shape
    h = ATTN_HEADS
    nkt = nk // tk
    kt = k.reshape(b, nkt, tk, h, LANES).transpose(0, 3, 1, 2, 4)
    v5 = v.reshape(b, nkt, tk, h, LANES).transpose(0, 3, 1, 4, 2)
    ones = jnp.ones((b, h, nkt, 1, tk), BF16)
    zeros = jnp.zeros((b, h, nkt, V_ROWS - LANES - 1, tk), BF16)
    return kt, jnp.concatenate([v5, ones, zeros], axis=3)


def _resid_out(x_ref, mod_ref, y, o_ref):
    o_ref[0] = x_ref[0] + mod_ref[0, 2:3, :] * y


def _attn_out_kernel(a_ref, w_ref, x_ref, mod_ref, o_ref):
    _resid_out(x_ref, mod_ref, _dot(a_ref[0], w_ref[...]), o_ref)


def attn_out(a, w, x, mod, tm=512):
    b, n, d = x.shape
    tm = _row_tile(n, tm)
    row = pl.BlockSpec((1, tm, d), lambda bi, i: (bi, i, 0))
    return pl.pallas_call(
        _attn_out_kernel,
        grid=(b, n // tm),
        in_specs=[row, pl.BlockSpec((d, d), lambda bi, i: (0, 0)), row,
                  pl.BlockSpec((1, MOD_ROWS, d), lambda bi, i: (bi, 0, 0))],
        out_specs=row,
        out_shape=jax.ShapeDtypeStruct((b, n, d), F32),
        compiler_params=_cparams("parallel", "parallel"),
        name="attn_out",
    )(a, w, x, mod)


def _hyena_out_kernel(y_ref, vg_ref, x0_ref, skip_ref, w_ref, b_ref, x_ref, mod_ref, o_ref):
    a = ((y_ref[0] + vg_ref[0] * skip_ref[...]) * x0_ref[0]).astype(BF16)
    _resid_out(x_ref, mod_ref, _dot(a, w_ref[...]) + b_ref[...], o_ref)


def hyena_out(y, vg, x0, skip, w, bias, x, mod, tm=512):
    b, n, d = x.shape
    tm = _row_tile(n, tm)
    row = pl.BlockSpec((1, tm, d), lambda bi, i: (bi, i, 0))
    vec = pl.BlockSpec((1, d), lambda bi, i: (0, 0))
    return pl.pallas_call(
        _hyena_out_kernel,
        grid=(b, n // tm),
        in_specs=[row, row, row, vec, pl.BlockSpec((d, d), lambda bi, i: (0, 0)), vec, row,
                  pl.BlockSpec((1, MOD_ROWS, d), lambda bi, i: (bi, 0, 0))],
        out_specs=row,
        out_shape=jax.ShapeDtypeStruct((b, n, d), F32),
        compiler_params=_cparams("parallel", "parallel"),
        name="hyena_out",
    )(y, vg, x0, skip.reshape(1, d), w, bias.reshape(1, d), x, mod)


def _fill_window(ext_ref, prev_ref, cur_ref, next_ref, halo, tm, i, last):
    ext_ref[0:halo, :] = jnp.where(i == 0, 0.0, prev_ref[0])
    ext_ref[halo:halo + tm, :] = cur_ref[0]
    ext_ref[halo + tm:halo + tm + halo, :] = jnp.where(i == last, 0.0, next_ref[0])


def _dwconv(ext_ref, w_ref, halo, tm, c0, c1, z_scr=None):
    width = w_ref.shape[0]
    pad = (width - 1) // 2
    offs = [halo - pad + j for j in range(width)]
    acc = None
    if z_scr is None:
        for j, o in enumerate(offs):
            term = w_ref[j:j + 1, c0:c1] * ext_ref[pl.ds(o, tm), c0:c1]
            acc = term if acc is None else acc + term
        return acc
    rows = z_scr.shape[0]
    for r in sorted({o % SUBLANES for o in offs}):
        z_scr[...] = ext_ref[pl.ds(r, rows), c0:c1]
        for j, o in enumerate(offs):
            if o % SUBLANES == r:
                term = w_ref[j:j + 1, c0:c1] * z_scr[pl.ds(o - r, tm), :]
                acc = term if acc is None else acc + term
    return acc


def _halo_specs(tm, halo, n, c):
    per = tm // halo
    nblk = n // halo
    prev = pl.BlockSpec((1, halo, c), lambda bi, i: (bi, jnp.maximum(i * per - 1, 0), 0))
    cur = pl.BlockSpec((1, tm, c), lambda bi, i: (bi, i, 0))
    nxt = pl.BlockSpec((1, halo, c), lambda bi, i: (bi, jnp.minimum((i + 1) * per, nblk - 1), 0))
    return [prev, cur, nxt]


def _hyena_gate_kernel(prev_ref, cur_ref, next_ref, w_ref, b_ref, x0_ref, vg_ref, ext_ref, *, halo, tm, d, last):
    i = pl.program_id(1)
    _fill_window(ext_ref, prev_ref, cur_ref, next_ref, halo, tm, i, last)
    x0_ref[0] = _dwconv(ext_ref, w_ref, halo, tm, 0, d) + b_ref[:, 0:d]
    x1 = _dwconv(ext_ref, w_ref, halo, tm, d, 2 * d) + b_ref[:, d:2 * d]
    v = _dwconv(ext_ref, w_ref, halo, tm, 2 * d, 3 * d) + b_ref[:, 2 * d:3 * d]
    vg_ref[0] = v * x1


def hyena_gate(u, w_short, b_short, tm=256):
    b, n, c = u.shape
    d = c // 3
    tm = _row_tile(n, tm)
    halo = SUBLANES
    row = pl.BlockSpec((1, tm, d), lambda bi, i: (bi, i, 0))
    out = jax.ShapeDtypeStruct((b, n, d), F32)
    return pl.pallas_call(
        functools.partial(_hyena_gate_kernel, halo=halo, tm=tm, d=d, last=n // tm - 1),
        grid=(b, n // tm),
        in_specs=_halo_specs(tm, halo, n, c) + [pl.BlockSpec(w_short.shape, lambda bi, i: (0, 0)),
                                                pl.BlockSpec((1, c), lambda bi, i: (0, 0))],
        out_specs=[row, row],
        out_shape=[out, out],
        scratch_shapes=[pltpu.VMEM((tm + 2 * halo, c), F32)],
        compiler_params=_cparams("parallel", "parallel"),
        name="hyena_gate",
    )(u, u, u, w_short, b_short.reshape(1, c))


def _conformer_out_kernel(prev_ref, cur_ref, next_ref, wdw_ref, bdw_ref, lg_ref, lb_ref, w_ref, b_ref,
                          x_ref, mod_ref, o_ref, ext_ref, z_scr, *, halo, tm, d, last):
    i = pl.program_id(1)
    _fill_window(ext_ref, prev_ref, cur_ref, next_ref, halo, tm, i, last)
    u = _dwconv(ext_ref, wdw_ref, halo, tm, 0, d, z_scr) + bdw_ref[...]
    mu = jnp.mean(u, axis=-1, keepdims=True)
    uc = u - mu
    var = jnp.mean(uc * uc, axis=-1, keepdims=True)
    z = _silu(uc * lax.rsqrt(var + LN_EPS) * lg_ref[...] + lb_ref[...]).astype(BF16)
    _resid_out(x_ref, mod_ref, _dot(z, w_ref[...]) + b_ref[...], o_ref)


def conformer_out(u, w_dw, b_dw, ln_g, ln_b, w, bias, x, mod, tm=256):
    b, n, d = x.shape
    tm = _row_tile(n, tm)
    halo = 2 * SUBLANES
    assert (w_dw.shape[0] - 1) // 2 <= halo
    row = pl.BlockSpec((1, tm, d), lambda bi, i: (bi, i, 0))
    vec = pl.BlockSpec((1, d), lambda bi, i: (0, 0))
    return pl.pallas_call(
        functools.partial(_conformer_out_kernel, halo=halo, tm=tm, d=d, last=n // tm - 1),
        grid=(b, n // tm),
        in_specs=_halo_specs(tm, halo, n, d) + [pl.BlockSpec(w_dw.shape, lambda bi, i: (0, 0)), vec, vec, vec,
                                                pl.BlockSpec((d, d), lambda bi, i: (0, 0)), vec, row,
                                                pl.BlockSpec((1, MOD_ROWS, d), lambda bi, i: (bi, 0, 0))],
        out_specs=row,
        out_shape=jax.ShapeDtypeStruct((b, n, d), F32),
        scratch_shapes=[pltpu.VMEM((tm + 2 * halo, d), F32), pltpu.VMEM((tm + 2 * halo - SUBLANES, d), F32)],
        compiler_params=_cparams("parallel", "parallel"),
        name="conformer_out",
    )(u, u, u, w_dw, b_dw.reshape(1, d), ln_g.reshape(1, d), ln_b.reshape(1, d), w, bias.reshape(1, d), x, mod)


def _modglu_kernel(x_ref, mod_ref, g_ref, wa_ref, wg_ref, ba_ref, bg_ref, o_ref):
    h = _modulate(x_ref[0], g_ref[...], mod_ref[0, 0:1, :], mod_ref[0, 1:2, :]).astype(BF16)
    a = _dot(h, wa_ref[...]) + ba_ref[...]
    gt = _dot(h, wg_ref[...]) + bg_ref[...]
    o_ref[0] = a * jax.nn.sigmoid(gt)


def modglu(x, mod, g, w, bias, tm=512):
    b, n, d = x.shape
    tm = _row_tile(n, tm)
    row = pl.BlockSpec((1, tm, d), lambda bi, i: (bi, i, 0))
    bias2 = bias.reshape(1, 2 * d)
    return pl.pallas_call(
        _modglu_kernel,
        grid=(b, n // tm),
        in_specs=[row, pl.BlockSpec((1, MOD_ROWS, d), lambda bi, i: (bi, 0, 0)),
                  pl.BlockSpec((1, d), lambda bi, i: (0, 0)),
                  pl.BlockSpec((d, d), lambda bi, i: (0, 0)), pl.BlockSpec((d, d), lambda bi, i: (0, 1)),
                  pl.BlockSpec((1, d), lambda bi, i: (0, 0)), pl.BlockSpec((1, d), lambda bi, i: (0, 1))],
        out_specs=row,
        out_shape=jax.ShapeDtypeStruct((b, n, d), F32),
        compiler_params=_cparams("parallel", "parallel"),
        name="modglu",
    )(x, mod, g.reshape(1, d), w, w, bias2, bias2)


def _filter_kernel(emb_ref, t_ref, w1_ref, b1_ref, w2_ref, b2_ref, w3_ref, b3_ref, fr_ref, w4_ref, dl_ref, o_ref):
    fr = fr_ref[...]
    hdn = jnp.sin(fr * (_dot32(emb_ref[...], w1_ref[...]) + b1_ref[...]))
    hdn = jnp.sin(fr * (_dot32(hdn, w2_ref[...]) + b2_ref[...]))
    hdn = jnp.sin(fr * (_dot32(hdn, w3_ref[...]) + b3_ref[...]))
    h = _dot32(hdn, w4_ref[...]) * jnp.exp(-t_ref[...] * dl_ref[...])
    row = lax.broadcasted_iota(jnp.int32, h.shape, 0)
    col = lax.broadcasted_iota(jnp.int32, h.shape, 1)
    drop = (row == 0) & (pl.program_id(0) == 0) & (col >= h.shape[1] // 2)
    o_ref[...] = jnp.where(drop, 0.0, h)


def hyena_filter(L, w1, b1, w2, b2, w3, b3, freq, w4, d_model, tl=256):
    bands = (w1.shape[0] - 1) // 2
    hid = w1.shape[1]
    t = jnp.linspace(0.0, 1.0, L, dtype=F32)[:, None]
    wpos = (2.0 * math.pi / L) * jnp.arange(L, dtype=F32)
    bnd = jnp.linspace(1e-4, bands - 1, bands, dtype=F32)
    fw = wpos[:, None] * bnd[None, :]
    emb = jnp.concatenate([t, jnp.cos(fw), -jnp.sin(fw)], axis=-1)
    max_decay = math.log(HY_DECAY_TARGET) / HY_FAST_DECAY
    min_decay = math.log(HY_DECAY_TARGET) / HY_SLOW_DECAY
    deltas = jnp.abs(jnp.linspace(min_decay, max_decay, d_model, dtype=F32))
    dl2 = jnp.concatenate([deltas, deltas])[None, :]
    tl = _row_tile(L, tl)
    td = 2 * d_model
    full = lambda a: pl.BlockSpec(a.shape, lambda i, j: (0, 0))
    vec = lambda a: a.reshape(1, -1)
    args = [emb, t, w1, vec(b1), w2, vec(b2), w3, vec(b3), vec(freq)]
    return pl.pallas_call(
        _filter_kernel,
        grid=(L // tl, 2 * d_model // td),
        in_specs=[pl.BlockSpec((tl, emb.shape[1]), lambda i, j: (i, 0)), pl.BlockSpec((tl, 1), lambda i, j: (i, 0))]
                 + [full(a) for a in args[2:]]
                 + [pl.BlockSpec((hid, td), lambda i, j: (0, j)), pl.BlockSpec((1, td), lambda i, j: (0, j))],
        out_specs=pl.BlockSpec((tl, td), lambda i, j: (i, j)),
        out_shape=jax.ShapeDtypeStruct((L, 2 * d_model), F32),
        compiler_params=_cparams("parallel", "parallel"),
        name="hyena_filter",
    )(*args, w4, dl2)


FFT_N2 = 128


def _split_bf16(a):
    hi = a.astype(BF16)
    return hi, (a - hi.astype(F32)).astype(BF16)


def _dotp_split(ah, al, b):
    bh, bl = _split_bf16(b)
    return _dot(ah, bh) + (_dot(ah, bl) + _dot(al, bh))


def _dotp(a, b):
    return _dotp_split(*_split_bf16(a), b)


def _to_block_major(x, n2):
    bx, L, c = x.shape
    r = L // n2
    return x.reshape(bx, r, n2 // SUBLANES, SUBLANES, c).transpose(0, 2, 1, 3, 4).reshape(
        bx, n2 // SUBLANES, r * SUBLANES, c)


def _from_block_major(y, n2):
    bx, na, rows8, c = y.shape
    r = rows8 // SUBLANES
    return y.reshape(bx, na, r, SUBLANES, c).transpose(0, 2, 1, 3, 4).reshape(bx, r * n2, c)


def _major_kernel(m_ref, x_ref, o_ref):
    mh, ml = _split_bf16(m_ref[...])
    rows_out, rows_in = m_ref.shape
    x2 = x_ref.at[0, 0]
    o2 = o_ref.at[0, 0]
    for s in range(SUBLANES):
        y = _dotp_split(mh, ml, x2[pl.ds(s, rows_in, stride=SUBLANES), :])
        o2[pl.ds(s, rows_out, stride=SUBLANES), :] = y


def dft_major(mat, x):
    bx, na, rows8, c = x.shape
    m, r = mat.shape
    assert rows8 == r * SUBLANES
    tc = LANES
    return pl.pallas_call(
        _major_kernel,
        grid=(bx, na, c // tc),
        in_specs=[pl.BlockSpec((m, r), lambda bi, a, j: (0, 0)),
                  pl.BlockSpec((1, 1, rows8, tc), lambda bi, a, j: (bi, a, 0, j))],
        out_specs=pl.BlockSpec((1, 1, m * SUBLANES, tc), lambda bi, a, j: (bi, a, 0, j)),
        out_shape=jax.ShapeDtypeStruct((bx, na, m * SUBLANES, c), F32),
        compiler_params=_cparams("parallel", "parallel", "parallel"),
        name="dft_major",
    )(mat, x)


def _twiddled_block(f_ref, tw_ref):
    fr, fi = f_ref[0], f_ref[1]
    twr, twi = tw_ref[0, 0:1, :], tw_ref[0, 1:2, :]
    gr = fr * twr - fi * twi
    gi = fr * twi + fi * twr
    return jnp.concatenate([jnp.concatenate([gr, -gi], axis=1), jnp.concatenate([gi, gr], axis=1)], axis=0)


def _gather_k1(ref, n2):
    return jnp.concatenate([ref[0, :, p, 0].reshape(n2, -1) for p in range(2)], axis=0)


def _scatter_k1(ref, x, n2):
    for p in range(2):
        ref[0, :, p, 0] = x[p * n2:(p + 1) * n2].reshape(n2 // SUBLANES, SUBLANES, -1)


def _spec_taps_kernel(f_ref, tw_ref, af_ref, ab_ref, o_ref):
    n2 = f_ref.shape[1]
    g = _twiddled_block(f_ref, tw_ref)
    xf = _dotp(g, _gather_k1(af_ref, n2))
    xb = _dotp(g, _gather_k1(ab_ref, n2))
    _scatter_k1(o_ref, jnp.concatenate([xf[:n2] + xb[:n2], xf[n2:] - xb[n2:]], axis=0), n2)


def _spec_mid_kernel(f_ref, tw_ref, a_ref, h_ref, o_ref):
    n2 = f_ref.shape[1]
    g = _twiddled_block(f_ref, tw_ref)
    x = _dotp(g, _gather_k1(a_ref, n2))
    h = _gather_k1(h_ref, n2)
    xr, xi = x[:n2], x[n2:]
    hr, hi = h[:n2], h[n2:]
    z = jnp.concatenate([xr * hr - xi * hi, xr * hi + xi * hr], axis=0)
    _scatter_k1(o_ref, _dotp(g.T, z), n2)


def spec_stage(fmat, tw, a, h=None, tc=1024):
    bx, na, rows8, ca = a.shape
    n2 = na * SUBLANES
    n1 = rows8 // (2 * SUBLANES)
    c = ca // 2 if h is None else ca
    tc = _row_tile(c, tc)
    nj = c // tc
    six = lambda t: t.reshape(t.shape[0], na, 2, n1, SUBLANES, t.shape[-1])
    blk = lambda off: pl.BlockSpec((1, na, 2, 1, SUBLANES, tc), lambda k, j, bi: (bi, 0, 0, k, 0, off + j))
    in_specs = [pl.BlockSpec((2, n2, n2), lambda k, j, bi: (0, 0, 0)),
                pl.BlockSpec((1, 2, n2), lambda k, j, bi: (k, 0, 0)), blk(0)]
    if h is None:
        kern, args = _spec_taps_kernel, [fmat, tw, six(a), six(a)]
        in_specs.append(blk(nj))
    else:
        kern, args = _spec_mid_kernel, [fmat, tw, six(a), six(h)]
        in_specs.append(pl.BlockSpec((1, na, 2, 1, SUBLANES, tc), lambda k, j, bi: (0, 0, 0, k, 0, j)))
    out = pl.pallas_call(
        kern,
        grid=(n1, nj, bx),
        in_specs=in_specs,
        out_specs=blk(0),
        out_shape=jax.ShapeDtypeStruct((bx, na, 2, n1, SUBLANES, c), F32),
        compiler_params=_cparams("parallel", "parallel", "arbitrary"),
        name="dft_minor",
    )(*args)
    return out.reshape(bx, na, rows8, c)


def _dft_constants(n1, n2):
    n = n1 * n2
    k1 = np.arange(n1, dtype=np.float64)
    ang1 = -2.0 * np.pi * np.outer(k1, k1) / n1
    w1r, w1i = np.cos(ang1), np.sin(ang1)
    k2 = np.arange(n2, dtype=np.float64)
    ang2 = -2.0 * np.pi * np.outer(k2, k2) / n2
    fmat = np.stack([np.cos(ang2), np.sin(ang2)])
    angt = -2.0 * np.pi * np.outer(k1, k2) / n
    tw = np.stack([np.cos(angt), np.sin(angt)], axis=1)
    fwd_half = np.concatenate([w1r[:, :n1 // 2], w1i[:, :n1 // 2]], axis=0)
    inv_half = np.concatenate([w1r[:n1 // 2], w1i[:n1 // 2]], axis=1) / n
    f = lambda a: jnp.asarray(a, F32)
    return f(fwd_half), f(inv_half), f(fmat), f(tw)


def long_conv(v, taps):
    b, L, c = v.shape
    n2 = FFT_N2
    n1 = 2 * L // n2
    fwd_half, inv_half, fmat, tw = _dft_constants(n1, n2)
    hs = spec_stage(fmat, tw, dft_major(fwd_half, _to_block_major(taps[None], n2)))
    a = dft_major(fwd_half, _to_block_major(v, n2))
    y = dft_major(inv_half, spec_stage(fmat, tw, a, hs))
    return _from_block_major(y, n2)


def _dense_conv_kernel(f_ref, g_ref, v_ref, hf_ref, hb_ref, o_ref):
    n = f_ref.shape[0] // 2
    f = f_ref[...]
    x = _dotp(f, v_ref[0])
    hf = _dotp(f, hf_ref[...])
    hb = _dotp(f, hb_ref[...])
    xr, xi = x[:n], x[n:]
    hr = hf[:n] + hb[:n]
    hi = hf[n:] - hb[n:]
    z = jnp.concatenate([xr * hr - xi * hi, xr * hi + xi * hr], axis=0)
    o_ref[0] = _dotp(g_ref[...], z)


def dense_long_conv(v, taps, tc=512):
    b, L, c = v.shape
    n = 2 * L
    ang = 2.0 * np.pi * np.outer(np.arange(n, dtype=np.float64), np.arange(L, dtype=np.float64)) / n
    fmat = jnp.asarray(np.concatenate([np.cos(ang), -np.sin(ang)], axis=0), F32)
    gmat = jnp.asarray(np.concatenate([np.cos(ang.T), -np.sin(ang.T)], axis=1) / n, F32)
    tc = _row_tile(c, tc)
    nj = c // tc
    return pl.pallas_call(
        _dense_conv_kernel,
        grid=(b, nj),
        in_specs=[pl.BlockSpec((2 * n, L), lambda bi, j: (0, 0)), pl.BlockSpec((L, 2 * n), lambda bi, j: (0, 0)),
                  pl.BlockSpec((1, L, tc), lambda bi, j: (bi, 0, j)),
                  pl.BlockSpec((L, tc), lambda bi, j: (0, j)), pl.BlockSpec((L, tc), lambda bi, j: (0, nj + j))],
        out_specs=pl.BlockSpec((1, L, tc), lambda bi, j: (bi, 0, j)),
        out_shape=jax.ShapeDtypeStruct((b, L, c), F32),
        compiler_params=_cparams("parallel", "parallel"),
        name="dense_long_conv",
    )(fmat, gmat, v, taps, taps)


def _pad_mod(m3):
    return jnp.pad(m3, ((0, 0), (0, MOD_ROWS - 3), (0, 0)))


def kernel(x, c, ctx, c_ctx, w_mod, b_mod, norm_g, w_ffn_in, w_ffn_out, attn_w_qkv, attn_w_o, attn_lambda,
           attn_subln_g, hy_w_in, hy_b_in, hy_w_short, hy_b_short, hy_f_w1, hy_f_b1, hy_f_w2, hy_f_b2, hy_f_w3,
           hy_f_b3, hy_f_freq, hy_f_w4, hy_skip, hy_w_out, hy_b_out, cv_w_pw1, cv_b_pw1, cv_w_dw, cv_b_dw,
           cv_ln_g, cv_ln_b, cv_w_pw2, cv_b_pw2, final_g):
    bsz, n_lat, d = x.shape
    n_ctx = ctx.shape[1]
    depth = w_mod.shape[0]
    assert bsz + 1 <= MOD_ROWS

    rows = jnp.concatenate([c, c_ctx[None, :], jnp.zeros((MOD_ROWS - bsz - 1, d), F32)], axis=0)
    table = mod_table(rows, w_mod, b_mod).reshape(depth, MOD_ROWS, N_MOD, d)

    def mods(i, s, latent):
        r = table[i, :bsz] if latent else jnp.broadcast_to(table[i, bsz:bsz + 1], (bsz, N_MOD, d))
        return _pad_mod(r[:, 3 * s:3 * s + 3])

    bf = lambda w: w.astype(BF16)
    xc = ctx
    tk_lat = 1280 if (n_lat + n_ctx) % 1280 == 0 else n_ctx
    rope = rope_tables(n_lat)

    for i in range(depth):
        kind = i % N_MIXERS
        j = i // N_MIXERS
        last = i == depth - 1
        ctx_in_use = (not last) or kind == 0
        ctx_advance = not last
        w_in0, w_out0 = bf(w_ffn_in[i, 0]), bf(w_ffn_out[i, 0])
        w_in1, w_out1 = bf(w_ffn_in[i, 1]), bf(w_ffn_out[i, 1])

        x = ffn(x, mods(i, 0, True), norm_g[i, 0], w_in0, w_out0)
        if ctx_in_use:
            xc = ffn(xc, mods(i, 0, False), norm_g[i, 0], w_in0, w_out0)

        ml, mc = mods(i, 1, True), mods(i, 1, False)
        if kind == 0:
            lam_init = 0.8 - 0.6 * math.exp(-0.3 * i)
            wqkv = bf(attn_w_qkv[j])
            wo = bf(attn_w_o[j])
            qkv_l = modproj(x, ml, norm_g[i, 1], wqkv, None, BF16, rope=rope, rope_blocks=2)
            qkv_c = modproj(xc, mc, norm_g[i, 1], wqkv, None, BF16)
            k_all = jnp.concatenate([qkv_l[:, :, d:2 * d], qkv_c[:, :, d:2 * d]], axis=1)
            v_all = jnp.concatenate([qkv_l[:, :, 2 * d:], qkv_c[:, :, 2 * d:]], axis=1)
            kt, vt = _key_value_tiles(k_all, v_all, tk_lat)
            o_l = diff_attention(qkv_l[:, :, :d], kt, vt, attn_lambda[j], attn_subln_g[j], lam_init)
            x = attn_out(o_l, wo, x, ml)
            if ctx_advance:
                ktc, vtc = _key_value_tiles(qkv_c[:, :, d:2 * d], qkv_c[:, :, 2 * d:], n_ctx)
                o_c = diff_attention(qkv_c[:, :, :d], ktc, vtc, attn_lambda[j], attn_subln_g[j], lam_init)
                yc_fn = lambda xcur: attn_out(o_c, wo, xcur, mc)
        elif kind == 1:
            w_in, w_o = bf(hy_w_in[j]), bf(hy_w_out[j])
            filt = (hy_f_w1[j], hy_f_b1[j], hy_f_w2[j], hy_f_b2[j], hy_f_w3[j], hy_f_b3[j], hy_f_freq[j], hy_f_w4[j])

            def hyena(xs, mod, L):
                u = modproj(xs, mod, norm_g[i, 1], w_in, hy_b_in[j], F32)
                x0, vg = hyena_gate(u, hy_w_short[j], hy_b_short[j])
                taps = hyena_filter(L, *filt, d)
                y = long_conv(vg, taps) if L == n_lat else dense_long_conv(vg, taps)
                return y, vg, x0

            y_l, vg_l, x0_l = hyena(x, ml, n_lat)
            x = hyena_out(y_l, vg_l, x0_l, hy_skip[j], w_o, hy_b_out[j], x, ml)
            if ctx_advance:
                y_c, vg_c, x0_c = hyena(xc, mc, n_ctx)
                yc_fn = lambda xcur: hyena_out(y_c, vg_c, x0_c, hy_skip[j], w_o, hy_b_out[j], xcur, mc)
        else:
            w1, w2 = bf(cv_w_pw1[j]), bf(cv_w_pw2[j])
            u_l = modglu(x, ml, norm_g[i, 1], w1, cv_b_pw1[j])
            x = conformer_out(u_l, cv_w_dw[j], cv_b_dw[j], cv_ln_g[j], cv_ln_b[j], w2, cv_b_pw2[j], x, ml)
            if ctx_advance:
                u_c = modglu(xc, mc, norm_g[i, 1], w1, cv_b_pw1[j])
                yc_fn = lambda xcur: conformer_out(u_c, cv_w_dw[j], cv_b_dw[j], cv_ln_g[j], cv_ln_b[j], w2,
                                                   cv_b_pw2[j], xcur, mc)

        x = ffn(x, mods(i, 2, True), norm_g[i, 2], w_in1, w_out1, final_g=final_g if last else None)
        if ctx_advance:
            xc = yc_fn(xc)
            xc = ffn(xc, mods(i, 2, False), norm_g[i, 2], w_in1, w_out1)
    return x
```

```python
import functools
import math

import numpy as np
import jax
import jax.numpy as jnp
from jax import lax
from jax.experimental import pallas as pl
from jax.experimental.pallas import tpu as pltpu

F32 = jnp.float32
BF16 = jnp.bfloat16

GRID_W = 64
ATTN_HEADS = 8
ROPE_THETA = 10000.0
HY_FAST_DECAY = 0.3
HY_SLOW_DECAY = 1.5
HY_DECAY_TARGET = 1e-2
EPS = 1e-6
LN_EPS = 1e-5
N_MIXERS = 3
N_MOD = 9

LANES = 128
SUBLANES = 8
BF16_SUBLANES = 16
VMEM_LIMIT_BYTES = 56 * 1024 * 1024

MOD_ROWS = SUBLANES
HIGHEST = lax.Precision.HIGHEST


def _cparams(*sem):
    return pltpu.CompilerParams(dimension_semantics=sem, vmem_limit_bytes=VMEM_LIMIT_BYTES)


def _row_tile(n, want):
    t = min(n, want)
    assert n % t == 0, (n, t)
    return t


def _dot(a, b):
    return jnp.dot(a, b, preferred_element_type=F32)


def _dot32(a, b):
    return jnp.dot(a, b, preferred_element_type=F32, precision=HIGHEST)


def _rmsnorm(x, g):
    return x * lax.rsqrt(jnp.mean(x * x, axis=-1, keepdims=True) + EPS) * g


def _modulate(x, g, shift, scale):
    return _rmsnorm(x, g) * (1.0 + scale) + shift


def _silu(x):
    return x * jax.nn.sigmoid(x)


def _mod_kernel(r_ref, w_ref, b_ref, o_ref):
    r = _silu(r_ref[...]).astype(BF16)
    o_ref[0] = _dot(r, w_ref[0].astype(BF16)) + b_ref[0]


def mod_table(rows, w_mod, b_mod):
    depth, d, nm = w_mod.shape
    tn = nm // N_MOD
    return pl.pallas_call(
        _mod_kernel,
        grid=(depth, nm // tn),
        in_specs=[pl.BlockSpec((MOD_ROWS, d), lambda i, j: (0, 0)),
                  pl.BlockSpec((1, d, tn), lambda i, j: (i, 0, j)),
                  pl.BlockSpec((1, 1, tn), lambda i, j: (i, 0, j))],
        out_specs=pl.BlockSpec((1, MOD_ROWS, tn), lambda i, j: (i, 0, j)),
        out_shape=jax.ShapeDtypeStruct((depth, MOD_ROWS, nm), F32),
        compiler_params=_cparams("parallel", "parallel"),
        name="mod_table",
    )(rows, w_mod, b_mod.reshape(depth, 1, nm))


FFN_CHUNKS = 2


def _ffn_kernel(x_ref, mod_ref, g_ref, wg_ref, wu_ref, wo_ref, *rest, final):
    if final:
        fg_ref, o_ref = rest
    else:
        (o_ref,) = rest
    x = x_ref[0]
    h = _modulate(x, g_ref[...], mod_ref[0, 0:1, :], mod_ref[0, 1:2, :]).astype(BF16)
    tf = wo_ref.shape[0] // FFN_CHUNKS
    acc = None
    for c in range(FFN_CHUNKS):
        gate = _dot(h, wg_ref[:, c * tf:(c + 1) * tf])
        up = _dot(h, wu_ref[:, c * tf:(c + 1) * tf])
        act = (_silu(gate) * up).astype(BF16)
        part = _dot(act, wo_ref[c * tf:(c + 1) * tf, :])
        acc = part if acc is None else acc + part
    xn = x + 0.5 * mod_ref[0, 2:3, :] * acc
    if final:
        xn = _rmsnorm(xn, fg_ref[...])
    o_ref[0] = xn


def ffn(x, mod, g, w_in, w_out, final_g=None, tm=1024):
    b, n, d = x.shape
    ff = w_out.shape[0]
    tm = _row_tile(n, tm)
    assert (ff // FFN_CHUNKS) % LANES == 0
    final = final_g is not None
    resident = dict(pipeline_mode=pl.Buffered(1))
    in_specs = [pl.BlockSpec((1, tm, d), lambda bi, i: (bi, i, 0)),
                pl.BlockSpec((1, MOD_ROWS, d), lambda bi, i: (bi, 0, 0)),
                pl.BlockSpec((1, d), lambda bi, i: (0, 0)),
                pl.BlockSpec((d, ff), lambda bi, i: (0, 0), **resident),
                pl.BlockSpec((d, ff), lambda bi, i: (0, 1), **resident),
                pl.BlockSpec((ff, d), lambda bi, i: (0, 0), **resident)]
    args = [x, mod, g.reshape(1, d), w_in, w_in, w_out]
    if final:
        in_specs.append(pl.BlockSpec((1, d), lambda bi, i: (0, 0)))
        args.append(final_g.reshape(1, d))
    return pl.pallas_call(
        functools.partial(_ffn_kernel, final=final),
        grid=(b, n // tm),
        in_specs=in_specs,
        out_specs=pl.BlockSpec((1, tm, d), lambda bi, i: (bi, i, 0)),
        out_shape=jax.ShapeDtypeStruct((b, n, d), F32),
        compiler_params=_cparams("parallel", "parallel"),
        name="ffn",
    )(*args)


def _rope_cols(y, cos, sa, sb):
    outs = []
    for c in range(y.shape[1] // LANES):
        yc = y[:, c * LANES:(c + 1) * LANES]
        outs.append(yc * cos + pltpu.roll(yc, LANES - 16, 1) * sa + pltpu.roll(yc, 16, 1) * sb)
    return jnp.concatenate(outs, axis=1)


def _modproj_kernel(x_ref, mod_ref, g_ref, w_ref, *rest, rope_blocks):
    if rope_blocks:
        cos_ref, sa_ref, sb_ref, o_ref, h_scr = rest
    else:
        o_ref, h_scr = rest
    j = pl.program_id(2)

    @pl.when(j == 0)
    def _():
        h = _modulate(x_ref[0], g_ref[...], mod_ref[0, 0:1, :], mod_ref[0, 1:2, :])
        h_scr[...] = h.astype(BF16)

    y = _dot(h_scr[...], w_ref[...])
    if rope_blocks:
        @pl.when(j < rope_blocks)
        def _():
            o_ref[0] = _rope_cols(y, cos_ref[...], sa_ref[...], sb_ref[...]).astype(o_ref.dtype)

        @pl.when(j >= rope_blocks)
        def _():
            o_ref[0] = y.astype(o_ref.dtype)
    else:
        o_ref[0] = y.astype(o_ref.dtype)


def modproj(x, mod, g, w, rope=None, rope_blocks=0, tm=512, tn=1024):
    b, n, d = x.shape
    nout = w.shape[1]
    tm = _row_tile(n, tm)
    in_specs = [pl.BlockSpec((1, tm, d), lambda bi, i, j: (bi, i, 0)),
                pl.BlockSpec((1, MOD_ROWS, d), lambda bi, i, j: (bi, 0, 0)),
                pl.BlockSpec((1, d), lambda bi, i, j: (0, 0)),
                pl.BlockSpec((d, tn), lambda bi, i, j: (0, j))]
    args = [x, mod, g.reshape(1, d), w]
    if rope_blocks:
        in_specs += [pl.BlockSpec((tm, LANES), lambda bi, i, j: (i, 0))] * 3
        args += list(rope)
    return pl.pallas_call(
        functools.partial(_modproj_kernel, rope_blocks=rope_blocks),
        grid=(b, n // tm, nout // tn),
        in_specs=in_specs,
        out_specs=pl.BlockSpec((1, tm, tn), lambda bi, i, j: (bi, i, j)),
        out_shape=jax.ShapeDtypeStruct((b, n, nout), BF16),
        scratch_shapes=[pltpu.VMEM((tm, d), BF16)],
        compiler_params=_cparams("parallel", "parallel", "arbitrary"),
        name="modproj",
    )(*args)


def rope_tables(n):
    rows = n // GRID_W
    row = jnp.repeat(jnp.arange(rows), GRID_W).astype(F32)
    col = jnp.tile(jnp.arange(GRID_W), rows).astype(F32)
    quarter = 16
    half = 32
    inv = ROPE_THETA ** (-(2.0 * jnp.arange(quarter, dtype=F32)) / half)
    ang_r = row[:, None] * inv
    ang_c = col[:, None] * inv
    zero = jnp.zeros_like(ang_r)
    cr, sr, cc, sc = jnp.cos(ang_r), jnp.sin(ang_r), jnp.cos(ang_c), jnp.sin(ang_c)
    cos64 = jnp.concatenate([cr, cr, cc, cc], axis=1)
    sa64 = jnp.concatenate([-sr, zero, -sc, zero], axis=1)
    sb64 = jnp.concatenate([zero, sr, zero, sc], axis=1)
    tile2 = lambda t: jnp.concatenate([t, t], axis=1)
    return tile2(cos64), tile2(sa64), tile2(sb64)


V_ROWS = LANES + BF16_SUBLANES
PIPE_UNROLL = 8


def _attn_kernel(q_ref, k_ref, v_ref, lam_ref, g_ref, o_ref, qm_scr, sa_scr, sb_scr, ma_scr, mb_scr, m_scr, acc_scr, *,
                 nkt, nq, tq, lam_init, half):
    lane = lax.broadcasted_iota(jnp.int32, (tq, LANES), 1)
    for qi in range(nq):
        q = q_ref[0, qi * tq:(qi + 1) * tq, :].astype(F32) * (half ** -0.5 * math.log2(math.e))
        qm_scr[qi, 0] = jnp.where(lane < half, q, 0.0).astype(BF16)
        qm_scr[qi, 1] = jnp.where(lane >= half, q, 0.0).astype(BF16)
    m_scr[...] = jnp.full(m_scr.shape, -jnp.inf, F32)
    acc_scr[...] = jnp.zeros_like(acc_scr)

    def scores(u, bufs):
        s_ref, mt_ref = bufs
        qi, t = u // nkt, u % nkt
        k = k_ref[0, 0, t]
        for mi in range(2):
            s = lax.dot_general(k, qm_scr[qi, mi], (((1,), (1,)), ((), ())),
                                preferred_element_type=F32)
            s_ref[mi] = s
            mt_ref[mi] = jnp.max(s, axis=0, keepdims=True)

    def softmax_pv(u, bufs):
        s_ref, mt_ref = bufs
        qi, t = u // nkt, u % nkt
        vt = v_ref[0, 0, t]
        for mi in range(2):
            s = s_ref[mi]
            m_old = m_scr[qi, mi]
            m_new = jnp.maximum(m_old, mt_ref[mi])
            alpha = jnp.exp2(m_old - m_new)
            p = jnp.exp2(s - m_new).astype(BF16)
            acc_scr[qi, mi] = alpha * acc_scr[qi, mi] + _dot(vt, p)
            m_scr[qi, mi] = m_new

    units = nq * nkt
    bufs = ((sa_scr, ma_scr), (sb_scr, mb_scr))
    scores(0, bufs[0])

    def steps(first, count):
        for i in range(count):
            scores(first + i + 1, bufs[(i + 1) % 2])
            softmax_pv(first + i, bufs[i % 2])

    n_steps = units - 1
    n_loop = n_steps // PIPE_UNROLL

    def body(j, carry):
        steps(PIPE_UNROLL * j, PIPE_UNROLL)
        return carry

    lax.fori_loop(0, n_loop, body, 0)
    steps(n_loop * PIPE_UNROLL, n_steps - n_loop * PIPE_UNROLL)
    softmax_pv(units - 1, bufs[(units - 1) % 2])

    lv = lam_ref[...]
    lam = (jnp.exp(jnp.sum(lv[0:1] * lv[1:2], axis=1, keepdims=True))
           - jnp.exp(jnp.sum(lv[2:3] * lv[3:4], axis=1, keepdims=True)) + lam_init)
    for qi in range(nq):
        a1 = acc_scr[qi, 0]
        a2 = acc_scr[qi, 1]
        ot = a1[:LANES] / a1[LANES:LANES + 1] - lam * (a2[:LANES] / a2[LANES:LANES + 1])
        o = _rmsnorm(ot.T, g_ref[...]) * (1.0 - lam_init)
        o_ref[0, qi * tq:(qi + 1) * tq, :] = o.astype(o_ref.dtype)


def diff_attention(q, kt, vt, lam_vecs, subln_g, lam_init, tq=256, nq=4):
    b, n, dq = q.shape
    _, h, nkt, tk, _ = kt.shape
    tq = _row_tile(n, tq)
    nq = min(nq, n // tq)
    tb = nq * tq
    assert n % tb == 0
    return pl.pallas_call(
        functools.partial(_attn_kernel, nkt=nkt, nq=nq, tq=tq, lam_init=lam_init, half=LANES // 2),
        grid=(b, h, n // tb),
        in_specs=[pl.BlockSpec((1, tb, LANES), lambda bi, hi, i: (bi, i, hi)),
                  pl.BlockSpec((1, 1, nkt, tk, LANES), lambda bi, hi, i: (bi, hi, 0, 0, 0)),
                  pl.BlockSpec((1, 1, nkt, V_ROWS, tk), lambda bi, hi, i: (bi, hi, 0, 0, 0)),
                  pl.BlockSpec((4, LANES // 2), lambda bi, hi, i: (0, 0)),
                  pl.BlockSpec((1, LANES), lambda bi, hi, i: (0, 0))],
        out_specs=pl.BlockSpec((1, tb, LANES), lambda bi, hi, i: (bi, i, hi)),
        out_shape=jax.ShapeDtypeStruct((b, n, dq), BF16),
        scratch_shapes=[pltpu.VMEM((nq, 2, tq, LANES), BF16),
                        pltpu.VMEM((2, tk, tq), F32), pltpu.VMEM((2, tk, tq), F32),
                        pltpu.VMEM((2, 1, tq), F32), pltpu.VMEM((2, 1, tq), F32),
                        pltpu.VMEM((nq, 2, 1, tq), F32), pltpu.VMEM((nq, 2, V_ROWS, tq), F32)],
        compiler_params=_cparams("parallel", "parallel", "arbitrary"),
        name="diff_attention",
    )(q, kt, vt, lam_vecs, subln_g.reshape(1, LANES))


def _key_value_tiles(k, v, tk):
    b, nk, _ = k.shape
    h = ATTN_HEADS
    nkt = nk // tk
    kt = k.reshape(b, nkt, tk, h, LANES).transpose(0, 3, 1, 2, 4)
    v5 = v.reshape(b, nkt, tk, h, LANES).transpose(0, 3, 1, 4, 2)
    ones = jnp.ones((b, h, nkt, 1, tk), BF16)
    zeros = jnp.zeros((b, h, nkt, V_ROWS - LANES - 1, tk), BF16)
    return kt, jnp.concatenate([v5, ones, zeros], axis=3)


def _resid_out(x_ref, mod_ref, y, o_ref):
    o_ref[0] = x_ref[0] + mod_ref[0, 2:3, :] * y


def _attn_out_kernel(a_ref, w_ref, x_ref, mod_ref, o_ref):
    _resid_out(x_ref, mod_ref, _dot(a_ref[0], w_ref[...]), o_ref)


def attn_out(a, w, x, mod, tm=512):
    b, n, d = x.shape
    tm = _row_tile(n, tm)
    row = pl.BlockSpec((1, tm, d), lambda bi, i: (bi, i, 0))
    return pl.pallas_call(
        _attn_out_kernel,
        grid=(b, n // tm),
        in_specs=[row, pl.BlockSpec((d, d), lambda bi, i: (0, 0)), row,
                  pl.BlockSpec((1, MOD_ROWS, d), lambda bi, i: (bi, 0, 0))],
        out_specs=row,
        out_shape=jax.ShapeDtypeStruct((b, n, d), F32),
        compiler_params=_cparams("parallel", "parallel"),
        name="attn_out",
    )(a, w, x, mod)


def _hyena_out_kernel(y_ref, vg_ref, x0_ref, skip_ref, w_ref, b_ref, x_ref, mod_ref, o_ref):
    a = ((y_ref[0] + vg_ref[0] * skip_ref[...]) * x0_ref[0]).astype(BF16)
    _resid_out(x_ref, mod_ref, _dot(a, w_ref[...]) + b_ref[...], o_ref)


def hyena_out(y, vg, x0, skip, w, bias, x, mod, tm=512):
    b, n, d = x.shape
    tm = _row_tile(n, tm)
    row = pl.BlockSpec((1, tm, d), lambda bi, i: (bi, i, 0))
    vec = pl.BlockSpec((1, d), lambda bi, i: (0, 0))
    return pl.pallas_call(
        _hyena_out_kernel,
        grid=(b, n // tm),
        in_specs=[row, row, row, vec, pl.BlockSpec((d, d), lambda bi, i: (0, 0)), vec, row,
                  pl.BlockSpec((1, MOD_ROWS, d), lambda bi, i: (bi, 0, 0))],
        out_specs=row,
        out_shape=jax.ShapeDtypeStruct((b, n, d), F32),
        compiler_params=_cparams("parallel", "parallel"),
        name="hyena_out",
    )(y, vg, x0, skip.reshape(1, d), w, bias.reshape(1, d), x, mod)


def _fill_window(ext_ref, prev_ref, cur_ref, next_ref, halo, tm, i, last):
    ext_ref[0:halo, :] = jnp.where(i == 0, 0.0, prev_ref[0])
    ext_ref[halo:halo + tm, :] = cur_ref[0]
    ext_ref[halo + tm:halo + tm + halo, :] = jnp.where(i == last, 0.0, next_ref[0])


def _dwconv(ext_ref, w_ref, halo, tm, z_scr):
    width = w_ref.shape[0]
    pad = (width - 1) // 2
    offs = [halo - pad + j for j in range(width)]
    acc = None
    rows = z_scr.shape[0]
    for r in sorted({o % SUBLANES for o in offs}):
        z_scr[...] = ext_ref[pl.ds(r, rows), :]
        for j, o in enumerate(offs):
            if o % SUBLANES == r:
                term = w_ref[j:j + 1, :] * z_scr[pl.ds(o - r, tm), :]
                acc = term if acc is None else acc + term
    return acc


def _halo_specs(tm, halo, n, c):
    per = tm // halo
    nblk = n // halo
    prev = pl.BlockSpec((1, halo, c), lambda bi, i: (bi, jnp.maximum(i * per - 1, 0), 0))
    cur = pl.BlockSpec((1, tm, c), lambda bi, i: (bi, i, 0))
    nxt = pl.BlockSpec((1, halo, c), lambda bi, i: (bi, jnp.minimum((i + 1) * per, nblk - 1), 0))
    return [prev, cur, nxt]


def _hyena_in_kernel(prev_ref, cur_ref, next_ref, mod_ref, g_ref, w_ref, bin_ref, ws_ref, bs_ref, x0_ref, vg_ref,
                     xe_scr, ext_scr, *, halo, tm, d, last):
    i = pl.program_id(1)
    xe_scr[0:halo, :] = prev_ref[0]
    xe_scr[halo:halo + tm, :] = cur_ref[0]
    xe_scr[halo + tm:halo + tm + halo, :] = next_ref[0]
    h = _modulate(xe_scr[...], g_ref[...], mod_ref[0, 0:1, :], mod_ref[0, 1:2, :]).astype(BF16)
    row = lax.broadcasted_iota(jnp.int32, (tm + 2 * halo, 1), 0)
    inside = ((row >= halo) | (i > 0)) & ((row < halo + tm) | (i < last))
    pad = (ws_ref.shape[0] - 1) // 2
    parts = []
    for c in range(3):
        cols = slice(c * d, (c + 1) * d)
        u = _dot(h, w_ref[:, cols]) + bin_ref[:, cols]
        ext_scr[...] = jnp.where(inside, u, 0.0)
        acc = bs_ref[:, cols]
        for j in range(ws_ref.shape[0]):
            acc = acc + ws_ref[j:j + 1, cols] * ext_scr[pl.ds(halo - pad + j, tm), :]
        parts.append(acc)
    x0_ref[0] = parts[0]
    vg_ref[0] = parts[2] * parts[1]


def hyena_in(x, mod, g, w_in, b_in, w_short, b_short, tm=512):
    b, n, d = x.shape
    c = w_in.shape[1]
    tm = _row_tile(n, tm)
    halo = SUBLANES
    row = pl.BlockSpec((1, tm, d), lambda bi, i: (bi, i, 0))
    vec = pl.BlockSpec((1, c), lambda bi, i: (0, 0))
    out = jax.ShapeDtypeStruct((b, n, d), F32)
    return pl.pallas_call(
        functools.partial(_hyena_in_kernel, halo=halo, tm=tm, d=d, last=n // tm - 1),
        grid=(b, n // tm),
        in_specs=_halo_specs(tm, halo, n, d) + [pl.BlockSpec((1, MOD_ROWS, d), lambda bi, i: (bi, 0, 0)),
                                                pl.BlockSpec((1, d), lambda bi, i: (0, 0)),
                                                pl.BlockSpec((d, c), lambda bi, i: (0, 0),
                                                             pipeline_mode=pl.Buffered(1)),
                                                vec, pl.BlockSpec(w_short.shape, lambda bi, i: (0, 0)), vec],
        out_specs=[row, row],
        out_shape=[out, out],
        scratch_shapes=[pltpu.VMEM((tm + 2 * halo, d), F32), pltpu.VMEM((tm + 2 * halo, d), F32)],
        compiler_params=_cparams("parallel", "parallel"),
        name="hyena_in",
    )(x, x, x, mod, g.reshape(1, d), w_in, b_in.reshape(1, c), w_short, b_short.reshape(1, c))


def _conformer_out_kernel(prev_ref, cur_ref, next_ref, wdw_ref, bdw_ref, lg_ref, lb_ref, w_ref, b_ref,
                          x_ref, mod_ref, o_ref, ext_ref, z_scr, *, halo, tm, d, last):
    i = pl.program_id(1)
    _fill_window(ext_ref, prev_ref, cur_ref, next_ref, halo, tm, i, last)
    u = _dwconv(ext_ref, wdw_ref, halo, tm, z_scr) + bdw_ref[...]
    mu = jnp.mean(u, axis=-1, keepdims=True)
    uc = u - mu
    var = jnp.mean(uc * uc, axis=-1, keepdims=True)
    z = _silu(uc * lax.rsqrt(var + LN_EPS) * lg_ref[...] + lb_ref[...]).astype(BF16)
    _resid_out(x_ref, mod_ref, _dot(z, w_ref[...]) + b_ref[...], o_ref)


def conformer_out(u, w_dw, b_dw, ln_g, ln_b, w, bias, x, mod, tm=256):
    b, n, d = x.shape
    tm = _row_tile(n, tm)
    halo = 2 * SUBLANES
    assert (w_dw.shape[0] - 1) // 2 <= halo
    row = pl.BlockSpec((1, tm, d), lambda bi, i: (bi, i, 0))
    vec = pl.BlockSpec((1, d), lambda bi, i: (0, 0))
    return pl.pallas_call(
        functools.partial(_conformer_out_kernel, halo=halo, tm=tm, d=d, last=n // tm - 1),
        grid=(b, n // tm),
        in_specs=_halo_specs(tm, halo, n, d) + [pl.BlockSpec(w_dw.shape, lambda bi, i: (0, 0)), vec, vec, vec,
                                                pl.BlockSpec((d, d), lambda bi, i: (0, 0)), vec, row,
                                                pl.BlockSpec((1, MOD_ROWS, d), lambda bi, i: (bi, 0, 0))],
        out_specs=row,
        out_shape=jax.ShapeDtypeStruct((b, n, d), F32),
        scratch_shapes=[pltpu.VMEM((tm + 2 * halo, d), F32), pltpu.VMEM((tm + 2 * halo - SUBLANES, d), F32)],
        compiler_params=_cparams("parallel", "parallel"),
        name="conformer_out",
    )(u, u, u, w_dw, b_dw.reshape(1, d), ln_g.reshape(1, d), ln_b.reshape(1, d), w, bias.reshape(1, d), x, mod)


def _modglu_kernel(x_ref, mod_ref, g_ref, wa_ref, wg_ref, ba_ref, bg_ref, o_ref):
    h = _modulate(x_ref[0], g_ref[...], mod_ref[0, 0:1, :], mod_ref[0, 1:2, :]).astype(BF16)
    a = _dot(h, wa_ref[...]) + ba_ref[...]
    gt = _dot(h, wg_ref[...]) + bg_ref[...]
    o_ref[0] = a * jax.nn.sigmoid(gt)


def modglu(x, mod, g, w, bias, tm=512):
    b, n, d = x.shape
    tm = _row_tile(n, tm)
    row = pl.BlockSpec((1, tm, d), lambda bi, i: (bi, i, 0))
    bias2 = bias.reshape(1, 2 * d)
    return pl.pallas_call(
        _modglu_kernel,
        grid=(b, n // tm),
        in_specs=[row, pl.BlockSpec((1, MOD_ROWS, d), lambda bi, i: (bi, 0, 0)),
                  pl.BlockSpec((1, d), lambda bi, i: (0, 0)),
                  pl.BlockSpec((d, d), lambda bi, i: (0, 0)), pl.BlockSpec((d, d), lambda bi, i: (0, 1)),
                  pl.BlockSpec((1, d), lambda bi, i: (0, 0)), pl.BlockSpec((1, d), lambda bi, i: (0, 1))],
        out_specs=row,
        out_shape=jax.ShapeDtypeStruct((b, n, d), F32),
        compiler_params=_cparams("parallel", "parallel"),
        name="modglu",
    )(x, mod, g.reshape(1, d), w, w, bias2, bias2)


def _filter_kernel(emb_ref, t_ref, w1_ref, b1_ref, w2_ref, b2_ref, w3_ref, b3_ref, fr_ref, w4_ref, dl_ref, o_ref):
    fr = fr_ref[...]
    hdn = jnp.sin(fr * (_dot32(emb_ref[...], w1_ref[...]) + b1_ref[...]))
    hdn = jnp.sin(fr * (_dot32(hdn, w2_ref[...]) + b2_ref[...]))
    hdn = jnp.sin(fr * (_dot32(hdn, w3_ref[...]) + b3_ref[...]))
    h = _dot32(hdn, w4_ref[...]) * jnp.exp(-t_ref[...] * dl_ref[...])
    row = lax.broadcasted_iota(jnp.int32, h.shape, 0)
    col = lax.broadcasted_iota(jnp.int32, h.shape, 1)
    drop = (row == 0) & (pl.program_id(0) == 0) & (col >= h.shape[1] // 2)
    o_ref[...] = jnp.where(drop, 0.0, h)


def hyena_filter(L, w1, b1, w2, b2, w3, b3, freq, w4, d_model, tl=256):
    bands = (w1.shape[0] - 1) // 2
    hid = w1.shape[1]
    t = jnp.linspace(0.0, 1.0, L, dtype=F32)[:, None]
    wpos = (2.0 * math.pi / L) * jnp.arange(L, dtype=F32)
    bnd = jnp.linspace(1e-4, bands - 1, bands, dtype=F32)
    fw = wpos[:, None] * bnd[None, :]
    emb = jnp.concatenate([t, jnp.cos(fw), -jnp.sin(fw)], axis=-1)
    max_decay = math.log(HY_DECAY_TARGET) / HY_FAST_DECAY
    min_decay = math.log(HY_DECAY_TARGET) / HY_SLOW_DECAY
    deltas = jnp.abs(jnp.linspace(min_decay, max_decay, d_model, dtype=F32))
    dl2 = jnp.concatenate([deltas, deltas])[None, :]
    tl = _row_tile(L, tl)
    td = 2 * d_model
    full = lambda a: pl.BlockSpec(a.shape, lambda i, j: (0, 0))
    vec = lambda a: a.reshape(1, -1)
    args = [emb, t, w1, vec(b1), w2, vec(b2), w3, vec(b3), vec(freq)]
    return pl.pallas_call(
        _filter_kernel,
        grid=(L // tl, 2 * d_model // td),
        in_specs=[pl.BlockSpec((tl, emb.shape[1]), lambda i, j: (i, 0)), pl.BlockSpec((tl, 1), lambda i, j: (i, 0))]
                 + [full(a) for a in args[2:]]
                 + [pl.BlockSpec((hid, td), lambda i, j: (0, j)), pl.BlockSpec((1, td), lambda i, j: (0, j))],
        out_specs=pl.BlockSpec((tl, td), lambda i, j: (i, j)),
        out_shape=jax.ShapeDtypeStruct((L, 2 * d_model), F32),
        compiler_params=_cparams("parallel", "parallel"),
        name="hyena_filter",
    )(*args, w4, dl2)


FFT_N2 = 128
DFT_BLOCKS_PER_STEP = 2
DFT_K1_PER_STEP = 4


def _split_bf16(a):
    hi = a.astype(BF16)
    return hi, (a - hi.astype(F32)).astype(BF16)


def _dotp_split(ah, al, b):
    bh, bl = _split_bf16(b)
    return _dot(ah, bh) + (_dot(ah, bl) + _dot(al, bh))


def _dotp(a, b):
    return _dotp_split(*_split_bf16(a), b)


def _to_block_major(x, n2):
    bx, L, c = x.shape
    r = L // n2
    return x.reshape(bx, r, n2 // SUBLANES, SUBLANES, c).transpose(0, 2, 1, 3, 4).reshape(
        bx, n2 // SUBLANES, r * SUBLANES, c)


def _from_block_major(y, n2):
    bx, na, rows8, c = y.shape
    r = rows8 // SUBLANES
    return y.reshape(bx, na, r, SUBLANES, c).transpose(0, 2, 1, 3, 4).reshape(bx, r * n2, c)


def _major_kernel(m_ref, x_ref, o_ref):
    mh, ml = _split_bf16(m_ref[...])
    rows_out, rows_in = m_ref.shape
    for a in range(x_ref.shape[1]):
        x2 = x_ref.at[0, a]
        o2 = o_ref.at[0, a]
        for s in range(SUBLANES):
            y = _dotp_split(mh, ml, x2[pl.ds(s, rows_in, stride=SUBLANES), :])
            o2[pl.ds(s, rows_out, stride=SUBLANES), :] = y


def dft_major(mat, x):
    bx, na, rows8, c = x.shape
    m, r = mat.shape
    assert rows8 == r * SUBLANES
    tc = LANES
    ka = DFT_BLOCKS_PER_STEP if na % DFT_BLOCKS_PER_STEP == 0 else 1
    return pl.pallas_call(
        _major_kernel,
        grid=(bx, na // ka, c // tc),
        in_specs=[pl.BlockSpec((m, r), lambda bi, a, j: (0, 0)),
                  pl.BlockSpec((1, ka, rows8, tc), lambda bi, a, j: (bi, a, 0, j))],
        out_specs=pl.BlockSpec((1, ka, m * SUBLANES, tc), lambda bi, a, j: (bi, a, 0, j)),
        out_shape=jax.ShapeDtypeStruct((bx, na, m * SUBLANES, c), F32),
        compiler_params=_cparams("parallel", "parallel", "parallel"),
        name="dft_major",
    )(mat, x)


def _twiddled_block(f_ref, tw_ref, q):
    fr, fi = f_ref[0], f_ref[1]
    twr, twi = tw_ref[q, 0:1, :], tw_ref[q, 1:2, :]
    gr = fr * twr - fi * twi
    gi = fr * twi + fi * twr
    return jnp.concatenate([jnp.concatenate([gr, -gi], axis=1), jnp.concatenate([gi, gr], axis=1)], axis=0)


def _gather_k1(ref, q, n2):
    return jnp.concatenate([ref[0, :, p, q].reshape(n2, -1) for p in range(2)], axis=0)


def _scatter_k1(ref, q, x, n2):
    for p in range(2):
        ref[0, :, p, q] = x[p * n2:(p + 1) * n2].reshape(n2 // SUBLANES, SUBLANES, -1)


def _spec_taps_kernel(f_ref, tw_ref, af_ref, ab_ref, o_ref):
    n2 = f_ref.shape[1]
    for q in range(tw_ref.shape[0]):
        g = _twiddled_block(f_ref, tw_ref, q)
        xf = _dotp(g, _gather_k1(af_ref, q, n2))
        xb = _dotp(g, _gather_k1(ab_ref, q, n2))
        _scatter_k1(o_ref, q, jnp.concatenate([xf[:n2] + xb[:n2], xf[n2:] - xb[n2:]], axis=0), n2)


def _spec_mid_kernel(f_ref, tw_ref, a_ref, h_ref, o_ref):
    n2 = f_ref.shape[1]
    for q in range(tw_ref.shape[0]):
        g = _twiddled_block(f_ref, tw_ref, q)
        x = _dotp(g, _gather_k1(a_ref, q, n2))
        h = _gather_k1(h_ref, q, n2)
        xr, xi = x[:n2], x[n2:]
        hr, hi = h[:n2], h[n2:]
        z = jnp.concatenate([xr * hr - xi * hi, xr * hi + xi * hr], axis=0)
        _scatter_k1(o_ref, q, _dotp(g.T, z), n2)


def spec_stage(fmat, tw, a, h=None, tc=1024):
    bx, na, rows8, ca = a.shape
    n2 = na * SUBLANES
    n1 = rows8 // (2 * SUBLANES)
    c = ca // 2 if h is None else ca
    tc = _row_tile(c, tc)
    nj = c // tc
    kk = DFT_K1_PER_STEP if n1 % DFT_K1_PER_STEP == 0 else 1
    six = lambda t: t.reshape(t.shape[0], na, 2, n1, SUBLANES, t.shape[-1])
    blk = lambda off: pl.BlockSpec((1, na, 2, kk, SUBLANES, tc), lambda k, j, bi: (bi, 0, 0, k, 0, off + j))
    in_specs = [pl.BlockSpec((2, n2, n2), lambda k, j, bi: (0, 0, 0)),
                pl.BlockSpec((kk, 2, n2), lambda k, j, bi: (k, 0, 0)), blk(0)]
    if h is None:
        kern, args = _spec_taps_kernel, [fmat, tw, six(a), six(a)]
        in_specs.append(blk(nj))
    else:
        kern, args = _spec_mid_kernel, [fmat, tw, six(a), six(h)]
        in_specs.append(pl.BlockSpec((1, na, 2, kk, SUBLANES, tc), lambda k, j, bi: (0, 0, 0, k, 0, j)))
    out = pl.pallas_call(
        kern,
        grid=(n1 // kk, nj, bx),
        in_specs=in_specs,
        out_specs=blk(0),
        out_shape=jax.ShapeDtypeStruct((bx, na, 2, n1, SUBLANES, c), F32),
        compiler_params=_cparams("parallel", "parallel", "arbitrary"),
        name="dft_minor",
    )(*args)
    return out.reshape(bx, na, rows8, c)


def _dft_constants(n1, n2):
    n = n1 * n2
    k1 = np.arange(n1, dtype=np.float64)
    ang1 = -2.0 * np.pi * np.outer(k1, k1) / n1
    w1r, w1i = np.cos(ang1), np.sin(ang1)
    k2 = np.arange(n2, dtype=np.float64)
    ang2 = -2.0 * np.pi * np.outer(k2, k2) / n2
    fmat = np.stack([np.cos(ang2), np.sin(ang2)])
    angt = -2.0 * np.pi * np.outer(k1, k2) / n
    tw = np.stack([np.cos(angt), np.sin(angt)], axis=1)
    fwd_half = np.concatenate([w1r[:, :n1 // 2], w1i[:, :n1 // 2]], axis=0)
    inv_half = np.concatenate([w1r[:n1 // 2], w1i[:n1 // 2]], axis=1) / n
    f = lambda a: jnp.asarray(a, F32)
    return f(fwd_half), f(inv_half), f(fmat), f(tw)


def long_conv(v, taps):
    b, L, c = v.shape
    n2 = FFT_N2
    n1 = 2 * L // n2
    fwd_half, inv_half, fmat, tw = _dft_constants(n1, n2)
    hs = spec_stage(fmat, tw, dft_major(fwd_half, _to_block_major(taps[None], n2)))
    a = dft_major(fwd_half, _to_block_major(v, n2))
    y = dft_major(inv_half, spec_stage(fmat, tw, a, hs))
    return _from_block_major(y, n2)


def _dense_conv_kernel(f_ref, g_ref, v_ref, hf_ref, hb_ref, o_ref):
    n = f_ref.shape[0] // 2
    f = f_ref[...]
    x = _dotp(f, v_ref[0])
    hf = _dotp(f, hf_ref[...])
    hb = _dotp(f, hb_ref[...])
    xr, xi = x[:n], x[n:]
    hr = hf[:n] + hb[:n]
    hi = hf[n:] - hb[n:]
    z = jnp.concatenate([xr * hr - xi * hi, xr * hi + xi * hr], axis=0)
    o_ref[0] = _dotp(g_ref[...], z)


def dense_long_conv(v, taps, tc=512):
    b, L, c = v.shape
    n = 2 * L
    ang = 2.0 * np.pi * np.outer(np.arange(n, dtype=np.float64), np.arange(L, dtype=np.float64)) / n
    fmat = jnp.asarray(np.concatenate([np.cos(ang), -np.sin(ang)], axis=0), F32)
    gmat = jnp.asarray(np.concatenate([np.cos(ang.T), -np.sin(ang.T)], axis=1) / n, F32)
    tc = _row_tile(c, tc)
    nj = c // tc
    return pl.pallas_call(
        _dense_conv_kernel,
        grid=(b, nj),
        in_specs=[pl.BlockSpec((2 * n, L), lambda bi, j: (0, 0)), pl.BlockSpec((L, 2 * n), lambda bi, j: (0, 0)),
                  pl.BlockSpec((1, L, tc), lambda bi, j: (bi, 0, j)),
                  pl.BlockSpec((L, tc), lambda bi, j: (0, j)), pl.BlockSpec((L, tc), lambda bi, j: (0, nj + j))],
        out_specs=pl.BlockSpec((1, L, tc), lambda bi, j: (bi, 0, j)),
        out_shape=jax.ShapeDtypeStruct((b, L, c), F32),
        compiler_params=_cparams("parallel", "parallel"),
        name="dense_long_conv",
    )(fmat, gmat, v, taps, taps)


def _pad_mod(m3):
    return jnp.pad(m3, ((0, 0), (0, MOD_ROWS - 3), (0, 0)))


def kernel(x, c, ctx, c_ctx, w_mod, b_mod, norm_g, w_ffn_in, w_ffn_out, attn_w_qkv, attn_w_o, attn_lambda,
           attn_subln_g, hy_w_in, hy_b_in, hy_w_short, hy_b_short, hy_f_w1, hy_f_b1, hy_f_w2, hy_f_b2, hy_f_w3,
           hy_f_b3, hy_f_freq, hy_f_w4, hy_skip, hy_w_out, hy_b_out, cv_w_pw1, cv_b_pw1, cv_w_dw, cv_b_dw,
           cv_ln_g, cv_ln_b, cv_w_pw2, cv_b_pw2, final_g):
    bsz, n_lat, d = x.shape
    n_ctx = ctx.shape[1]
    depth = w_mod.shape[0]
    assert bsz + 1 <= MOD_ROWS

    rows = jnp.concatenate([c, c_ctx[None, :], jnp.zeros((MOD_ROWS - bsz - 1, d), F32)], axis=0)
    table = mod_table(rows, w_mod, b_mod).reshape(depth, MOD_ROWS, N_MOD, d)

    def mods(i, s, latent):
        r = table[i, :bsz] if latent else jnp.broadcast_to(table[i, bsz:bsz + 1], (bsz, N_MOD, d))
        return _pad_mod(r[:, 3 * s:3 * s + 3])

    bf = lambda w: w.astype(BF16)
    xc = ctx
    tk_lat = 1280 if (n_lat + n_ctx) % 1280 == 0 else n_ctx
    rope = rope_tables(n_lat)

    for i in range(depth):
        kind = i % N_MIXERS
        j = i // N_MIXERS
        last = i == depth - 1
        ctx_in_use = (not last) or kind == 0
        ctx_advance = not last
        w_in0, w_out0 = bf(w_ffn_in[i, 0]), bf(w_ffn_out[i, 0])
        w_in1, w_out1 = bf(w_ffn_in[i, 1]), bf(w_ffn_out[i, 1])

        x = ffn(x, mods(i, 0, True), norm_g[i, 0], w_in0, w_out0)
        if ctx_in_use:
            xc = ffn(xc, mods(i, 0, False), norm_g[i, 0], w_in0, w_out0)

        ml, mc = mods(i, 1, True), mods(i, 1, False)
        if kind == 0:
            lam_init = 0.8 - 0.6 * math.exp(-0.3 * i)
            wqkv = bf(attn_w_qkv[j])
            wo = bf(attn_w_o[j])
            qkv_l = modproj(x, ml, norm_g[i, 1], wqkv, rope=rope, rope_blocks=2)
            qkv_c = modproj(xc, mc, norm_g[i, 1], wqkv)
            k_all = jnp.concatenate([qkv_l[:, :, d:2 * d], qkv_c[:, :, d:2 * d]], axis=1)
            v_all = jnp.concatenate([qkv_l[:, :, 2 * d:], qkv_c[:, :, 2 * d:]], axis=1)
            kt, vt = _key_value_tiles(k_all, v_all, tk_lat)
            o_l = diff_attention(qkv_l[:, :, :d], kt, vt, attn_lambda[j], attn_subln_g[j], lam_init)
            x = attn_out(o_l, wo, x, ml)
            if ctx_advance:
                ktc, vtc = _key_value_tiles(qkv_c[:, :, d:2 * d], qkv_c[:, :, 2 * d:], n_ctx)
                o_c = diff_attention(qkv_c[:, :, :d], ktc, vtc, attn_lambda[j], attn_subln_g[j], lam_init)
                yc_fn = lambda xcur: attn_out(o_c, wo, xcur, mc)
        elif kind == 1:
            w_in, w_o = bf(hy_w_in[j]), bf(hy_w_out[j])
            filt = (hy_f_w1[j], hy_f_b1[j], hy_f_w2[j], hy_f_b2[j], hy_f_w3[j], hy_f_b3[j], hy_f_freq[j], hy_f_w4[j])

            def hyena(xs, mod, L):
                x0, vg = hyena_in(xs, mod, norm_g[i, 1], w_in, hy_b_in[j], hy_w_short[j], hy_b_short[j])
                taps = hyena_filter(L, *filt, d)
                y = long_conv(vg, taps) if L == n_lat else dense_long_conv(vg, taps)
                return y, vg, x0

            y_l, vg_l, x0_l = hyena(x, ml, n_lat)
            x = hyena_out(y_l, vg_l, x0_l, hy_skip[j], w_o, hy_b_out[j], x, ml)
            if ctx_advance:
                y_c, vg_c, x0_c = hyena(xc, mc, n_ctx)
                yc_fn = lambda xcur: hyena_out(y_c, vg_c, x0_c, hy_skip[j], w_o, hy_b_out[j], xcur, mc)
        else:
            w1, w2 = bf(cv_w_pw1[j]), bf(cv_w_pw2[j])
            u_l = modglu(x, ml, norm_g[i, 1], w1, cv_b_pw1[j])
            x = conformer_out(u_l, cv_w_dw[j], cv_b_dw[j], cv_ln_g[j], cv_ln_b[j], w2, cv_b_pw2[j], x, ml)
            if ctx_advance:
                u_c = modglu(xc, mc, norm_g[i, 1], w1, cv_b_pw1[j])
                yc_fn = lambda xcur: conformer_out(u_c, cv_w_dw[j], cv_b_dw[j], cv_ln_g[j], cv_ln_b[j], w2,
                                                   cv_b_pw2[j], xcur, mc)

        x = ffn(x, mods(i, 2, True), norm_g[i, 2], w_in1, w_out1, final_g=final_g if last else None)
        if ctx_advance:
            xc = yc_fn(xc)
            xc = ffn(xc, mods(i, 2, False), norm_g[i, 2], w_in1, w_out1)
    return x
```

```python
import functools
import math

import numpy as np
import jax
import jax.numpy as jnp
from jax import lax
from jax.experimental import pallas as pl
from jax.experimental.pallas import tpu as pltpu

F32 = jnp.float32
BF16 = jnp.bfloat16

GRID_W = 64
ROPE_THETA = 10000.0
HY_FAST_DECAY = 0.3
HY_SLOW_DECAY = 1.5
HY_DECAY_TARGET = 1e-2
EPS = 1e-6
LN_EPS = 1e-5
N_MIXERS = 3
N_MOD = 9

LANES = 128
SUBLANES = 8
BF16_SUBLANES = 16
VMEM_LIMIT_BYTES = 56 * 1024 * 1024

MOD_ROWS = SUBLANES
HIGHEST = lax.Precision.HIGHEST


def _cparams(*sem):
    return pltpu.CompilerParams(dimension_semantics=sem, vmem_limit_bytes=VMEM_LIMIT_BYTES)


def _row_tile(n, want):
    t = min(n, want)
    assert n % t == 0, (n, t)
    return t


def _dot(a, b):
    return jnp.dot(a, b, preferred_element_type=F32)


def _dot32(a, b):
    return jnp.dot(a, b, preferred_element_type=F32, precision=HIGHEST)


def _rmsnorm(x, g):
    return x * lax.rsqrt(jnp.mean(x * x, axis=-1, keepdims=True) + EPS) * g


def _modulate(x, g, shift, scale):
    return _rmsnorm(x, g) * (1.0 + scale) + shift


def _silu(x):
    return x * jax.nn.sigmoid(x)


def _mod_kernel(r_ref, w_ref, b_ref, o_ref):
    r = _silu(r_ref[...]).astype(BF16)
    o_ref[0] = _dot(r, w_ref[0].astype(BF16)) + b_ref[0]


def mod_table(rows, w_mod, b_mod):
    depth, d, nm = w_mod.shape
    tn = nm // N_MOD
    return pl.pallas_call(
        _mod_kernel,
        grid=(depth, nm // tn),
        in_specs=[pl.BlockSpec((MOD_ROWS, d), lambda i, j: (0, 0)),
                  pl.BlockSpec((1, d, tn), lambda i, j: (i, 0, j)),
                  pl.BlockSpec((1, 1, tn), lambda i, j: (i, 0, j))],
        out_specs=pl.BlockSpec((1, MOD_ROWS, tn), lambda i, j: (i, 0, j)),
        out_shape=jax.ShapeDtypeStruct((depth, MOD_ROWS, nm), F32),
        compiler_params=_cparams("parallel", "parallel"),
        name="mod_table",
    )(rows, w_mod, b_mod.reshape(depth, 1, nm))


FFN_CHUNKS = 2


def _ffn_kernel(x_ref, mod_ref, g_ref, wg_ref, wu_ref, wo_ref, *rest, final):
    if final:
        fg_ref, o_ref = rest
    else:
        (o_ref,) = rest
    x = x_ref[0]
    h = _modulate(x, g_ref[...], mod_ref[0, 0:1, :], mod_ref[0, 1:2, :]).astype(BF16)
    tf = wo_ref.shape[0] // FFN_CHUNKS
    acc = None
    for c in range(FFN_CHUNKS):
        gate = _dot(h, wg_ref[:, c * tf:(c + 1) * tf])
        up = _dot(h, wu_ref[:, c * tf:(c + 1) * tf])
        act = (_silu(gate) * up).astype(BF16)
        part = _dot(act, wo_ref[c * tf:(c + 1) * tf, :])
        acc = part if acc is None else acc + part
    xn = x + 0.5 * mod_ref[0, 2:3, :] * acc
    if final:
        xn = _rmsnorm(xn, fg_ref[...])
    o_ref[0] = xn


def ffn(x, mod, g, w_in, w_out, final_g=None, tm=1024):
    b, n, d = x.shape
    ff = w_out.shape[0]
    tm = _row_tile(n, tm)
    assert (ff // FFN_CHUNKS) % LANES == 0
    final = final_g is not None
    resident = dict(pipeline_mode=pl.Buffered(1))
    in_specs = [pl.BlockSpec((1, tm, d), lambda bi, i: (bi, i, 0)),
                pl.BlockSpec((1, MOD_ROWS, d), lambda bi, i: (bi, 0, 0)),
                pl.BlockSpec((1, d), lambda bi, i: (0, 0)),
                pl.BlockSpec((d, ff), lambda bi, i: (0, 0), **resident),
                pl.BlockSpec((d, ff), lambda bi, i: (0, 1), **resident),
                pl.BlockSpec((ff, d), lambda bi, i: (0, 0), **resident)]
    args = [x, mod, g.reshape(1, d), w_in, w_in, w_out]
    if final:
        in_specs.append(pl.BlockSpec((1, d), lambda bi, i: (0, 0)))
        args.append(final_g.reshape(1, d))
    return pl.pallas_call(
        functools.partial(_ffn_kernel, final=final),
        grid=(b, n // tm),
        in_specs=in_specs,
        out_specs=pl.BlockSpec((1, tm, d), lambda bi, i: (bi, i, 0)),
        out_shape=jax.ShapeDtypeStruct((b, n, d), F32),
        compiler_params=_cparams("parallel", "parallel"),
        name="ffn",
    )(*args)


def _rope_cols(y, cos, sa, sb):
    outs = []
    for c in range(y.shape[1] // LANES):
        yc = y[:, c * LANES:(c + 1) * LANES]
        outs.append(yc * cos + pltpu.roll(yc, LANES - 16, 1) * sa + pltpu.roll(yc, 16, 1) * sb)
    return jnp.concatenate(outs, axis=1)


def _modproj_kernel(x_ref, mod_ref, g_ref, w_ref, *rest, rope_blocks):
    if rope_blocks:
        cos_ref, sa_ref, sb_ref, o_ref, h_scr = rest
    else:
        o_ref, h_scr = rest
    j = pl.program_id(2)

    @pl.when(j == 0)
    def _():
        h = _modulate(x_ref[0], g_ref[...], mod_ref[0, 0:1, :], mod_ref[0, 1:2, :])
        h_scr[...] = h.astype(BF16)

    y = _dot(h_scr[...], w_ref[...])
    if rope_blocks:
        @pl.when(j < rope_blocks)
        def _():
            o_ref[0] = _rope_cols(y, cos_ref[...], sa_ref[...], sb_ref[...]).astype(o_ref.dtype)

        @pl.when(j >= rope_blocks)
        def _():
            o_ref[0] = y.astype(o_ref.dtype)
    else:
        o_ref[0] = y.astype(o_ref.dtype)


def modproj(x, mod, g, w, rope=None, rope_blocks=0, tm=512, tn=1024):
    b, n, d = x.shape
    nout = w.shape[1]
    tm = _row_tile(n, tm)
    in_specs = [pl.BlockSpec((1, tm, d), lambda bi, i, j: (bi, i, 0)),
                pl.BlockSpec((1, MOD_ROWS, d), lambda bi, i, j: (bi, 0, 0)),
                pl.BlockSpec((1, d), lambda bi, i, j: (0, 0)),
                pl.BlockSpec((d, tn), lambda bi, i, j: (0, j))]
    args = [x, mod, g.reshape(1, d), w]
    if rope_blocks:
        in_specs += [pl.BlockSpec((tm, LANES), lambda bi, i, j: (i, 0))] * 3
        args += list(rope)
    return pl.pallas_call(
        functools.partial(_modproj_kernel, rope_blocks=rope_blocks),
        grid=(b, n // tm, nout // tn),
        in_specs=in_specs,
        out_specs=pl.BlockSpec((1, tm, tn), lambda bi, i, j: (bi, i, j)),
        out_shape=jax.ShapeDtypeStruct((b, n, nout), BF16),
        scratch_shapes=[pltpu.VMEM((tm, d), BF16)],
        compiler_params=_cparams("parallel", "parallel", "arbitrary"),
        name="modproj",
    )(*args)


def rope_tables(n):
    rows = n // GRID_W
    row = jnp.repeat(jnp.arange(rows), GRID_W).astype(F32)
    col = jnp.tile(jnp.arange(GRID_W), rows).astype(F32)
    quarter = 16
    half = 32
    inv = ROPE_THETA ** (-(2.0 * jnp.arange(quarter, dtype=F32)) / half)
    ang_r = row[:, None] * inv
    ang_c = col[:, None] * inv
    zero = jnp.zeros_like(ang_r)
    cr, sr, cc, sc = jnp.cos(ang_r), jnp.sin(ang_r), jnp.cos(ang_c), jnp.sin(ang_c)
    cos64 = jnp.concatenate([cr, cr, cc, cc], axis=1)
    sa64 = jnp.concatenate([-sr, zero, -sc, zero], axis=1)
    sb64 = jnp.concatenate([zero, sr, zero, sc], axis=1)
    tile2 = lambda t: jnp.concatenate([t, t], axis=1)
    return tile2(cos64), tile2(sa64), tile2(sb64)


V_ROWS = LANES + BF16_SUBLANES
PIPE_UNROLL = 8


def _attn_kernel(q_ref, k_ref, v_ref, lam_ref, g_ref, o_ref, vt_scr, qm_scr, sa_scr, sb_scr, ma_scr, mb_scr, m_scr,
                 acc_scr, *, nkt, tk, nq, tq, lam_init, half):
    @pl.when(pl.program_id(2) == 0)
    def _():
        extra = lax.broadcasted_iota(jnp.int32, (V_ROWS - LANES, tk), 0)
        for t in range(nkt):
            vt_scr[t, 0:LANES, :] = v_ref[0, t * tk:(t + 1) * tk, :].astype(F32).T.astype(BF16)
            vt_scr[t, LANES:V_ROWS, :] = jnp.where(extra == 0, 1.0, 0.0).astype(BF16)

    lane = lax.broadcasted_iota(jnp.int32, (tq, LANES), 1)
    for qi in range(nq):
        q = q_ref[0, qi * tq:(qi + 1) * tq, :].astype(F32) * (half ** -0.5 * math.log2(math.e))
        qm_scr[qi, 0] = jnp.where(lane < half, q, 0.0).astype(BF16)
        qm_scr[qi, 1] = jnp.where(lane >= half, q, 0.0).astype(BF16)
    m_scr[...] = jnp.full(m_scr.shape, -jnp.inf, F32)
    acc_scr[...] = jnp.zeros_like(acc_scr)

    def scores(u, bufs):
        s_ref, mt_ref = bufs
        qi, t = u // nkt, u % nkt
        k = k_ref[0, pl.ds(pl.multiple_of(t * tk, tk), tk), :]
        for mi in range(2):
            s = lax.dot_general(k, qm_scr[qi, mi], (((1,), (1,)), ((), ())),
                                preferred_element_type=F32)
            s_ref[mi] = s
            mt_ref[mi] = jnp.max(s, axis=0, keepdims=True)

    def softmax_pv(u, bufs):
        s_ref, mt_ref = bufs
        qi, t = u // nkt, u % nkt
        vt = vt_scr[t]
        for mi in range(2):
            s = s_ref[mi]
            m_old = m_scr[qi, mi]
            m_new = jnp.maximum(m_old, mt_ref[mi])
            alpha = jnp.exp2(m_old - m_new)
            p = jnp.exp2(s - m_new).astype(BF16)
            acc_scr[qi, mi] = alpha * acc_scr[qi, mi] + _dot(vt, p)
            m_scr[qi, mi] = m_new

    units = nq * nkt
    bufs = ((sa_scr, ma_scr), (sb_scr, mb_scr))
    scores(0, bufs[0])

    def steps(first, count):
        for i in range(count):
            scores(first + i + 1, bufs[(i + 1) % 2])
            softmax_pv(first + i, bufs[i % 2])

    n_steps = units - 1
    n_loop = n_steps // PIPE_UNROLL

    def body(j, carry):
        steps(PIPE_UNROLL * j, PIPE_UNROLL)
        return carry

    lax.fori_loop(0, n_loop, body, 0)
    steps(n_loop * PIPE_UNROLL, n_steps - n_loop * PIPE_UNROLL)
    softmax_pv(units - 1, bufs[(units - 1) % 2])

    lv = lam_ref[...]
    lam = (jnp.exp(jnp.sum(lv[0:1] * lv[1:2], axis=1, keepdims=True))
           - jnp.exp(jnp.sum(lv[2:3] * lv[3:4], axis=1, keepdims=True)) + lam_init)
    for qi in range(nq):
        a1 = acc_scr[qi, 0]
        a2 = acc_scr[qi, 1]
        ot = a1[:LANES] / a1[LANES:LANES + 1] - lam * (a2[:LANES] / a2[LANES:LANES + 1])
        o = _rmsnorm(ot.T, g_ref[...]) * (1.0 - lam_init)
        o_ref[0, qi * tq:(qi + 1) * tq, :] = o.astype(o_ref.dtype)


def diff_attention(q_src, kv_src, n_q, lam_vecs, subln_g, lam_init, tk, tq=256, nq=4):
    b, nk, c3 = kv_src.shape
    d = c3 // 3
    h = d // LANES
    assert nk % tk == 0
    nkt = nk // tk
    tq = _row_tile(n_q, tq)
    nq = min(nq, n_q // tq)
    tb = nq * tq
    assert n_q % tb == 0
    return pl.pallas_call(
        functools.partial(_attn_kernel, nkt=nkt, tk=tk, nq=nq, tq=tq, lam_init=lam_init, half=LANES // 2),
        grid=(b, h, n_q // tb),
        in_specs=[pl.BlockSpec((1, tb, LANES), lambda bi, hi, i: (bi, i, hi)),
                  pl.BlockSpec((1, nk, LANES), lambda bi, hi, i: (bi, 0, h + hi)),
                  pl.BlockSpec((1, nk, LANES), lambda bi, hi, i: (bi, 0, 2 * h + hi)),
                  pl.BlockSpec((4, LANES // 2), lambda bi, hi, i: (0, 0)),
                  pl.BlockSpec((1, LANES), lambda bi, hi, i: (0, 0))],
        out_specs=pl.BlockSpec((1, tb, LANES), lambda bi, hi, i: (bi, i, hi)),
        out_shape=jax.ShapeDtypeStruct((b, n_q, d), BF16),
        scratch_shapes=[pltpu.VMEM((nkt, V_ROWS, tk), BF16), pltpu.VMEM((nq, 2, tq, LANES), BF16),
                        pltpu.VMEM((2, tk, tq), F32), pltpu.VMEM((2, tk, tq), F32),
                        pltpu.VMEM((2, 1, tq), F32), pltpu.VMEM((2, 1, tq), F32),
                        pltpu.VMEM((nq, 2, 1, tq), F32), pltpu.VMEM((nq, 2, V_ROWS, tq), F32)],
        compiler_params=_cparams("parallel", "parallel", "arbitrary"),
        name="diff_attention",
    )(q_src, kv_src, kv_src, lam_vecs, subln_g.reshape(1, LANES))


def _resid_out(x_ref, mod_ref, y, o_ref):
    o_ref[0] = x_ref[0] + mod_ref[0, 2:3, :] * y


def _attn_out_kernel(a_ref, w_ref, x_ref, mod_ref, o_ref):
    _resid_out(x_ref, mod_ref, _dot(a_ref[0], w_ref[...]), o_ref)


def attn_out(a, w, x, mod, tm=512):
    b, n, d = x.shape
    tm = _row_tile(n, tm)
    row = pl.BlockSpec((1, tm, d), lambda bi, i: (bi, i, 0))
    return pl.pallas_call(
        _attn_out_kernel,
        grid=(b, n // tm),
        in_specs=[row, pl.BlockSpec((d, d), lambda bi, i: (0, 0)), row,
                  pl.BlockSpec((1, MOD_ROWS, d), lambda bi, i: (bi, 0, 0))],
        out_specs=row,
        out_shape=jax.ShapeDtypeStruct((b, n, d), F32),
        compiler_params=_cparams("parallel", "parallel"),
        name="attn_out",
    )(a, w, x, mod)


def _hyena_out_kernel(y_ref, vg_ref, x0_ref, skip_ref, w_ref, b_ref, x_ref, mod_ref, o_ref):
    a = ((y_ref[0] + vg_ref[0] * skip_ref[...]) * x0_ref[0]).astype(BF16)
    _resid_out(x_ref, mod_ref, _dot(a, w_ref[...]) + b_ref[...], o_ref)


def hyena_out(y, vg, x0, skip, w, bias, x, mod, tm=512):
    b, n, d = x.shape
    tm = _row_tile(n, tm)
    row = pl.BlockSpec((1, tm, d), lambda bi, i: (bi, i, 0))
    vec = pl.BlockSpec((1, d), lambda bi, i: (0, 0))
    return pl.pallas_call(
        _hyena_out_kernel,
        grid=(b, n // tm),
        in_specs=[row, row, row, vec, pl.BlockSpec((d, d), lambda bi, i: (0, 0)), vec, row,
                  pl.BlockSpec((1, MOD_ROWS, d), lambda bi, i: (bi, 0, 0))],
        out_specs=row,
        out_shape=jax.ShapeDtypeStruct((b, n, d), F32),
        compiler_params=_cparams("parallel", "parallel"),
        name="hyena_out",
    )(y, vg, x0, skip.reshape(1, d), w, bias.reshape(1, d), x, mod)


def _fill_window(ext_ref, prev_ref, cur_ref, next_ref, halo, tm, i, last):
    ext_ref[0:halo, :] = jnp.where(i == 0, 0.0, prev_ref[0])
    ext_ref[halo:halo + tm, :] = cur_ref[0]
    ext_ref[halo + tm:halo + tm + halo, :] = jnp.where(i == last, 0.0, next_ref[0])


def _dwconv(ext_ref, w_ref, halo, tm, z_scr):
    width = w_ref.shape[0]
    pad = (width - 1) // 2
    offs = [halo - pad + j for j in range(width)]
    acc = None
    rows = z_scr.shape[0]
    for r in sorted({o % SUBLANES for o in offs}):
        z_scr[...] = ext_ref[pl.ds(r, rows), :]
        for j, o in enumerate(offs):
            if o % SUBLANES == r:
                term = w_ref[j:j + 1, :] * z_scr[pl.ds(o - r, tm), :]
                acc = term if acc is None else acc + term
    return acc


def _halo_specs(tm, halo, n, c):
    per = tm // halo
    nblk = n // halo
    prev = pl.BlockSpec((1, halo, c), lambda bi, i: (bi, jnp.maximum(i * per - 1, 0), 0))
    cur = pl.BlockSpec((1, tm, c), lambda bi, i: (bi, i, 0))
    nxt = pl.BlockSpec((1, halo, c), lambda bi, i: (bi, jnp.minimum((i + 1) * per, nblk - 1), 0))
    return [prev, cur, nxt]


def _hyena_in_kernel(prev_ref, cur_ref, next_ref, mod_ref, g_ref, w_ref, bin_ref, ws_ref, bs_ref, x0_ref, vg_ref,
                     xe_scr, ext_scr, *, halo, tm, d, last):
    i = pl.program_id(1)
    xe_scr[0:halo, :] = prev_ref[0]
    xe_scr[halo:halo + tm, :] = cur_ref[0]
    xe_scr[halo + tm:halo + tm + halo, :] = next_ref[0]
    h = _modulate(xe_scr[...], g_ref[...], mod_ref[0, 0:1, :], mod_ref[0, 1:2, :]).astype(BF16)
    row = lax.broadcasted_iota(jnp.int32, (tm + 2 * halo, 1), 0)
    inside = ((row >= halo) | (i > 0)) & ((row < halo + tm) | (i < last))
    pad = (ws_ref.shape[0] - 1) // 2
    parts = []
    for c in range(3):
        cols = slice(c * d, (c + 1) * d)
        u = _dot(h, w_ref[:, cols]) + bin_ref[:, cols]
        ext_scr[...] = jnp.where(inside, u, 0.0)
        acc = bs_ref[:, cols]
        for j in range(ws_ref.shape[0]):
            acc = acc + ws_ref[j:j + 1, cols] * ext_scr[pl.ds(halo - pad + j, tm), :]
        parts.append(acc)
    x0_ref[0] = parts[0]
    vg_ref[0] = parts[2] * parts[1]


def hyena_in(x, mod, g, w_in, b_in, w_short, b_short, tm=512):
    b, n, d = x.shape
    c = w_in.shape[1]
    tm = _row_tile(n, tm)
    halo = SUBLANES
    row = pl.BlockSpec((1, tm, d), lambda bi, i: (bi, i, 0))
    vec = pl.BlockSpec((1, c), lambda bi, i: (0, 0))
    out = jax.ShapeDtypeStruct((b, n, d), F32)
    return pl.pallas_call(
        functools.partial(_hyena_in_kernel, halo=halo, tm=tm, d=d, last=n // tm - 1),
        grid=(b, n // tm),
        in_specs=_halo_specs(tm, halo, n, d) + [pl.BlockSpec((1, MOD_ROWS, d), lambda bi, i: (bi, 0, 0)),
                                                pl.BlockSpec((1, d), lambda bi, i: (0, 0)),
                                                pl.BlockSpec((d, c), lambda bi, i: (0, 0),
                                                             pipeline_mode=pl.Buffered(1)),
                                                vec, pl.BlockSpec(w_short.shape, lambda bi, i: (0, 0)), vec],
        out_specs=[row, row],
        out_shape=[out, out],
        scratch_shapes=[pltpu.VMEM((tm + 2 * halo, d), F32), pltpu.VMEM((tm + 2 * halo, d), F32)],
        compiler_params=_cparams("parallel", "parallel"),
        name="hyena_in",
    )(x, x, x, mod, g.reshape(1, d), w_in, b_in.reshape(1, c), w_short, b_short.reshape(1, c))


def _conformer_out_kernel(prev_ref, cur_ref, next_ref, wdw_ref, bdw_ref, lg_ref, lb_ref, w_ref, b_ref,
                          x_ref, mod_ref, o_ref, ext_ref, z_scr, *, halo, tm, d, last):
    i = pl.program_id(1)
    _fill_window(ext_ref, prev_ref, cur_ref, next_ref, halo, tm, i, last)
    u = _dwconv(ext_ref, wdw_ref, halo, tm, z_scr) + bdw_ref[...]
    mu = jnp.mean(u, axis=-1, keepdims=True)
    uc = u - mu
    var = jnp.mean(uc * uc, axis=-1, keepdims=True)
    z = _silu(uc * lax.rsqrt(var + LN_EPS) * lg_ref[...] + lb_ref[...]).astype(BF16)
    _resid_out(x_ref, mod_ref, _dot(z, w_ref[...]) + b_ref[...], o_ref)


def conformer_out(u, w_dw, b_dw, ln_g, ln_b, w, bias, x, mod, tm=256):
    b, n, d = x.shape
    tm = _row_tile(n, tm)
    halo = 2 * SUBLANES
    assert (w_dw.shape[0] - 1) // 2 <= halo
    row = pl.BlockSpec((1, tm, d), lambda bi, i: (bi, i, 0))
    vec = pl.BlockSpec((1, d), lambda bi, i: (0, 0))
    return pl.pallas_call(
        functools.partial(_conformer_out_kernel, halo=halo, tm=tm, d=d, last=n // tm - 1),
        grid=(b, n // tm),
        in_specs=_halo_specs(tm, halo, n, d) + [pl.BlockSpec(w_dw.shape, lambda bi, i: (0, 0)), vec, vec, vec,
                                                pl.BlockSpec((d, d), lambda bi, i: (0, 0)), vec, row,
                                                pl.BlockSpec((1, MOD_ROWS, d), lambda bi, i: (bi, 0, 0))],
        out_specs=row,
        out_shape=jax.ShapeDtypeStruct((b, n, d), F32),
        scratch_shapes=[pltpu.VMEM((tm + 2 * halo, d), F32), pltpu.VMEM((tm + 2 * halo - SUBLANES, d), F32)],
        compiler_params=_cparams("parallel", "parallel"),
        name="conformer_out",
    )(u, u, u, w_dw, b_dw.reshape(1, d), ln_g.reshape(1, d), ln_b.reshape(1, d), w, bias.reshape(1, d), x, mod)


def _modglu_kernel(x_ref, mod_ref, g_ref, wa_ref, wg_ref, ba_ref, bg_ref, o_ref):
    h = _modulate(x_ref[0], g_ref[...], mod_ref[0, 0:1, :], mod_ref[0, 1:2, :]).astype(BF16)
    a = _dot(h, wa_ref[...]) + ba_ref[...]
    gt = _dot(h, wg_ref[...]) + bg_ref[...]
    o_ref[0] = a * jax.nn.sigmoid(gt)


def modglu(x, mod, g, w, bias, tm=512):
    b, n, d = x.shape
    tm = _row_tile(n, tm)
    row = pl.BlockSpec((1, tm, d), lambda bi, i: (bi, i, 0))
    bias2 = bias.reshape(1, 2 * d)
    return pl.pallas_call(
        _modglu_kernel,
        grid=(b, n // tm),
        in_specs=[row, pl.BlockSpec((1, MOD_ROWS, d), lambda bi, i: (bi, 0, 0)),
                  pl.BlockSpec((1, d), lambda bi, i: (0, 0)),
                  pl.BlockSpec((d, d), lambda bi, i: (0, 0)), pl.BlockSpec((d, d), lambda bi, i: (0, 1)),
                  pl.BlockSpec((1, d), lambda bi, i: (0, 0)), pl.BlockSpec((1, d), lambda bi, i: (0, 1))],
        out_specs=row,
        out_shape=jax.ShapeDtypeStruct((b, n, d), F32),
        compiler_params=_cparams("parallel", "parallel"),
        name="modglu",
    )(x, mod, g.reshape(1, d), w, w, bias2, bias2)


def _filter_kernel(emb_ref, t_ref, w1_ref, b1_ref, w2_ref, b2_ref, w3_ref, b3_ref, fr_ref, w4_ref, dl_ref, o_ref):
    fr = fr_ref[...]
    hdn = jnp.sin(fr * (_dot32(emb_ref[...], w1_ref[...]) + b1_ref[...]))
    hdn = jnp.sin(fr * (_dot32(hdn, w2_ref[...]) + b2_ref[...]))
    hdn = jnp.sin(fr * (_dot32(hdn, w3_ref[...]) + b3_ref[...]))
    h = _dot32(hdn, w4_ref[...]) * jnp.exp(-t_ref[...] * dl_ref[...])
    row = lax.broadcasted_iota(jnp.int32, h.shape, 0)
    col = lax.broadcasted_iota(jnp.int32, h.shape, 1)
    drop = (row == 0) & (pl.program_id(0) == 0) & (col >= h.shape[1] // 2)
    o_ref[...] = jnp.where(drop, 0.0, h)


def hyena_filter(L, w1, b1, w2, b2, w3, b3, freq, w4, d_model, tl=256):
    bands = (w1.shape[0] - 1) // 2
    hid = w1.shape[1]
    t = jnp.linspace(0.0, 1.0, L, dtype=F32)[:, None]
    wpos = (2.0 * math.pi / L) * jnp.arange(L, dtype=F32)
    bnd = jnp.linspace(1e-4, bands - 1, bands, dtype=F32)
    fw = wpos[:, None] * bnd[None, :]
    emb = jnp.concatenate([t, jnp.cos(fw), -jnp.sin(fw)], axis=-1)
    max_decay = math.log(HY_DECAY_TARGET) / HY_FAST_DECAY
    min_decay = math.log(HY_DECAY_TARGET) / HY_SLOW_DECAY
    deltas = jnp.abs(jnp.linspace(min_decay, max_decay, d_model, dtype=F32))
    dl2 = jnp.concatenate([deltas, deltas])[None, :]
    tl = _row_tile(L, tl)
    td = 2 * d_model
    full = lambda a: pl.BlockSpec(a.shape, lambda i, j: (0, 0))
    vec = lambda a: a.reshape(1, -1)
    args = [emb, t, w1, vec(b1), w2, vec(b2), w3, vec(b3), vec(freq)]
    return pl.pallas_call(
        _filter_kernel,
        grid=(L // tl, 2 * d_model // td),
        in_specs=[pl.BlockSpec((tl, emb.shape[1]), lambda i, j: (i, 0)), pl.BlockSpec((tl, 1), lambda i, j: (i, 0))]
                 + [full(a) for a in args[2:]]
                 + [pl.BlockSpec((hid, td), lambda i, j: (0, j)), pl.BlockSpec((1, td), lambda i, j: (0, j))],
        out_specs=pl.BlockSpec((tl, td), lambda i, j: (i, j)),
        out_shape=jax.ShapeDtypeStruct((L, 2 * d_model), F32),
        compiler_params=_cparams("parallel", "parallel"),
        name="hyena_filter",
    )(*args, w4, dl2)


FFT_N2 = 128
DFT_BLOCKS_PER_STEP = 2
DFT_K1_PER_STEP = 4


def _split_bf16(a):
    hi = a.astype(BF16)
    return hi, (a - hi.astype(F32)).astype(BF16)


def _dotp_split(ah, al, b):
    bh, bl = _split_bf16(b)
    return _dot(ah, bh) + (_dot(ah, bl) + _dot(al, bh))


def _dotp(a, b):
    return _dotp_split(*_split_bf16(a), b)


def _to_block_major(x, n2):
    bx, L, c = x.shape
    r = L // n2
    return x.reshape(bx, r, n2 // SUBLANES, SUBLANES, c).transpose(0, 2, 1, 3, 4).reshape(
        bx, n2 // SUBLANES, r * SUBLANES, c)


def _from_block_major(y, n2):
    bx, na, rows8, c = y.shape
    r = rows8 // SUBLANES
    return y.reshape(bx, na, r, SUBLANES, c).transpose(0, 2, 1, 3, 4).reshape(bx, r * n2, c)


def _major_kernel(m_ref, x_ref, o_ref):
    mh, ml = _split_bf16(m_ref[...])
    rows_out, rows_in = m_ref.shape
    for a in range(x_ref.shape[1]):
        x2 = x_ref.at[0, a]
        o2 = o_ref.at[0, a]
        for s in range(SUBLANES):
            y = _dotp_split(mh, ml, x2[pl.ds(s, rows_in, stride=SUBLANES), :])
            o2[pl.ds(s, rows_out, stride=SUBLANES), :] = y


def dft_major(mat, x):
    bx, na, rows8, c = x.shape
    m, r = mat.shape
    assert rows8 == r * SUBLANES
    tc = LANES
    ka = DFT_BLOCKS_PER_STEP if na % DFT_BLOCKS_PER_STEP == 0 else 1
    return pl.pallas_call(
        _major_kernel,
        grid=(bx, na // ka, c // tc),
        in_specs=[pl.BlockSpec((m, r), lambda bi, a, j: (0, 0)),
                  pl.BlockSpec((1, ka, rows8, tc), lambda bi, a, j: (bi, a, 0, j))],
        out_specs=pl.BlockSpec((1, ka, m * SUBLANES, tc), lambda bi, a, j: (bi, a, 0, j)),
        out_shape=jax.ShapeDtypeStruct((bx, na, m * SUBLANES, c), F32),
        compiler_params=_cparams("parallel", "parallel", "parallel"),
        name="dft_major",
    )(mat, x)


def _twiddled_block(f_ref, tw_ref, q):
    fr, fi = f_ref[0], f_ref[1]
    twr, twi = tw_ref[q, 0:1, :], tw_ref[q, 1:2, :]
    gr = fr * twr - fi * twi
    gi = fr * twi + fi * twr
    return jnp.concatenate([jnp.concatenate([gr, -gi], axis=1), jnp.concatenate([gi, gr], axis=1)], axis=0)


def _gather_k1(ref, q, n2):
    return jnp.concatenate([ref[0, :, p, q].reshape(n2, -1) for p in range(2)], axis=0)


def _scatter_k1(ref, q, x, n2):
    for p in range(2):
        ref[0, :, p, q] = x[p * n2:(p + 1) * n2].reshape(n2 // SUBLANES, SUBLANES, -1)


def _spec_taps_kernel(f_ref, tw_ref, af_ref, ab_ref, o_ref):
    n2 = f_ref.shape[1]
    for q in range(tw_ref.shape[0]):
        g = _twiddled_block(f_ref, tw_ref, q)
        xf = _dotp(g, _gather_k1(af_ref, q, n2))
        xb = _dotp(g, _gather_k1(ab_ref, q, n2))
        _scatter_k1(o_ref, q, jnp.concatenate([xf[:n2] + xb[:n2], xf[n2:] - xb[n2:]], axis=0), n2)


def _spec_mid_kernel(f_ref, tw_ref, a_ref, h_ref, o_ref):
    n2 = f_ref.shape[1]
    for q in range(tw_ref.shape[0]):
        g = _twiddled_block(f_ref, tw_ref, q)
        x = _dotp(g, _gather_k1(a_ref, q, n2))
        h = _gather_k1(h_ref, q, n2)
        xr, xi = x[:n2], x[n2:]
        hr, hi = h[:n2], h[n2:]
        z = jnp.concatenate([xr * hr - xi * hi, xr * hi + xi * hr], axis=0)
        _scatter_k1(o_ref, q, _dotp(g.T, z), n2)


def spec_stage(fmat, tw, a, h=None, tc=1024):
    bx, na, rows8, ca = a.shape
    n2 = na * SUBLANES
    n1 = rows8 // (2 * SUBLANES)
    c = ca // 2 if h is None else ca
    tc = _row_tile(c, tc)
    nj = c // tc
    kk = DFT_K1_PER_STEP if n1 % DFT_K1_PER_STEP == 0 else 1
    six = lambda t: t.reshape(t.shape[0], na, 2, n1, SUBLANES, t.shape[-1])
    blk = lambda off: pl.BlockSpec((1, na, 2, kk, SUBLANES, tc), lambda k, j, bi: (bi, 0, 0, k, 0, off + j))
    in_specs = [pl.BlockSpec((2, n2, n2), lambda k, j, bi: (0, 0, 0)),
                pl.BlockSpec((kk, 2, n2), lambda k, j, bi: (k, 0, 0)), blk(0)]
    if h is None:
        kern, args = _spec_taps_kernel, [fmat, tw, six(a), six(a)]
        in_specs.append(blk(nj))
    else:
        kern, args = _spec_mid_kernel, [fmat, tw, six(a), six(h)]
        in_specs.append(pl.BlockSpec((1, na, 2, kk, SUBLANES, tc), lambda k, j, bi: (0, 0, 0, k, 0, j)))
    out = pl.pallas_call(
        kern,
        grid=(n1 // kk, nj, bx),
        in_specs=in_specs,
        out_specs=blk(0),
        out_shape=jax.ShapeDtypeStruct((bx, na, 2, n1, SUBLANES, c), F32),
        compiler_params=_cparams("parallel", "parallel", "arbitrary"),
        name="dft_minor",
    )(*args)
    return out.reshape(bx, na, rows8, c)


def _dft_constants(n1, n2):
    n = n1 * n2
    k1 = np.arange(n1, dtype=np.float64)
    ang1 = -2.0 * np.pi * np.outer(k1, k1) / n1
    w1r, w1i = np.cos(ang1), np.sin(ang1)
    k2 = np.arange(n2, dtype=np.float64)
    ang2 = -2.0 * np.pi * np.outer(k2, k2) / n2
    fmat = np.stack([np.cos(ang2), np.sin(ang2)])
    angt = -2.0 * np.pi * np.outer(k1, k2) / n
    tw = np.stack([np.cos(angt), np.sin(angt)], axis=1)
    fwd_half = np.concatenate([w1r[:, :n1 // 2], w1i[:, :n1 // 2]], axis=0)
    inv_half = np.concatenate([w1r[:n1 // 2], w1i[:n1 // 2]], axis=1) / n
    f = lambda a: jnp.asarray(a, F32)
    return f(fwd_half), f(inv_half), f(fmat), f(tw)


def long_conv(v, taps):
    b, L, c = v.shape
    n2 = FFT_N2
    n1 = 2 * L // n2
    fwd_half, inv_half, fmat, tw = _dft_constants(n1, n2)
    hs = spec_stage(fmat, tw, dft_major(fwd_half, _to_block_major(taps[None], n2)))
    a = dft_major(fwd_half, _to_block_major(v, n2))
    y = dft_major(inv_half, spec_stage(fmat, tw, a, hs))
    return _from_block_major(y, n2)


def _dense_conv_kernel(f_ref, g_ref, v_ref, hf_ref, hb_ref, o_ref):
    n = f_ref.shape[0] // 2
    f = f_ref[...]
    x = _dotp(f, v_ref[0])
    hf = _dotp(f, hf_ref[...])
    hb = _dotp(f, hb_ref[...])
    xr, xi = x[:n], x[n:]
    hr = hf[:n] + hb[:n]
    hi = hf[n:] - hb[n:]
    z = jnp.concatenate([xr * hr - xi * hi, xr * hi + xi * hr], axis=0)
    o_ref[0] = _dotp(g_ref[...], z)


def dense_long_conv(v, taps, tc=512):
    b, L, c = v.shape
    n = 2 * L
    ang = 2.0 * np.pi * np.outer(np.arange(n, dtype=np.float64), np.arange(L, dtype=np.float64)) / n
    fmat = jnp.asarray(np.concatenate([np.cos(ang), -np.sin(ang)], axis=0), F32)
    gmat = jnp.asarray(np.concatenate([np.cos(ang.T), -np.sin(ang.T)], axis=1) / n, F32)
    tc = _row_tile(c, tc)
    nj = c // tc
    return pl.pallas_call(
        _dense_conv_kernel,
        grid=(b, nj),
        in_specs=[pl.BlockSpec((2 * n, L), lambda bi, j: (0, 0)), pl.BlockSpec((L, 2 * n), lambda bi, j: (0, 0)),
                  pl.BlockSpec((1, L, tc), lambda bi, j: (bi, 0, j)),
                  pl.BlockSpec((L, tc), lambda bi, j: (0, j)), pl.BlockSpec((L, tc), lambda bi, j: (0, nj + j))],
        out_specs=pl.BlockSpec((1, L, tc), lambda bi, j: (bi, 0, j)),
        out_shape=jax.ShapeDtypeStruct((b, L, c), F32),
        compiler_params=_cparams("parallel", "parallel"),
        name="dense_long_conv",
    )(fmat, gmat, v, taps, taps)


def _pad_mod(m3):
    return jnp.pad(m3, ((0, 0), (0, MOD_ROWS - 3), (0, 0)))


def kernel(x, c, ctx, c_ctx, w_mod, b_mod, norm_g, w_ffn_in, w_ffn_out, attn_w_qkv, attn_w_o, attn_lambda,
           attn_subln_g, hy_w_in, hy_b_in, hy_w_short, hy_b_short, hy_f_w1, hy_f_b1, hy_f_w2, hy_f_b2, hy_f_w3,
           hy_f_b3, hy_f_freq, hy_f_w4, hy_skip, hy_w_out, hy_b_out, cv_w_pw1, cv_b_pw1, cv_w_dw, cv_b_dw,
           cv_ln_g, cv_ln_b, cv_w_pw2, cv_b_pw2, final_g):
    bsz, n_lat, d = x.shape
    n_ctx = ctx.shape[1]
    depth = w_mod.shape[0]
    assert bsz + 1 <= MOD_ROWS

    rows = jnp.concatenate([c, c_ctx[None, :], jnp.zeros((MOD_ROWS - bsz - 1, d), F32)], axis=0)
    table = mod_table(rows, w_mod, b_mod).reshape(depth, MOD_ROWS, N_MOD, d)

    def mods(i, s, latent):
        r = table[i, :bsz] if latent else jnp.broadcast_to(table[i, bsz:bsz + 1], (bsz, N_MOD, d))
        return _pad_mod(r[:, 3 * s:3 * s + 3])

    bf = lambda w: w.astype(BF16)
    xc = ctx
    tk_lat = 1280 if (n_lat + n_ctx) % 1280 == 0 else n_ctx
    rope = rope_tables(n_lat)

    for i in range(depth):
        kind = i % N_MIXERS
        j = i // N_MIXERS
        last = i == depth - 1
        ctx_in_use = (not last) or kind == 0
        ctx_advance = not last
        w_in0, w_out0 = bf(w_ffn_in[i, 0]), bf(w_ffn_out[i, 0])
        w_in1, w_out1 = bf(w_ffn_in[i, 1]), bf(w_ffn_out[i, 1])

        x = ffn(x, mods(i, 0, True), norm_g[i, 0], w_in0, w_out0)
        if ctx_in_use:
            xc = ffn(xc, mods(i, 0, False), norm_g[i, 0], w_in0, w_out0)

        ml, mc = mods(i, 1, True), mods(i, 1, False)
        if kind == 0:
            lam_init = 0.8 - 0.6 * math.exp(-0.3 * i)
            wqkv = bf(attn_w_qkv[j])
            wo = bf(attn_w_o[j])
            qkv_l = modproj(x, ml, norm_g[i, 1], wqkv, rope=rope, rope_blocks=2)
            qkv_c = modproj(xc, mc, norm_g[i, 1], wqkv)
            qkv_all = jnp.concatenate([qkv_l, qkv_c], axis=1)
            o_l = diff_attention(qkv_all, qkv_all, n_lat, attn_lambda[j], attn_subln_g[j], lam_init, tk_lat)
            x = attn_out(o_l, wo, x, ml)
            if ctx_advance:
                o_c = diff_attention(qkv_c, qkv_c, n_ctx, attn_lambda[j], attn_subln_g[j], lam_init, n_ctx)
                yc_fn = lambda xcur: attn_out(o_c, wo, xcur, mc)
        elif kind == 1:
            w_in, w_o = bf(hy_w_in[j]), bf(hy_w_out[j])
            filt = (hy_f_w1[j], hy_f_b1[j], hy_f_w2[j], hy_f_b2[j], hy_f_w3[j], hy_f_b3[j], hy_f_freq[j], hy_f_w4[j])

            def hyena(xs, mod, L):
                x0, vg = hyena_in(xs, mod, norm_g[i, 1], w_in, hy_b_in[j], hy_w_short[j], hy_b_short[j])
                taps = hyena_filter(L, *filt, d)
                y = long_conv(vg, taps) if L == n_lat else dense_long_conv(vg, taps)
                return y, vg, x0

            y_l, vg_l, x0_l = hyena(x, ml, n_lat)
            x = hyena_out(y_l, vg_l, x0_l, hy_skip[j], w_o, hy_b_out[j], x, ml)
            if ctx_advance:
                y_c, vg_c, x0_c = hyena(xc, mc, n_ctx)
                yc_fn = lambda xcur: hyena_out(y_c, vg_c, x0_c, hy_skip[j], w_o, hy_b_out[j], xcur, mc)
        else:
            w1, w2 = bf(cv_w_pw1[j]), bf(cv_w_pw2[j])
            u_l = modglu(x, ml, norm_g[i, 1], w1, cv_b_pw1[j])
            x = conformer_out(u_l, cv_w_dw[j], cv_b_dw[j], cv_ln_g[j], cv_ln_b[j], w2, cv_b_pw2[j], x, ml)
            if ctx_advance:
                u_c = modglu(xc, mc, norm_g[i, 1], w1, cv_b_pw1[j])
                yc_fn = lambda xcur: conformer_out(u_c, cv_w_dw[j], cv_b_dw[j], cv_ln_g[j], cv_ln_b[j], w2,
                                                   cv_b_pw2[j], xcur, mc)

        x = ffn(x, mods(i, 2, True), norm_g[i, 2], w_in1, w_out1, final_g=final_g if last else None)
        if ctx_advance:
            xc = yc_fn(xc)
            xc = ffn(xc, mods(i, 2, False), norm_g[i, 2], w_in1, w_out1)
    return x
```

```python
import functools
import math

import numpy as np
import jax
import jax.numpy as jnp
from jax import lax
from jax.experimental import pallas as pl
from jax.experimental.pallas import tpu as pltpu

F32 = jnp.float32
BF16 = jnp.bfloat16

GRID_W = 64
ROPE_THETA = 10000.0
HY_FAST_DECAY = 0.3
HY_SLOW_DECAY = 1.5
HY_DECAY_TARGET = 1e-2
EPS = 1e-6
LN_EPS = 1e-5
N_MIXERS = 3
N_MOD = 9

LANES = 128
SUBLANES = 8
BF16_SUBLANES = 16
VMEM_LIMIT_BYTES = 56 * 1024 * 1024

MOD_ROWS = SUBLANES
HIGHEST = lax.Precision.HIGHEST


def _cparams(*sem):
    return pltpu.CompilerParams(dimension_semantics=sem, vmem_limit_bytes=VMEM_LIMIT_BYTES)


def _row_tile(n, want):
    t = min(n, want)
    assert n % t == 0, (n, t)
    return t


def _dot(a, b):
    return jnp.dot(a, b, preferred_element_type=F32)


def _dot32(a, b):
    return jnp.dot(a, b, preferred_element_type=F32, precision=HIGHEST)


def _rmsnorm(x, g):
    return x * lax.rsqrt(jnp.mean(x * x, axis=-1, keepdims=True) + EPS) * g


def _modulate(x, g, shift, scale):
    return _rmsnorm(x, g) * (1.0 + scale) + shift


def _silu(x):
    return x * jax.nn.sigmoid(x)


def _mod_kernel(r_ref, w_ref, b_ref, o_ref):
    r = _silu(r_ref[...]).astype(BF16)
    o_ref[0] = _dot(r, w_ref[0].astype(BF16)) + b_ref[0]


def mod_table(rows, w_mod, b_mod):
    depth, d, nm = w_mod.shape
    tn = nm // N_MOD
    return pl.pallas_call(
        _mod_kernel,
        grid=(depth, nm // tn),
        in_specs=[pl.BlockSpec((MOD_ROWS, d), lambda i, j: (0, 0)),
                  pl.BlockSpec((1, d, tn), lambda i, j: (i, 0, j)),
                  pl.BlockSpec((1, 1, tn), lambda i, j: (i, 0, j))],
        out_specs=pl.BlockSpec((1, MOD_ROWS, tn), lambda i, j: (i, 0, j)),
        out_shape=jax.ShapeDtypeStruct((depth, MOD_ROWS, nm), F32),
        compiler_params=_cparams("parallel", "parallel"),
        name="mod_table",
    )(rows, w_mod, b_mod.reshape(depth, 1, nm))


FFN_CHUNKS = 2


def _ffn_kernel(x_ref, mod_ref, g_ref, wg_ref, wu_ref, wo_ref, *rest, final):
    if final:
        fg_ref, o_ref = rest
    else:
        (o_ref,) = rest
    x = x_ref[0]
    h = _modulate(x, g_ref[...], mod_ref[0, 0:1, :], mod_ref[0, 1:2, :]).astype(BF16)
    tf = wo_ref.shape[0] // FFN_CHUNKS
    acc = None
    for c in range(FFN_CHUNKS):
        gate = _dot(h, wg_ref[:, c * tf:(c + 1) * tf])
        up = _dot(h, wu_ref[:, c * tf:(c + 1) * tf])
        act = (_silu(gate) * up).astype(BF16)
        part = _dot(act, wo_ref[c * tf:(c + 1) * tf, :])
        acc = part if acc is None else acc + part
    xn = x + 0.5 * mod_ref[0, 2:3, :] * acc
    if final:
        xn = _rmsnorm(xn, fg_ref[...])
    o_ref[0] = xn


def ffn(x, mod, g, w_in, w_out, final_g=None, tm=1024):
    b, n, d = x.shape
    ff = w_out.shape[0]
    tm = _row_tile(n, tm)
    assert (ff // FFN_CHUNKS) % LANES == 0
    final = final_g is not None
    resident = dict(pipeline_mode=pl.Buffered(1))
    in_specs = [pl.BlockSpec((1, tm, d), lambda bi, i: (bi, i, 0)),
                pl.BlockSpec((1, MOD_ROWS, d), lambda bi, i: (bi, 0, 0)),
                pl.BlockSpec((1, d), lambda bi, i: (0, 0)),
                pl.BlockSpec((d, ff), lambda bi, i: (0, 0), **resident),
                pl.BlockSpec((d, ff), lambda bi, i: (0, 1), **resident),
                pl.BlockSpec((ff, d), lambda bi, i: (0, 0), **resident)]
    args = [x, mod, g.reshape(1, d), w_in, w_in, w_out]
    if final:
        in_specs.append(pl.BlockSpec((1, d), lambda bi, i: (0, 0)))
        args.append(final_g.reshape(1, d))
    return pl.pallas_call(
        functools.partial(_ffn_kernel, final=final),
        grid=(b, n // tm),
        in_specs=in_specs,
        out_specs=pl.BlockSpec((1, tm, d), lambda bi, i: (bi, i, 0)),
        out_shape=jax.ShapeDtypeStruct((b, n, d), F32),
        compiler_params=_cparams("parallel", "parallel"),
        name="ffn",
    )(*args)


def _rope_cols(y, cos, sa, sb):
    outs = []
    for c in range(y.shape[1] // LANES):
        yc = y[:, c * LANES:(c + 1) * LANES]
        outs.append(yc * cos + pltpu.roll(yc, LANES - 16, 1) * sa + pltpu.roll(yc, 16, 1) * sb)
    return jnp.concatenate(outs, axis=1)


def _modproj_kernel(x_ref, mod_ref, g_ref, w_ref, *rest, rope_blocks, aliased):
    if rope_blocks:
        cos_ref, sa_ref, sb_ref = rest[:3]
        rest = rest[3:]
    if aliased:
        rest = rest[1:]
    o_ref, h_scr = rest
    j = pl.program_id(2)

    @pl.when(j == 0)
    def _():
        h = _modulate(x_ref[0], g_ref[...], mod_ref[0, 0:1, :], mod_ref[0, 1:2, :])
        h_scr[...] = h.astype(BF16)

    y = _dot(h_scr[...], w_ref[...])
    if rope_blocks:
        @pl.when(j < rope_blocks)
        def _():
            o_ref[0] = _rope_cols(y, cos_ref[...], sa_ref[...], sb_ref[...]).astype(o_ref.dtype)

        @pl.when(j >= rope_blocks)
        def _():
            o_ref[0] = y.astype(o_ref.dtype)
    else:
        o_ref[0] = y.astype(o_ref.dtype)


def modproj(x, mod, g, w, rope=None, rope_blocks=0, out_rows=None, into=None, row_offset=0, tm=512, tn=1024):
    b, n, d = x.shape
    nout = w.shape[1]
    tm = _row_tile(n, tm)
    out_rows = n if out_rows is None else out_rows
    assert row_offset % tm == 0 and row_offset + n <= out_rows
    off = row_offset // tm
    in_specs = [pl.BlockSpec((1, tm, d), lambda bi, i, j: (bi, i, 0)),
                pl.BlockSpec((1, MOD_ROWS, d), lambda bi, i, j: (bi, 0, 0)),
                pl.BlockSpec((1, d), lambda bi, i, j: (0, 0)),
                pl.BlockSpec((d, tn), lambda bi, i, j: (0, j))]
    args = [x, mod, g.reshape(1, d), w]
    if rope_blocks:
        in_specs += [pl.BlockSpec((tm, LANES), lambda bi, i, j: (i, 0))] * 3
        args += list(rope)
    aliases = {}
    if into is not None:
        assert into.shape == (b, out_rows, nout) and into.dtype == BF16
        in_specs.append(pl.BlockSpec(memory_space=pl.ANY))
        aliases = {len(args): 0}
        args.append(into)
    return pl.pallas_call(
        functools.partial(_modproj_kernel, rope_blocks=rope_blocks, aliased=into is not None),
        grid=(b, n // tm, nout // tn),
        in_specs=in_specs,
        out_specs=pl.BlockSpec((1, tm, tn), lambda bi, i, j: (bi, off + i, j)),
        out_shape=jax.ShapeDtypeStruct((b, out_rows, nout), BF16),
        scratch_shapes=[pltpu.VMEM((tm, d), BF16)],
        input_output_aliases=aliases,
        compiler_params=_cparams("parallel", "parallel", "arbitrary"),
        name="modproj",
    )(*args)


def rope_tables(n):
    f32 = np.float32
    rows = n // GRID_W
    row = np.repeat(np.arange(rows), GRID_W).astype(f32)
    col = np.tile(np.arange(GRID_W), rows).astype(f32)
    quarter = 16
    half = 32
    inv = (f32(ROPE_THETA) ** (-(f32(2.0) * np.arange(quarter, dtype=f32)) / f32(half))).astype(f32)
    ang_r = row[:, None] * inv
    ang_c = col[:, None] * inv
    zero = np.zeros_like(ang_r)
    cr, sr, cc, sc = np.cos(ang_r), np.sin(ang_r), np.cos(ang_c), np.sin(ang_c)
    cos64 = np.concatenate([cr, cr, cc, cc], axis=1)
    sa64 = np.concatenate([-sr, zero, -sc, zero], axis=1)
    sb64 = np.concatenate([zero, sr, zero, sc], axis=1)
    tile2 = lambda t: jnp.asarray(np.concatenate([t, t], axis=1), F32)
    return tile2(cos64), tile2(sa64), tile2(sb64)


V_ROWS = LANES + BF16_SUBLANES
PIPE_UNROLL = 8


def _attn_kernel(q_ref, k_ref, v_ref, lam_ref, g_ref, o_ref, vt_scr, qm_scr, sa_scr, sb_scr, ma_scr, mb_scr, m_scr,
                 acc_scr, *, nkt, tk, nq, tq, lam_init, half):
    @pl.when(pl.program_id(2) == 0)
    def _():
        extra = lax.broadcasted_iota(jnp.int32, (V_ROWS - LANES, tk), 0)
        for t in range(nkt):
            vt_scr[t, 0:LANES, :] = v_ref[0, t * tk:(t + 1) * tk, :].astype(F32).T.astype(BF16)
            vt_scr[t, LANES:V_ROWS, :] = jnp.where(extra == 0, 1.0, 0.0).astype(BF16)

    lane = lax.broadcasted_iota(jnp.int32, (tq, LANES), 1)
    for qi in range(nq):
        q = q_ref[0, qi * tq:(qi + 1) * tq, :].astype(F32) * (half ** -0.5 * math.log2(math.e))
        qm_scr[qi, 0] = jnp.where(lane < half, q, 0.0).astype(BF16)
        qm_scr[qi, 1] = jnp.where(lane >= half, q, 0.0).astype(BF16)
    m_scr[...] = jnp.full(m_scr.shape, -jnp.inf, F32)
    acc_scr[...] = jnp.zeros_like(acc_scr)

    def scores(u, bufs):
        s_ref, mt_ref = bufs
        qi, t = u // nkt, u % nkt
        k = k_ref[0, pl.ds(pl.multiple_of(t * tk, tk), tk), :]
        for mi in range(2):
            s = lax.dot_general(k, qm_scr[qi, mi], (((1,), (1,)), ((), ())),
                                preferred_element_type=F32)
            s_ref[mi] = s
            mt_ref[mi] = jnp.max(s, axis=0, keepdims=True)

    def softmax_pv(u, bufs):
        s_ref, mt_ref = bufs
        qi, t = u // nkt, u % nkt
        vt = vt_scr[t]
        for mi in range(2):
            s = s_ref[mi]
            m_old = m_scr[qi, mi]
            m_new = jnp.maximum(m_old, mt_ref[mi])
            alpha = jnp.exp2(m_old - m_new)
            p = jnp.exp2(s - m_new).astype(BF16)
            acc_scr[qi, mi] = alpha * acc_scr[qi, mi] + _dot(vt, p)
            m_scr[qi, mi] = m_new

    units = nq * nkt
    bufs = ((sa_scr, ma_scr), (sb_scr, mb_scr))
    scores(0, bufs[0])

    def steps(first, count):
        for i in range(count):
            scores(first + i + 1, bufs[(i + 1) % 2])
            softmax_pv(first + i, bufs[i % 2])

    n_steps = units - 1
    n_loop = n_steps // PIPE_UNROLL

    def body(j, carry):
        steps(PIPE_UNROLL * j, PIPE_UNROLL)
        return carry

    lax.fori_loop(0, n_loop, body, 0)
    steps(n_loop * PIPE_UNROLL, n_steps - n_loop * PIPE_UNROLL)
    softmax_pv(units - 1, bufs[(units - 1) % 2])

    lv = lam_ref[...]
    lam = (jnp.exp(jnp.sum(lv[0:1] * lv[1:2], axis=1, keepdims=True))
           - jnp.exp(jnp.sum(lv[2:3] * lv[3:4], axis=1, keepdims=True)) + lam_init)
    for qi in range(nq):
        a1 = acc_scr[qi, 0]
        a2 = acc_scr[qi, 1]
        ot = a1[:LANES] / a1[LANES:LANES + 1] - lam * (a2[:LANES] / a2[LANES:LANES + 1])
        o = _rmsnorm(ot.T, g_ref[...]) * (1.0 - lam_init)
        o_ref[0, qi * tq:(qi + 1) * tq, :] = o.astype(o_ref.dtype)


def diff_attention(src, n_q, q_off, nk, kv_off, lam_vecs, subln_g, lam_init, tk, tq=256, nq=4):
    b, _, c3 = src.shape
    d = c3 // 3
    h = d // LANES
    assert nk % tk == 0 and kv_off % nk == 0
    nkt = nk // tk
    tq = _row_tile(n_q, tq)
    nq = min(nq, n_q // tq)
    tb = nq * tq
    assert n_q % tb == 0 and q_off % tb == 0
    qb, kb = q_off // tb, kv_off // nk
    return pl.pallas_call(
        functools.partial(_attn_kernel, nkt=nkt, tk=tk, nq=nq, tq=tq, lam_init=lam_init, half=LANES // 2),
        grid=(b, h, n_q // tb),
        in_specs=[pl.BlockSpec((1, tb, LANES), lambda bi, hi, i: (bi, qb + i, hi)),
                  pl.BlockSpec((1, nk, LANES), lambda bi, hi, i: (bi, kb, h + hi)),
                  pl.BlockSpec((1, nk, LANES), lambda bi, hi, i: (bi, kb, 2 * h + hi)),
                  pl.BlockSpec((4, LANES // 2), lambda bi, hi, i: (0, 0)),
                  pl.BlockSpec((1, LANES), lambda bi, hi, i: (0, 0))],
        out_specs=pl.BlockSpec((1, tb, LANES), lambda bi, hi, i: (bi, i, hi)),
        out_shape=jax.ShapeDtypeStruct((b, n_q, d), BF16),
        scratch_shapes=[pltpu.VMEM((nkt, V_ROWS, tk), BF16), pltpu.VMEM((nq, 2, tq, LANES), BF16),
                        pltpu.VMEM((2, tk, tq), F32), pltpu.VMEM((2, tk, tq), F32),
                        pltpu.VMEM((2, 1, tq), F32), pltpu.VMEM((2, 1, tq), F32),
                        pltpu.VMEM((nq, 2, 1, tq), F32), pltpu.VMEM((nq, 2, V_ROWS, tq), F32)],
        compiler_params=_cparams("parallel", "parallel", "arbitrary"),
        name="diff_attention",
    )(src, src, src, lam_vecs, subln_g.reshape(1, LANES))


def _resid_out(x_ref, mod_ref, y, o_ref):
    o_ref[0] = x_ref[0] + mod_ref[0, 2:3, :] * y


def _attn_out_kernel(a_ref, w_ref, x_ref, mod_ref, o_ref):
    _resid_out(x_ref, mod_ref, _dot(a_ref[0], w_ref[...]), o_ref)


def attn_out(a, w, x, mod, tm=512):
    b, n, d = x.shape
    tm = _row_tile(n, tm)
    row = pl.BlockSpec((1, tm, d), lambda bi, i: (bi, i, 0))
    return pl.pallas_call(
        _attn_out_kernel,
        grid=(b, n // tm),
        in_specs=[row, pl.BlockSpec((d, d), lambda bi, i: (0, 0)), row,
                  pl.BlockSpec((1, MOD_ROWS, d), lambda bi, i: (bi, 0, 0))],
        out_specs=row,
        out_shape=jax.ShapeDtypeStruct((b, n, d), F32),
        compiler_params=_cparams("parallel", "parallel"),
        name="attn_out",
    )(a, w, x, mod)


def _hyena_out_kernel(y_ref, vg_ref, x0_ref, skip_ref, w_ref, b_ref, x_ref, mod_ref, o_ref):
    a = ((y_ref[0] + vg_ref[0] * skip_ref[...]) * x0_ref[0]).astype(BF16)
    _resid_out(x_ref, mod_ref, _dot(a, w_ref[...]) + b_ref[...], o_ref)


def hyena_out(y, vg, x0, skip, w, bias, x, mod, tm=512):
    b, n, d = x.shape
    tm = _row_tile(n, tm)
    row = pl.BlockSpec((1, tm, d), lambda bi, i: (bi, i, 0))
    vec = pl.BlockSpec((1, d), lambda bi, i: (0, 0))
    return pl.pallas_call(
        _hyena_out_kernel,
        grid=(b, n // tm),
        in_specs=[row, row, row, vec, pl.BlockSpec((d, d), lambda bi, i: (0, 0)), vec, row,
                  pl.BlockSpec((1, MOD_ROWS, d), lambda bi, i: (bi, 0, 0))],
        out_specs=row,
        out_shape=jax.ShapeDtypeStruct((b, n, d), F32),
        compiler_params=_cparams("parallel", "parallel"),
        name="hyena_out",
    )(y, vg, x0, skip.reshape(1, d), w, bias.reshape(1, d), x, mod)


def _fill_window(ext_ref, prev_ref, cur_ref, next_ref, halo, tm, i, last):
    ext_ref[0:halo, :] = jnp.where(i == 0, 0.0, prev_ref[0])
    ext_ref[halo:halo + tm, :] = cur_ref[0]
    ext_ref[halo + tm:halo + tm + halo, :] = jnp.where(i == last, 0.0, next_ref[0])


def _dwconv(ext_ref, w_ref, halo, tm, z_scr):
    width = w_ref.shape[0]
    pad = (width - 1) // 2
    offs = [halo - pad + j for j in range(width)]
    acc = None
    rows = z_scr.shape[0]
    for r in sorted({o % SUBLANES for o in offs}):
        z_scr[...] = ext_ref[pl.ds(r, rows), :]
        for j, o in enumerate(offs):
            if o % SUBLANES == r:
                term = w_ref[j:j + 1, :] * z_scr[pl.ds(o - r, tm), :]
                acc = term if acc is None else acc + term
    return acc


def _halo_specs(tm, halo, n, c):
    per = tm // halo
    nblk = n // halo
    prev = pl.BlockSpec((1, halo, c), lambda bi, i: (bi, jnp.maximum(i * per - 1, 0), 0))
    cur = pl.BlockSpec((1, tm, c), lambda bi, i: (bi, i, 0))
    nxt = pl.BlockSpec((1, halo, c), lambda bi, i: (bi, jnp.minimum((i + 1) * per, nblk - 1), 0))
    return [prev, cur, nxt]


def _hyena_in_kernel(prev_ref, cur_ref, next_ref, mod_ref, g_ref, w_ref, bin_ref, ws_ref, bs_ref, x0_ref, vg_ref,
                     xe_scr, ext_scr, *, halo, tm, d, last):
    i = pl.program_id(1)
    xe_scr[0:halo, :] = prev_ref[0]
    xe_scr[halo:halo + tm, :] = cur_ref[0]
    xe_scr[halo + tm:halo + tm + halo, :] = next_ref[0]
    h = _modulate(xe_scr[...], g_ref[...], mod_ref[0, 0:1, :], mod_ref[0, 1:2, :]).astype(BF16)
    row = lax.broadcasted_iota(jnp.int32, (tm + 2 * halo, 1), 0)
    inside = ((row >= halo) | (i > 0)) & ((row < halo + tm) | (i < last))
    pad = (ws_ref.shape[0] - 1) // 2
    parts = []
    for c in range(3):
        cols = slice(c * d, (c + 1) * d)
        u = _dot(h, w_ref[:, cols]) + bin_ref[:, cols]
        ext_scr[...] = jnp.where(inside, u, 0.0)
        acc = bs_ref[:, cols]
        for j in range(ws_ref.shape[0]):
            acc = acc + ws_ref[j:j + 1, cols] * ext_scr[pl.ds(halo - pad + j, tm), :]
        parts.append(acc)
    x0_ref[0] = parts[0]
    vg_ref[0] = parts[2] * parts[1]


def hyena_in(x, mod, g, w_in, b_in, w_short, b_short, tm=512):
    b, n, d = x.shape
    c = w_in.shape[1]
    tm = _row_tile(n, tm)
    halo = SUBLANES
    row = pl.BlockSpec((1, tm, d), lambda bi, i: (bi, i, 0))
    vec = pl.BlockSpec((1, c), lambda bi, i: (0, 0))
    out = jax.ShapeDtypeStruct((b, n, d), F32)
    return pl.pallas_call(
        functools.partial(_hyena_in_kernel, halo=halo, tm=tm, d=d, last=n // tm - 1),
        grid=(b, n // tm),
        in_specs=_halo_specs(tm, halo, n, d) + [pl.BlockSpec((1, MOD_ROWS, d), lambda bi, i: (bi, 0, 0)),
                                                pl.BlockSpec((1, d), lambda bi, i: (0, 0)),
                                                pl.BlockSpec((d, c), lambda bi, i: (0, 0),
                                                             pipeline_mode=pl.Buffered(1)),
                                                vec, pl.BlockSpec(w_short.shape, lambda bi, i: (0, 0)), vec],
        out_specs=[row, row],
        out_shape=[out, out],
        scratch_shapes=[pltpu.VMEM((tm + 2 * halo, d), F32), pltpu.VMEM((tm + 2 * halo, d), F32)],
        compiler_params=_cparams("parallel", "parallel"),
        name="hyena_in",
    )(x, x, x, mod, g.reshape(1, d), w_in, b_in.reshape(1, c), w_short, b_short.reshape(1, c))


def _conformer_out_kernel(prev_ref, cur_ref, next_ref, wdw_ref, bdw_ref, lg_ref, lb_ref, w_ref, b_ref,
                          x_ref, mod_ref, o_ref, ext_ref, z_scr, *, halo, tm, d, last):
    i = pl.program_id(1)
    _fill_window(ext_ref, prev_ref, cur_ref, next_ref, halo, tm, i, last)
    u = _dwconv(ext_ref, wdw_ref, halo, tm, z_scr) + bdw_ref[...]
    mu = jnp.mean(u, axis=-1, keepdims=True)
    uc = u - mu
    var = jnp.mean(uc * uc, axis=-1, keepdims=True)
    z = _silu(uc * lax.rsqrt(var + LN_EPS) * lg_ref[...] + lb_ref[...]).astype(BF16)
    _resid_out(x_ref, mod_ref, _dot(z, w_ref[...]) + b_ref[...], o_ref)


def conformer_out(u, w_dw, b_dw, ln_g, ln_b, w, bias, x, mod, tm=256):
    b, n, d = x.shape
    tm = _row_tile(n, tm)
    halo = 2 * SUBLANES
    assert (w_dw.shape[0] - 1) // 2 <= halo
    row = pl.BlockSpec((1, tm, d), lambda bi, i: (bi, i, 0))
    vec = pl.BlockSpec((1, d), lambda bi, i: (0, 0))
    return pl.pallas_call(
        functools.partial(_conformer_out_kernel, halo=halo, tm=tm, d=d, last=n // tm - 1),
        grid=(b, n // tm),
        in_specs=_halo_specs(tm, halo, n, d) + [pl.BlockSpec(w_dw.shape, lambda bi, i: (0, 0)), vec, vec, vec,
                                                pl.BlockSpec((d, d), lambda bi, i: (0, 0)), vec, row,
                                                pl.BlockSpec((1, MOD_ROWS, d), lambda bi, i: (bi, 0, 0))],
        out_specs=row,
        out_shape=jax.ShapeDtypeStruct((b, n, d), F32),
        scratch_shapes=[pltpu.VMEM((tm + 2 * halo, d), F32), pltpu.VMEM((tm + 2 * halo - SUBLANES, d), F32)],
        compiler_params=_cparams("parallel", "parallel"),
        name="conformer_out",
    )(u, u, u, w_dw, b_dw.reshape(1, d), ln_g.reshape(1, d), ln_b.reshape(1, d), w, bias.reshape(1, d), x, mod)


def _modglu_kernel(x_ref, mod_ref, g_ref, wa_ref, wg_ref, ba_ref, bg_ref, o_ref):
    h = _modulate(x_ref[0], g_ref[...], mod_ref[0, 0:1, :], mod_ref[0, 1:2, :]).astype(BF16)
    a = _dot(h, wa_ref[...]) + ba_ref[...]
    gt = _dot(h, wg_ref[...]) + bg_ref[...]
    o_ref[0] = a * jax.nn.sigmoid(gt)


def modglu(x, mod, g, w, bias, tm=512):
    b, n, d = x.shape
    tm = _row_tile(n, tm)
    row = pl.BlockSpec((1, tm, d), lambda bi, i: (bi, i, 0))
    bias2 = bias.reshape(1, 2 * d)
    return pl.pallas_call(
        _modglu_kernel,
        grid=(b, n // tm),
        in_specs=[row, pl.BlockSpec((1, MOD_ROWS, d), lambda bi, i: (bi, 0, 0)),
                  pl.BlockSpec((1, d), lambda bi, i: (0, 0)),
                  pl.BlockSpec((d, d), lambda bi, i: (0, 0)), pl.BlockSpec((d, d), lambda bi, i: (0, 1)),
                  pl.BlockSpec((1, d), lambda bi, i: (0, 0)), pl.BlockSpec((1, d), lambda bi, i: (0, 1))],
        out_specs=row,
        out_shape=jax.ShapeDtypeStruct((b, n, d), F32),
        compiler_params=_cparams("parallel", "parallel"),
        name="modglu",
    )(x, mod, g.reshape(1, d), w, w, bias2, bias2)


def _filter_kernel(emb_ref, t_ref, w1_ref, b1_ref, w2_ref, b2_ref, w3_ref, b3_ref, fr_ref, w4_ref, dl_ref, o_ref):
    fr = fr_ref[...]
    hdn = jnp.sin(fr * (_dot32(emb_ref[...], w1_ref[...]) + b1_ref[...]))
    hdn = jnp.sin(fr * (_dot32(hdn, w2_ref[...]) + b2_ref[...]))
    hdn = jnp.sin(fr * (_dot32(hdn, w3_ref[...]) + b3_ref[...]))
    h = _dot32(hdn, w4_ref[...]) * jnp.exp(-t_ref[...] * dl_ref[...])
    row = lax.broadcasted_iota(jnp.int32, h.shape, 0)
    col = lax.broadcasted_iota(jnp.int32, h.shape, 1)
    drop = (row == 0) & (pl.program_id(0) == 0) & (col >= h.shape[1] // 2)
    o_ref[...] = jnp.where(drop, 0.0, h)


def hyena_filter(L, w1, b1, w2, b2, w3, b3, freq, w4, d_model, tl=256):
    bands = (w1.shape[0] - 1) // 2
    hid = w1.shape[1]
    f32 = np.float32
    t_np = np.linspace(0.0, 1.0, L, dtype=f32)[:, None]
    wpos = f32(2.0 * math.pi / L) * np.arange(L, dtype=f32)
    bnd = np.linspace(1e-4, bands - 1, bands, dtype=f32)
    fw = wpos[:, None] * bnd[None, :]
    emb = jnp.asarray(np.concatenate([t_np, np.cos(fw), -np.sin(fw)], axis=-1), F32)
    t = jnp.asarray(t_np, F32)
    max_decay = math.log(HY_DECAY_TARGET) / HY_FAST_DECAY
    min_decay = math.log(HY_DECAY_TARGET) / HY_SLOW_DECAY
    deltas = np.abs(np.linspace(min_decay, max_decay, d_model, dtype=f32))
    dl2 = jnp.asarray(np.concatenate([deltas, deltas])[None, :], F32)
    tl = _row_tile(L, tl)
    td = 2 * d_model
    full = lambda a: pl.BlockSpec(a.shape, lambda i, j: (0, 0))
    vec = lambda a: a.reshape(1, -1)
    args = [emb, t, w1, vec(b1), w2, vec(b2), w3, vec(b3), vec(freq)]
    return pl.pallas_call(
        _filter_kernel,
        grid=(L // tl, 2 * d_model // td),
        in_specs=[pl.BlockSpec((tl, emb.shape[1]), lambda i, j: (i, 0)), pl.BlockSpec((tl, 1), lambda i, j: (i, 0))]
                 + [full(a) for a in args[2:]]
                 + [pl.BlockSpec((hid, td), lambda i, j: (0, j)), pl.BlockSpec((1, td), lambda i, j: (0, j))],
        out_specs=pl.BlockSpec((tl, td), lambda i, j: (i, j)),
        out_shape=jax.ShapeDtypeStruct((L, 2 * d_model), F32),
        compiler_params=_cparams("parallel", "parallel"),
        name="hyena_filter",
    )(*args, w4, dl2)


FFT_N2 = 128
DFT_BLOCKS_PER_STEP = 2
DFT_K1_PER_STEP = 4


def _split_bf16(a):
    hi = a.astype(BF16)
    return hi, (a - hi.astype(F32)).astype(BF16)


def _dotp_split(ah, al, b):
    bh, bl = _split_bf16(b)
    return _dot(ah, bh) + (_dot(ah, bl) + _dot(al, bh))


def _dotp(a, b):
    return _dotp_split(*_split_bf16(a), b)


def _to_block_major(x, n2):
    bx, L, c = x.shape
    r = L // n2
    return x.reshape(bx, r, n2 // SUBLANES, SUBLANES, c).transpose(0, 2, 1, 3, 4).reshape(
        bx, n2 // SUBLANES, r * SUBLANES, c)


def _from_block_major(y, n2):
    bx, na, rows8, c = y.shape
    r = rows8 // SUBLANES
    return y.reshape(bx, na, r, SUBLANES, c).transpose(0, 2, 1, 3, 4).reshape(bx, r * n2, c)


def _major_kernel(m_ref, x_ref, o_ref):
    mh, ml = _split_bf16(m_ref[...])
    rows_out, rows_in = m_ref.shape
    for a in range(x_ref.shape[1]):
        x2 = x_ref.at[0, a]
        o2 = o_ref.at[0, a]
        for s in range(SUBLANES):
            y = _dotp_split(mh, ml, x2[pl.ds(s, rows_in, stride=SUBLANES), :])
            o2[pl.ds(s, rows_out, stride=SUBLANES), :] = y


def dft_major(mat, x):
    bx, na, rows8, c = x.shape
    m, r = mat.shape
    assert rows8 == r * SUBLANES
    tc = LANES
    ka = DFT_BLOCKS_PER_STEP if na % DFT_BLOCKS_PER_STEP == 0 else 1
    return pl.pallas_call(
        _major_kernel,
        grid=(bx, na // ka, c // tc),
        in_specs=[pl.BlockSpec((m, r), lambda bi, a, j: (0, 0)),
                  pl.BlockSpec((1, ka, rows8, tc), lambda bi, a, j: (bi, a, 0, j))],
        out_specs=pl.BlockSpec((1, ka, m * SUBLANES, tc), lambda bi, a, j: (bi, a, 0, j)),
        out_shape=jax.ShapeDtypeStruct((bx, na, m * SUBLANES, c), F32),
        compiler_params=_cparams("parallel", "parallel", "parallel"),
        name="dft_major",
    )(mat, x)


def _twiddled_block(f_ref, tw_ref, q):
    fr, fi = f_ref[0], f_ref[1]
    twr, twi = tw_ref[q, 0:1, :], tw_ref[q, 1:2, :]
    gr = fr * twr - fi * twi
    gi = fr * twi + fi * twr
    return jnp.concatenate([jnp.concatenate([gr, -gi], axis=1), jnp.concatenate([gi, gr], axis=1)], axis=0)


def _gather_k1(ref, q, n2):
    return jnp.concatenate([ref[0, :, p, q].reshape(n2, -1) for p in range(2)], axis=0)


def _scatter_k1(ref, q, x, n2):
    for p in range(2):
        ref[0, :, p, q] = x[p * n2:(p + 1) * n2].reshape(n2 // SUBLANES, SUBLANES, -1)


def _spec_taps_kernel(f_ref, tw_ref, af_ref, ab_ref, o_ref):
    n2 = f_ref.shape[1]
    for q in range(tw_ref.shape[0]):
        g = _twiddled_block(f_ref, tw_ref, q)
        xf = _dotp(g, _gather_k1(af_ref, q, n2))
        xb = _dotp(g, _gather_k1(ab_ref, q, n2))
        _scatter_k1(o_ref, q, jnp.concatenate([xf[:n2] + xb[:n2], xf[n2:] - xb[n2:]], axis=0), n2)


def _spec_mid_kernel(f_ref, tw_ref, a_ref, h_ref, o_ref):
    n2 = f_ref.shape[1]
    for q in range(tw_ref.shape[0]):
        g = _twiddled_block(f_ref, tw_ref, q)
        x = _dotp(g, _gather_k1(a_ref, q, n2))
        h = _gather_k1(h_ref, q, n2)
        xr, xi = x[:n2], x[n2:]
        hr, hi = h[:n2], h[n2:]
        z = jnp.concatenate([xr * hr - xi * hi, xr * hi + xi * hr], axis=0)
        _scatter_k1(o_ref, q, _dotp(g.T, z), n2)


def spec_stage(fmat, tw, a, h=None, tc=1024):
    bx, na, rows8, ca = a.shape
    n2 = na * SUBLANES
    n1 = rows8 // (2 * SUBLANES)
    c = ca // 2 if h is None else ca
    tc = _row_tile(c, tc)
    nj = c // tc
    kk = DFT_K1_PER_STEP if n1 % DFT_K1_PER_STEP == 0 else 1
    six = lambda t: t.reshape(t.shape[0], na, 2, n1, SUBLANES, t.shape[-1])
    blk = lambda off: pl.BlockSpec((1, na, 2, kk, SUBLANES, tc), lambda k, j, bi: (bi, 0, 0, k, 0, off + j))
    in_specs = [pl.BlockSpec((2, n2, n2), lambda k, j, bi: (0, 0, 0)),
                pl.BlockSpec((kk, 2, n2), lambda k, j, bi: (k, 0, 0)), blk(0)]
    if h is None:
        kern, args = _spec_taps_kernel, [fmat, tw, six(a), six(a)]
        in_specs.append(blk(nj))
    else:
        kern, args = _spec_mid_kernel, [fmat, tw, six(a), six(h)]
        in_specs.append(pl.BlockSpec((1, na, 2, kk, SUBLANES, tc), lambda k, j, bi: (0, 0, 0, k, 0, j)))
    out = pl.pallas_call(
        kern,
        grid=(n1 // kk, nj, bx),
        in_specs=in_specs,
        out_specs=blk(0),
        out_shape=jax.ShapeDtypeStruct((bx, na, 2, n1, SUBLANES, c), F32),
        compiler_params=_cparams("parallel", "parallel", "arbitrary"),
        name="dft_minor",
    )(*args)
    return out.reshape(bx, na, rows8, c)


def _dft_constants(n1, n2):
    n = n1 * n2
    k1 = np.arange(n1, dtype=np.float64)
    ang1 = -2.0 * np.pi * np.outer(k1, k1) / n1
    w1r, w1i = np.cos(ang1), np.sin(ang1)
    k2 = np.arange(n2, dtype=np.float64)
    ang2 = -2.0 * np.pi * np.outer(k2, k2) / n2
    fmat = np.stack([np.cos(ang2), np.sin(ang2)])
    angt = -2.0 * np.pi * np.outer(k1, k2) / n
    tw = np.stack([np.cos(angt), np.sin(angt)], axis=1)
    fwd_half = np.concatenate([w1r[:, :n1 // 2], w1i[:, :n1 // 2]], axis=0)
    inv_half = np.concatenate([w1r[:n1 // 2], w1i[:n1 // 2]], axis=1) / n
    f = lambda a: jnp.asarray(a, F32)
    return f(fwd_half), f(inv_half), f(fmat), f(tw)


def long_conv(v, taps):
    b, L, c = v.shape
    n2 = FFT_N2
    n1 = 2 * L // n2
    fwd_half, inv_half, fmat, tw = _dft_constants(n1, n2)
    hs = spec_stage(fmat, tw, dft_major(fwd_half, _to_block_major(taps[None], n2)))
    a = dft_major(fwd_half, _to_block_major(v, n2))
    y = dft_major(inv_half, spec_stage(fmat, tw, a, hs))
    return _from_block_major(y, n2)


def _dense_conv_kernel(f_ref, g_ref, v_ref, hf_ref, hb_ref, o_ref):
    n = f_ref.shape[0] // 2
    f = f_ref[...]
    x = _dotp(f, v_ref[0])
    hf = _dotp(f, hf_ref[...])
    hb = _dotp(f, hb_ref[...])
    xr, xi = x[:n], x[n:]
    hr = hf[:n] + hb[:n]
    hi = hf[n:] - hb[n:]
    z = jnp.concatenate([xr * hr - xi * hi, xr * hi + xi * hr], axis=0)
    o_ref[0] = _dotp(g_ref[...], z)


def dense_long_conv(v, taps, tc=512):
    b, L, c = v.shape
    n = 2 * L
    ang = 2.0 * np.pi * np.outer(np.arange(n, dtype=np.float64), np.arange(L, dtype=np.float64)) / n
    fmat = jnp.asarray(np.concatenate([np.cos(ang), -np.sin(ang)], axis=0), F32)
    gmat = jnp.asarray(np.concatenate([np.cos(ang.T), -np.sin(ang.T)], axis=1) / n, F32)
    tc = _row_tile(c, tc)
    nj = c // tc
    return pl.pallas_call(
        _dense_conv_kernel,
        grid=(b, nj),
        in_specs=[pl.BlockSpec((2 * n, L), lambda bi, j: (0, 0)), pl.BlockSpec((L, 2 * n), lambda bi, j: (0, 0)),
                  pl.BlockSpec((1, L, tc), lambda bi, j: (bi, 0, j)),
                  pl.BlockSpec((L, tc), lambda bi, j: (0, j)), pl.BlockSpec((L, tc), lambda bi, j: (0, nj + j))],
        out_specs=pl.BlockSpec((1, L, tc), lambda bi, j: (bi, 0, j)),
        out_shape=jax.ShapeDtypeStruct((b, L, c), F32),
        compiler_params=_cparams("parallel", "parallel"),
        name="dense_long_conv",
    )(fmat, gmat, v, taps, taps)


def _pad_mod(m3):
    return jnp.pad(m3, ((0, 0), (0, MOD_ROWS - 3), (0, 0)))


def kernel(x, c, ctx, c_ctx, w_mod, b_mod, norm_g, w_ffn_in, w_ffn_out, attn_w_qkv, attn_w_o, attn_lambda,
           attn_subln_g, hy_w_in, hy_b_in, hy_w_short, hy_b_short, hy_f_w1, hy_f_b1, hy_f_w2, hy_f_b2, hy_f_w3,
           hy_f_b3, hy_f_freq, hy_f_w4, hy_skip, hy_w_out, hy_b_out, cv_w_pw1, cv_b_pw1, cv_w_dw, cv_b_dw,
           cv_ln_g, cv_ln_b, cv_w_pw2, cv_b_pw2, final_g):
    bsz, n_lat, d = x.shape
    n_ctx = ctx.shape[1]
    depth = w_mod.shape[0]
    assert bsz + 1 <= MOD_ROWS

    rows = jnp.concatenate([c, c_ctx[None, :], jnp.zeros((MOD_ROWS - bsz - 1, d), F32)], axis=0)
    table = mod_table(rows, w_mod, b_mod).reshape(depth, MOD_ROWS, N_MOD, d)

    def mods(i, s, latent):
        r = table[i, :bsz] if latent else jnp.broadcast_to(table[i, bsz:bsz + 1], (bsz, N_MOD, d))
        return _pad_mod(r[:, 3 * s:3 * s + 3])

    bf = lambda w: w.astype(BF16)
    xc = ctx
    tk_lat = 1280 if (n_lat + n_ctx) % 1280 == 0 else n_ctx
    rope = rope_tables(n_lat)

    for i in range(depth):
        kind = i % N_MIXERS
        j = i // N_MIXERS
        last = i == depth - 1
        ctx_in_use = (not last) or kind == 0
        ctx_advance = not last
        w_in0, w_out0 = bf(w_ffn_in[i, 0]), bf(w_ffn_out[i, 0])
        w_in1, w_out1 = bf(w_ffn_in[i, 1]), bf(w_ffn_out[i, 1])

        x = ffn(x, mods(i, 0, True), norm_g[i, 0], w_in0, w_out0)
        if ctx_in_use:
            xc = ffn(xc, mods(i, 0, False), norm_g[i, 0], w_in0, w_out0)

        ml, mc = mods(i, 1, True), mods(i, 1, False)
        if kind == 0:
            lam_init = 0.8 - 0.6 * math.exp(-0.3 * i)
            wqkv = bf(attn_w_qkv[j])
            wo = bf(attn_w_o[j])
            n_all = n_lat + n_ctx
            qkv = modproj(x, ml, norm_g[i, 1], wqkv, rope=rope, rope_blocks=2, out_rows=n_all)
            qkv = modproj(xc, mc, norm_g[i, 1], wqkv, out_rows=n_all, into=qkv, row_offset=n_lat)
            o_l = diff_attention(qkv, n_lat, 0, n_all, 0, attn_lambda[j], attn_subln_g[j], lam_init, tk_lat)
            x = attn_out(o_l, wo, x, ml)
            if ctx_advance:
                o_c = diff_attention(qkv, n_ctx, n_lat, n_ctx, n_lat, attn_lambda[j], attn_subln_g[j], lam_init,
                                     n_ctx)
                yc_fn = lambda xcur: attn_out(o_c, wo, xcur, mc)
        elif kind == 1:
            w_in, w_o = bf(hy_w_in[j]), bf(hy_w_out[j])
            filt = (hy_f_w1[j], hy_f_b1[j], hy_f_w2[j], hy_f_b2[j], hy_f_w3[j], hy_f_b3[j], hy_f_freq[j], hy_f_w4[j])

            def hyena(xs, mod, L):
                x0, vg = hyena_in(xs, mod, norm_g[i, 1], w_in, hy_b_in[j], hy_w_short[j], hy_b_short[j])
                taps = hyena_filter(L, *filt, d)
                y = long_conv(vg, taps) if L == n_lat else dense_long_conv(vg, taps)
                return y, vg, x0

            y_l, vg_l, x0_l = hyena(x, ml, n_lat)
            x = hyena_out(y_l, vg_l, x0_l, hy_skip[j], w_o, hy_b_out[j], x, ml)
            if ctx_advance:
                y_c, vg_c, x0_c = hyena(xc, mc, n_ctx)
                yc_fn = lambda xcur: hyena_out(y_c, vg_c, x0_c, hy_skip[j], w_o, hy_b_out[j], xcur, mc)
        else:
            w1, w2 = bf(cv_w_pw1[j]), bf(cv_w_pw2[j])
            u_l = modglu(x, ml, norm_g[i, 1], w1, cv_b_pw1[j])
            x = conformer_out(u_l, cv_w_dw[j], cv_b_dw[j], cv_ln_g[j], cv_ln_b[j], w2, cv_b_pw2[j], x, ml)
            if ctx_advance:
                u_c = modglu(xc, mc, norm_g[i, 1], w1, cv_b_pw1[j])
                yc_fn = lambda xcur: conformer_out(u_c, cv_w_dw[j], cv_b_dw[j], cv_ln_g[j], cv_ln_b[j], w2,
                                                   cv_b_pw2[j], xcur, mc)

        x = ffn(x, mods(i, 2, True), norm_g[i, 2], w_in1, w_out1, final_g=final_g if last else None)
        if ctx_advance:
            xc = yc_fn(xc)
            xc = ffn(xc, mods(i, 2, False), norm_g[i, 2], w_in1, w_out1)
    return x
```

```python
import functools
import math

import numpy as np
import jax
import jax.numpy as jnp
from jax import lax
from jax.experimental import pallas as pl
from jax.experimental.pallas import tpu as pltpu

F32 = jnp.float32
BF16 = jnp.bfloat16

GRID_W = 64
ROPE_THETA = 10000.0
HY_FAST_DECAY = 0.3
HY_SLOW_DECAY = 1.5
HY_DECAY_TARGET = 1e-2
EPS = 1e-6
LN_EPS = 1e-5
N_MIXERS = 3
N_MOD = 9

LANES = 128
SUBLANES = 8
BF16_SUBLANES = 16
VMEM_LIMIT_BYTES = 56 * 1024 * 1024

MOD_ROWS = SUBLANES
HIGHEST = lax.Precision.HIGHEST


def _cparams(*sem):
    return pltpu.CompilerParams(dimension_semantics=sem, vmem_limit_bytes=VMEM_LIMIT_BYTES)


def _row_tile(n, want):
    t = min(n, want)
    assert n % t == 0, (n, t)
    return t


def _dot(a, b):
    return jnp.dot(a, b, preferred_element_type=F32)


def _dot32(a, b):
    return jnp.dot(a, b, preferred_element_type=F32, precision=HIGHEST)


def _rmsnorm(x, g):
    return x * lax.rsqrt(jnp.mean(x * x, axis=-1, keepdims=True) + EPS) * g


def _modulate(x, g, shift, scale):
    return _rmsnorm(x, g) * (1.0 + scale) + shift


def _silu(x):
    return x * jax.nn.sigmoid(x)


def _mod_kernel(r_ref, w_ref, b_ref, o_ref):
    r = _silu(r_ref[...]).astype(BF16)
    o_ref[0] = _dot(r, w_ref[0].astype(BF16)) + b_ref[0]


def mod_table(rows, w_mod, b_mod):
    depth, d, nm = w_mod.shape
    tn = nm // N_MOD
    return pl.pallas_call(
        _mod_kernel,
        grid=(depth, nm // tn),
        in_specs=[pl.BlockSpec((MOD_ROWS, d), lambda i, j: (0, 0)),
                  pl.BlockSpec((1, d, tn), lambda i, j: (i, 0, j)),
                  pl.BlockSpec((1, 1, tn), lambda i, j: (i, 0, j))],
        out_specs=pl.BlockSpec((1, MOD_ROWS, tn), lambda i, j: (i, 0, j)),
        out_shape=jax.ShapeDtypeStruct((depth, MOD_ROWS, nm), F32),
        compiler_params=_cparams("parallel", "parallel"),
        name="mod_table",
    )(rows, w_mod, b_mod.reshape(depth, 1, nm))


FFN_CHUNKS = 2


def _ffn_kernel(x_ref, mod_ref, g_ref, wg_ref, wu_ref, wo_ref, *rest, final):
    if final:
        fg_ref, o_ref = rest
    else:
        (o_ref,) = rest
    x = x_ref[0]
    h = _modulate(x, g_ref[...], mod_ref[0, 0:1, :], mod_ref[0, 1:2, :]).astype(BF16)
    tf = wo_ref.shape[0] // FFN_CHUNKS
    acc = None
    for c in range(FFN_CHUNKS):
        gate = _dot(h, wg_ref[:, c * tf:(c + 1) * tf])
        up = _dot(h, wu_ref[:, c * tf:(c + 1) * tf])
        act = (_silu(gate) * up).astype(BF16)
        part = _dot(act, wo_ref[c * tf:(c + 1) * tf, :])
        acc = part if acc is None else acc + part
    xn = x + 0.5 * mod_ref[0, 2:3, :] * acc
    if final:
        xn = _rmsnorm(xn, fg_ref[...])
    o_ref[0] = xn


def ffn(x, mod, g, w_in, w_out, final_g=None, tm=1024):
    b, n, d = x.shape
    ff = w_out.shape[0]
    tm = _row_tile(n, tm)
    assert (ff // FFN_CHUNKS) % LANES == 0
    final = final_g is not None
    resident = dict(pipeline_mode=pl.Buffered(1))
    in_specs = [pl.BlockSpec((1, tm, d), lambda bi, i: (bi, i, 0)),
                pl.BlockSpec((1, MOD_ROWS, d), lambda bi, i: (bi, 0, 0)),
                pl.BlockSpec((1, d), lambda bi, i: (0, 0)),
                pl.BlockSpec((d, ff), lambda bi, i: (0, 0), **resident),
                pl.BlockSpec((d, ff), lambda bi, i: (0, 1), **resident),
                pl.BlockSpec((ff, d), lambda bi, i: (0, 0), **resident)]
    args = [x, mod, g.reshape(1, d), w_in, w_in, w_out]
    if final:
        in_specs.append(pl.BlockSpec((1, d), lambda bi, i: (0, 0)))
        args.append(final_g.reshape(1, d))
    return pl.pallas_call(
        functools.partial(_ffn_kernel, final=final),
        grid=(b, n // tm),
        in_specs=in_specs,
        out_specs=pl.BlockSpec((1, tm, d), lambda bi, i: (bi, i, 0)),
        out_shape=jax.ShapeDtypeStruct((b, n, d), F32),
        compiler_params=_cparams("parallel", "parallel"),
        name="ffn",
    )(*args)


def _rope_cols(y, cos, sa, sb):
    outs = []
    for c in range(y.shape[1] // LANES):
        yc = y[:, c * LANES:(c + 1) * LANES]
        outs.append(yc * cos + pltpu.roll(yc, LANES - 16, 1) * sa + pltpu.roll(yc, 16, 1) * sb)
    return jnp.concatenate(outs, axis=1)


def _modproj_kernel(x_ref, mod_ref, g_ref, w_ref, *rest, rope_blocks, aliased):
    if rope_blocks:
        cos_ref, sa_ref, sb_ref = rest[:3]
        rest = rest[3:]
    if aliased:
        rest = rest[1:]
    o_ref, h_scr = rest
    j = pl.program_id(2)

    @pl.when(j == 0)
    def _():
        h = _modulate(x_ref[0], g_ref[...], mod_ref[0, 0:1, :], mod_ref[0, 1:2, :])
        h_scr[...] = h.astype(BF16)

    y = _dot(h_scr[...], w_ref[...])
    if rope_blocks:
        @pl.when(j < rope_blocks)
        def _():
            o_ref[0] = _rope_cols(y, cos_ref[...], sa_ref[...], sb_ref[...]).astype(o_ref.dtype)

        @pl.when(j >= rope_blocks)
        def _():
            o_ref[0] = y.astype(o_ref.dtype)
    else:
        o_ref[0] = y.astype(o_ref.dtype)


def modproj(x, mod, g, w, rope=None, rope_blocks=0, out_rows=None, into=None, row_offset=0, tm=512, tn=1024):
    b, n, d = x.shape
    nout = w.shape[1]
    tm = _row_tile(n, tm)
    out_rows = n if out_rows is None else out_rows
    assert row_offset % tm == 0 and row_offset + n <= out_rows
    off = row_offset // tm
    in_specs = [pl.BlockSpec((1, tm, d), lambda bi, i, j: (bi, i, 0)),
                pl.BlockSpec((1, MOD_ROWS, d), lambda bi, i, j: (bi, 0, 0)),
                pl.BlockSpec((1, d), lambda bi, i, j: (0, 0)),
                pl.BlockSpec((d, tn), lambda bi, i, j: (0, j))]
    args = [x, mod, g.reshape(1, d), w]
    if rope_blocks:
        in_specs += [pl.BlockSpec((tm, LANES), lambda bi, i, j: (i, 0))] * 3
        args += list(rope)
    aliases = {}
    if into is not None:
        assert into.shape == (b, out_rows, nout) and into.dtype == BF16
        in_specs.append(pl.BlockSpec(memory_space=pl.ANY))
        aliases = {len(args): 0}
        args.append(into)
    return pl.pallas_call(
        functools.partial(_modproj_kernel, rope_blocks=rope_blocks, aliased=into is not None),
        grid=(b, n // tm, nout // tn),
        in_specs=in_specs,
        out_specs=pl.BlockSpec((1, tm, tn), lambda bi, i, j: (bi, off + i, j)),
        out_shape=jax.ShapeDtypeStruct((b, out_rows, nout), BF16),
        scratch_shapes=[pltpu.VMEM((tm, d), BF16)],
        input_output_aliases=aliases,
        compiler_params=_cparams("parallel", "parallel", "arbitrary"),
        name="modproj",
    )(*args)


def rope_tables(n):
    f32 = np.float32
    rows = n // GRID_W
    row = np.repeat(np.arange(rows), GRID_W).astype(f32)
    col = np.tile(np.arange(GRID_W), rows).astype(f32)
    quarter = 16
    half = 32
    inv = (f32(ROPE_THETA) ** (-(f32(2.0) * np.arange(quarter, dtype=f32)) / f32(half))).astype(f32)
    ang_r = row[:, None] * inv
    ang_c = col[:, None] * inv
    zero = np.zeros_like(ang_r)
    cr, sr, cc, sc = np.cos(ang_r), np.sin(ang_r), np.cos(ang_c), np.sin(ang_c)
    cos64 = np.concatenate([cr, cr, cc, cc], axis=1)
    sa64 = np.concatenate([-sr, zero, -sc, zero], axis=1)
    sb64 = np.concatenate([zero, sr, zero, sc], axis=1)
    tile2 = lambda t: jnp.asarray(np.concatenate([t, t], axis=1), F32)
    return tile2(cos64), tile2(sa64), tile2(sb64)


V_ROWS = LANES + BF16_SUBLANES
PIPE_UNROLL = 8


def _attn_kernel(q_ref, k_ref, v_ref, lam_ref, g_ref, o_ref, vt_scr, qm_scr, sa_scr, sb_scr, ma_scr, mb_scr, m_scr,
                 acc_scr, *, nkt, tk, nq, tq, lam_init, half):
    @pl.when(pl.program_id(2) == 0)
    def _():
        extra = lax.broadcasted_iota(jnp.int32, (V_ROWS - LANES, tk), 0)
        for t in range(nkt):
            vt_scr[t, 0:LANES, :] = v_ref[0, t * tk:(t + 1) * tk, :].astype(F32).T.astype(BF16)
            vt_scr[t, LANES:V_ROWS, :] = jnp.where(extra == 0, 1.0, 0.0).astype(BF16)

    lane = lax.broadcasted_iota(jnp.int32, (tq, LANES), 1)
    for qi in range(nq):
        q = q_ref[0, qi * tq:(qi + 1) * tq, :].astype(F32) * (half ** -0.5 * math.log2(math.e))
        qm_scr[qi, 0] = jnp.where(lane < half, q, 0.0).astype(BF16)
        qm_scr[qi, 1] = jnp.where(lane >= half, q, 0.0).astype(BF16)
    m_scr[...] = jnp.full(m_scr.shape, -jnp.inf, F32)
    acc_scr[...] = jnp.zeros_like(acc_scr)

    def scores(u, bufs):
        s_ref, mt_ref = bufs
        qi, t = u // nkt, u % nkt
        k = k_ref[0, pl.ds(pl.multiple_of(t * tk, tk), tk), :]
        for mi in range(2):
            s = lax.dot_general(k, qm_scr[qi, mi], (((1,), (1,)), ((), ())),
                                preferred_element_type=F32)
            s_ref[mi] = s
            mt_ref[mi] = jnp.max(s, axis=0, keepdims=True)

    def softmax_pv(u, bufs):
        s_ref, mt_ref = bufs
        qi, t = u // nkt, u % nkt
        vt = vt_scr[t]
        for mi in range(2):
            s = s_ref[mi]
            m_old = m_scr[qi, mi]
            m_new = jnp.maximum(m_old, mt_ref[mi])
            alpha = jnp.exp2(m_old - m_new)
            p = jnp.exp2(s - m_new).astype(BF16)
            acc_scr[qi, mi] = alpha * acc_scr[qi, mi] + _dot(vt, p)
            m_scr[qi, mi] = m_new

    units = nq * nkt
    bufs = ((sa_scr, ma_scr), (sb_scr, mb_scr))
    scores(0, bufs[0])

    def steps(first, count):
        for i in range(count):
            scores(first + i + 1, bufs[(i + 1) % 2])
            softmax_pv(first + i, bufs[i % 2])

    n_steps = units - 1
    n_loop = n_steps // PIPE_UNROLL

    def body(j, carry):
        steps(PIPE_UNROLL * j, PIPE_UNROLL)
        return carry

    lax.fori_loop(0, n_loop, body, 0)
    steps(n_loop * PIPE_UNROLL, n_steps - n_loop * PIPE_UNROLL)
    softmax_pv(units - 1, bufs[(units - 1) % 2])

    lv = lam_ref[...]
    lam = (jnp.exp(jnp.sum(lv[0:1] * lv[1:2], axis=1, keepdims=True))
           - jnp.exp(jnp.sum(lv[2:3] * lv[3:4], axis=1, keepdims=True)) + lam_init)
    for qi in range(nq):
        a1 = acc_scr[qi, 0]
        a2 = acc_scr[qi, 1]
        ot = a1[:LANES] / a1[LANES:LANES + 1] - lam * (a2[:LANES] / a2[LANES:LANES + 1])
        o = _rmsnorm(ot.T, g_ref[...]) * (1.0 - lam_init)
        o_ref[0, qi * tq:(qi + 1) * tq, :] = o.astype(o_ref.dtype)


def diff_attention(src, n_q, q_off, nk, kv_off, lam_vecs, subln_g, lam_init, tk, tq=256, nq=4):
    b, _, c3 = src.shape
    d = c3 // 3
    h = d // LANES
    assert nk % tk == 0 and kv_off % nk == 0
    nkt = nk // tk
    tq = _row_tile(n_q, tq)
    nq = min(nq, n_q // tq)
    tb = nq * tq
    assert n_q % tb == 0 and q_off % tb == 0
    qb, kb = q_off // tb, kv_off // nk
    return pl.pallas_call(
        functools.partial(_attn_kernel, nkt=nkt, tk=tk, nq=nq, tq=tq, lam_init=lam_init, half=LANES // 2),
        grid=(b, h, n_q // tb),
        in_specs=[pl.BlockSpec((1, tb, LANES), lambda bi, hi, i: (bi, qb + i, hi)),
                  pl.BlockSpec((1, nk, LANES), lambda bi, hi, i: (bi, kb, h + hi)),
                  pl.BlockSpec((1, nk, LANES), lambda bi, hi, i: (bi, kb, 2 * h + hi)),
                  pl.BlockSpec((4, LANES // 2), lambda bi, hi, i: (0, 0)),
                  pl.BlockSpec((1, LANES), lambda bi, hi, i: (0, 0))],
        out_specs=pl.BlockSpec((1, tb, LANES), lambda bi, hi, i: (bi, i, hi)),
        out_shape=jax.ShapeDtypeStruct((b, n_q, d), BF16),
        scratch_shapes=[pltpu.VMEM((nkt, V_ROWS, tk), BF16), pltpu.VMEM((nq, 2, tq, LANES), BF16),
                        pltpu.VMEM((2, tk, tq), F32), pltpu.VMEM((2, tk, tq), F32),
                        pltpu.VMEM((2, 1, tq), F32), pltpu.VMEM((2, 1, tq), F32),
                        pltpu.VMEM((nq, 2, 1, tq), F32), pltpu.VMEM((nq, 2, V_ROWS, tq), F32)],
        compiler_params=_cparams("parallel", "parallel", "arbitrary"),
        name="diff_attention",
    )(src, src, src, lam_vecs, subln_g.reshape(1, LANES))


def _resid_out(x_ref, mod_ref, y, o_ref):
    o_ref[0] = x_ref[0] + mod_ref[0, 2:3, :] * y


def _attn_out_kernel(a_ref, w_ref, x_ref, mod_ref, o_ref):
    _resid_out(x_ref, mod_ref, _dot(a_ref[0], w_ref[...]), o_ref)


def attn_out(a, w, x, mod, tm=512):
    b, n, d = x.shape
    tm = _row_tile(n, tm)
    row = pl.BlockSpec((1, tm, d), lambda bi, i: (bi, i, 0))
    return pl.pallas_call(
        _attn_out_kernel,
        grid=(b, n // tm),
        in_specs=[row, pl.BlockSpec((d, d), lambda bi, i: (0, 0)), row,
                  pl.BlockSpec((1, MOD_ROWS, d), lambda bi, i: (bi, 0, 0))],
        out_specs=row,
        out_shape=jax.ShapeDtypeStruct((b, n, d), F32),
        compiler_params=_cparams("parallel", "parallel"),
        name="attn_out",
    )(a, w, x, mod)


def _hyena_out_kernel(y_ref, vg_ref, x0_ref, skip_ref, w_ref, b_ref, x_ref, mod_ref, o_ref):
    a = ((y_ref[0] + vg_ref[0] * skip_ref[...]) * x0_ref[0]).astype(BF16)
    _resid_out(x_ref, mod_ref, _dot(a, w_ref[...]) + b_ref[...], o_ref)


def hyena_out(y, vg, x0, skip, w, bias, x, mod, tm=512):
    b, n, d = x.shape
    tm = _row_tile(n, tm)
    row = pl.BlockSpec((1, tm, d), lambda bi, i: (bi, i, 0))
    vec = pl.BlockSpec((1, d), lambda bi, i: (0, 0))
    return pl.pallas_call(
        _hyena_out_kernel,
        grid=(b, n // tm),
        in_specs=[row, row, row, vec, pl.BlockSpec((d, d), lambda bi, i: (0, 0)), vec, row,
                  pl.BlockSpec((1, MOD_ROWS, d), lambda bi, i: (bi, 0, 0))],
        out_specs=row,
        out_shape=jax.ShapeDtypeStruct((b, n, d), F32),
        compiler_params=_cparams("parallel", "parallel"),
        name="hyena_out",
    )(y, vg, x0, skip.reshape(1, d), w, bias.reshape(1, d), x, mod)


def _fill_window(ext_ref, prev_ref, cur_ref, next_ref, halo, tm, i, last):
    ext_ref[0:halo, :] = jnp.where(i == 0, 0.0, prev_ref[0])
    ext_ref[halo:halo + tm, :] = cur_ref[0]
    ext_ref[halo + tm:halo + tm + halo, :] = jnp.where(i == last, 0.0, next_ref[0])


def _dwconv(ext_ref, w_ref, halo, tm, z_scr):
    width = w_ref.shape[0]
    pad = (width - 1) // 2
    offs = [halo - pad + j for j in range(width)]
    acc = None
    rows = z_scr.shape[0]
    for r in sorted({o % SUBLANES for o in offs}):
        z_scr[...] = ext_ref[pl.ds(r, rows), :]
        for j, o in enumerate(offs):
            if o % SUBLANES == r:
                term = w_ref[j:j + 1, :] * z_scr[pl.ds(o - r, tm), :]
                acc = term if acc is None else acc + term
    return acc


def _halo_specs(tm, halo, n, c):
    per = tm // halo
    nblk = n // halo
    prev = pl.BlockSpec((1, halo, c), lambda bi, i: (bi, jnp.maximum(i * per - 1, 0), 0))
    cur = pl.BlockSpec((1, tm, c), lambda bi, i: (bi, i, 0))
    nxt = pl.BlockSpec((1, halo, c), lambda bi, i: (bi, jnp.minimum((i + 1) * per, nblk - 1), 0))
    return [prev, cur, nxt]


def _hyena_in_kernel(prev_ref, cur_ref, next_ref, mod_ref, g_ref, w_ref, bin_ref, ws_ref, bs_ref, x0_ref, vg_ref,
                     xe_scr, ext_scr, *, halo, tm, d, last):
    i = pl.program_id(1)
    xe_scr[0:halo, :] = prev_ref[0]
    xe_scr[halo:halo + tm, :] = cur_ref[0]
    xe_scr[halo + tm:halo + tm + halo, :] = next_ref[0]
    h = _modulate(xe_scr[...], g_ref[...], mod_ref[0, 0:1, :], mod_ref[0, 1:2, :]).astype(BF16)
    row = lax.broadcasted_iota(jnp.int32, (tm + 2 * halo, 1), 0)
    inside = ((row >= halo) | (i > 0)) & ((row < halo + tm) | (i < last))
    pad = (ws_ref.shape[0] - 1) // 2
    parts = []
    for c in range(3):
        cols = slice(c * d, (c + 1) * d)
        u = _dot(h, w_ref[:, cols]) + bin_ref[:, cols]
        ext_scr[...] = jnp.where(inside, u, 0.0)
        acc = bs_ref[:, cols]
        for j in range(ws_ref.shape[0]):
            acc = acc + ws_ref[j:j + 1, cols] * ext_scr[pl.ds(halo - pad + j, tm), :]
        parts.append(acc)
    x0_ref[0] = parts[0]
    vg_ref[0] = parts[2] * parts[1]


def hyena_in(x, mod, g, w_in, b_in, w_short, b_short, tm=512):
    b, n, d = x.shape
    c = w_in.shape[1]
    tm = _row_tile(n, tm)
    halo = SUBLANES
    row = pl.BlockSpec((1, tm, d), lambda bi, i: (bi, i, 0))
    vec = pl.BlockSpec((1, c), lambda bi, i: (0, 0))
    out = jax.ShapeDtypeStruct((b, n, d), F32)
    return pl.pallas_call(
        functools.partial(_hyena_in_kernel, halo=halo, tm=tm, d=d, last=n // tm - 1),
        grid=(b, n // tm),
        in_specs=_halo_specs(tm, halo, n, d) + [pl.BlockSpec((1, MOD_ROWS, d), lambda bi, i: (bi, 0, 0)),
                                                pl.BlockSpec((1, d), lambda bi, i: (0, 0)),
                                                pl.BlockSpec((d, c), lambda bi, i: (0, 0),
                                                             pipeline_mode=pl.Buffered(1)),
                                                vec, pl.BlockSpec(w_short.shape, lambda bi, i: (0, 0)), vec],
        out_specs=[row, row],
        out_shape=[out, out],
        scratch_shapes=[pltpu.VMEM((tm + 2 * halo, d), F32), pltpu.VMEM((tm + 2 * halo, d), F32)],
        compiler_params=_cparams("parallel", "parallel"),
        name="hyena_in",
    )(x, x, x, mod, g.reshape(1, d), w_in, b_in.reshape(1, c), w_short, b_short.reshape(1, c))


def _conformer_out_kernel(prev_ref, cur_ref, next_ref, wdw_ref, bdw_ref, lg_ref, lb_ref, w_ref, b_ref,
                          x_ref, mod_ref, o_ref, ext_ref, z_scr, *, halo, tm, d, last):
    i = pl.program_id(1)
    _fill_window(ext_ref, prev_ref, cur_ref, next_ref, halo, tm, i, last)
    u = _dwconv(ext_ref, wdw_ref, halo, tm, z_scr) + bdw_ref[...]
    mu = jnp.mean(u, axis=-1, keepdims=True)
    uc = u - mu
    var = jnp.mean(uc * uc, axis=-1, keepdims=True)
    z = _silu(uc * lax.rsqrt(var + LN_EPS) * lg_ref[...] + lb_ref[...]).astype(BF16)
    _resid_out(x_ref, mod_ref, _dot(z, w_ref[...]) + b_ref[...], o_ref)


def conformer_out(u, w_dw, b_dw, ln_g, ln_b, w, bias, x, mod, tm=256):
    b, n, d = x.shape
    tm = _row_tile(n, tm)
    halo = 2 * SUBLANES
    assert (w_dw.shape[0] - 1) // 2 <= halo
    row = pl.BlockSpec((1, tm, d), lambda bi, i: (bi, i, 0))
    vec = pl.BlockSpec((1, d), lambda bi, i: (0, 0))
    return pl.pallas_call(
        functools.partial(_conformer_out_kernel, halo=halo, tm=tm, d=d, last=n // tm - 1),
        grid=(b, n // tm),
        in_specs=_halo_specs(tm, halo, n, d) + [pl.BlockSpec(w_dw.shape, lambda bi, i: (0, 0)), vec, vec, vec,
                                                pl.BlockSpec((d, d), lambda bi, i: (0, 0)), vec, row,
                                                pl.BlockSpec((1, MOD_ROWS, d), lambda bi, i: (bi, 0, 0))],
        out_specs=row,
        out_shape=jax.ShapeDtypeStruct((b, n, d), F32),
        scratch_shapes=[pltpu.VMEM((tm + 2 * halo, d), F32), pltpu.VMEM((tm + 2 * halo - SUBLANES, d), F32)],
        compiler_params=_cparams("parallel", "parallel"),
        name="conformer_out",
    )(u, u, u, w_dw, b_dw.reshape(1, d), ln_g.reshape(1, d), ln_b.reshape(1, d), w, bias.reshape(1, d), x, mod)


def _modglu_kernel(x_ref, mod_ref, g_ref, wa_ref, wg_ref, ba_ref, bg_ref, o_ref):
    h = _modulate(x_ref[0], g_ref[...], mod_ref[0, 0:1, :], mod_ref[0, 1:2, :]).astype(BF16)
    a = _dot(h, wa_ref[...]) + ba_ref[...]
    gt = _dot(h, wg_ref[...]) + bg_ref[...]
    o_ref[0] = a * jax.nn.sigmoid(gt)


def modglu(x, mod, g, w, bias, tm=512):
    b, n, d = x.shape
    tm = _row_tile(n, tm)
    row = pl.BlockSpec((1, tm, d), lambda bi, i: (bi, i, 0))
    bias2 = bias.reshape(1, 2 * d)
    return pl.pallas_call(
        _modglu_kernel,
        grid=(b, n // tm),
        in_specs=[row, pl.BlockSpec((1, MOD_ROWS, d), lambda bi, i: (bi, 0, 0)),
                  pl.BlockSpec((1, d), lambda bi, i: (0, 0)),
                  pl.BlockSpec((d, d), lambda bi, i: (0, 0)), pl.BlockSpec((d, d), lambda bi, i: (0, 1)),
                  pl.BlockSpec((1, d), lambda bi, i: (0, 0)), pl.BlockSpec((1, d), lambda bi, i: (0, 1))],
        out_specs=row,
        out_shape=jax.ShapeDtypeStruct((b, n, d), F32),
        compiler_params=_cparams("parallel", "parallel"),
        name="modglu",
    )(x, mod, g.reshape(1, d), w, w, bias2, bias2)


def _filter_kernel(emb_ref, t_ref, w1_ref, b1_ref, w2_ref, b2_ref, w3_ref, b3_ref, fr_ref, w4_ref, dl_ref, o_ref):
    fr = fr_ref[...]
    hdn = jnp.sin(fr * (_dot32(emb_ref[...], w1_ref[...]) + b1_ref[...]))
    hdn = jnp.sin(fr * (_dot32(hdn, w2_ref[...]) + b2_ref[...]))
    hdn = jnp.sin(fr * (_dot32(hdn, w3_ref[...]) + b3_ref[...]))
    h = _dot32(hdn, w4_ref[...]) * jnp.exp(-t_ref[...] * dl_ref[...])
    row = lax.broadcasted_iota(jnp.int32, h.shape, 0)
    col = lax.broadcasted_iota(jnp.int32, h.shape, 1)
    drop = (row == 0) & (pl.program_id(0) == 0) & (col >= h.shape[1] // 2)
    o_ref[...] = jnp.where(drop, 0.0, h)


def hyena_filter(L, w1, b1, w2, b2, w3, b3, freq, w4, d_model, tl=256):
    bands = (w1.shape[0] - 1) // 2
    hid = w1.shape[1]
    f32 = np.float32
    t_np = np.linspace(0.0, 1.0, L, dtype=f32)[:, None]
    wpos = f32(2.0 * math.pi / L) * np.arange(L, dtype=f32)
    bnd = np.linspace(1e-4, bands - 1, bands, dtype=f32)
    fw = wpos[:, None] * bnd[None, :]
    emb = jnp.asarray(np.concatenate([t_np, np.cos(fw), -np.sin(fw)], axis=-1), F32)
    t = jnp.asarray(t_np, F32)
    max_decay = math.log(HY_DECAY_TARGET) / HY_FAST_DECAY
    min_decay = math.log(HY_DECAY_TARGET) / HY_SLOW_DECAY
    deltas = np.abs(np.linspace(min_decay, max_decay, d_model, dtype=f32))
    dl2 = jnp.asarray(np.concatenate([deltas, deltas])[None, :], F32)
    tl = _row_tile(L, tl)
    td = 2 * d_model
    full = lambda a: pl.BlockSpec(a.shape, lambda i, j: (0, 0))
    vec = lambda a: a.reshape(1, -1)
    args = [emb, t, w1, vec(b1), w2, vec(b2), w3, vec(b3), vec(freq)]
    return pl.pallas_call(
        _filter_kernel,
        grid=(L // tl, 2 * d_model // td),
        in_specs=[pl.BlockSpec((tl, emb.shape[1]), lambda i, j: (i, 0)), pl.BlockSpec((tl, 1), lambda i, j: (i, 0))]
                 + [full(a) for a in args[2:]]
                 + [pl.BlockSpec((hid, td), lambda i, j: (0, j)), pl.BlockSpec((1, td), lambda i, j: (0, j))],
        out_specs=pl.BlockSpec((tl, td), lambda i, j: (i, j)),
        out_shape=jax.ShapeDtypeStruct((L, 2 * d_model), F32),
        compiler_params=_cparams("parallel", "parallel"),
        name="hyena_filter",
    )(*args, w4, dl2)


FFT_N2 = 128
DFT_BLOCKS_PER_STEP = 2
DFT_K1_PER_STEP = 4


def _split_bf16(a):
    hi = a.astype(BF16)
    return hi, (a - hi.astype(F32)).astype(BF16)


def _dotp_split(ah, al, b):
    bh, bl = _split_bf16(b)
    return _dot(ah, bh) + (_dot(ah, bl) + _dot(al, bh))


def _dotp(a, b):
    return _dotp_split(*_split_bf16(a), b)


def _to_block_major(x, n2):
    bx, L, c = x.shape
    r = L // n2
    return x.reshape(bx, r, n2 // SUBLANES, SUBLANES, c).transpose(0, 2, 1, 3, 4).reshape(
        bx, n2 // SUBLANES, r * SUBLANES, c)


def _from_block_major(y, n2):
    bx, na, rows8, c = y.shape
    r = rows8 // SUBLANES
    return y.reshape(bx, na, r, SUBLANES, c).transpose(0, 2, 1, 3, 4).reshape(bx, r * n2, c)


def _major_kernel(m_ref, x_ref, o_ref):
    mh, ml = _split_bf16(m_ref[...])
    rows_out, rows_in = m_ref.shape
    for a in range(x_ref.shape[1]):
        x2 = x_ref.at[0, a]
        o2 = o_ref.at[0, a]
        for s in range(SUBLANES):
            y = _dotp_split(mh, ml, x2[pl.ds(s, rows_in, stride=SUBLANES), :])
            o2[pl.ds(s, rows_out, stride=SUBLANES), :] = y


def dft_major(mat, x):
    bx, na, rows8, c = x.shape
    m, r = mat.shape
    assert rows8 == r * SUBLANES
    tc = LANES
    ka = DFT_BLOCKS_PER_STEP if na % DFT_BLOCKS_PER_STEP == 0 else 1
    return pl.pallas_call(
        _major_kernel,
        grid=(bx, na // ka, c // tc),
        in_specs=[pl.BlockSpec((m, r), lambda bi, a, j: (0, 0)),
                  pl.BlockSpec((1, ka, rows8, tc), lambda bi, a, j: (bi, a, 0, j))],
        out_specs=pl.BlockSpec((1, ka, m * SUBLANES, tc), lambda bi, a, j: (bi, a, 0, j)),
        out_shape=jax.ShapeDtypeStruct((bx, na, m * SUBLANES, c), F32),
        compiler_params=_cparams("parallel", "parallel", "parallel"),
        name="dft_major",
    )(mat, x)


def _twiddled_block(f_ref, tw_ref, q):
    fr, fi = f_ref[0], f_ref[1]
    twr, twi = tw_ref[q, 0:1, :], tw_ref[q, 1:2, :]
    gr = fr * twr - fi * twi
    gi = fr * twi + fi * twr
    return jnp.concatenate([jnp.concatenate([gr, -gi], axis=1), jnp.concatenate([gi, gr], axis=1)], axis=0)


def _gather_k1(ref, q, n2):
    return jnp.concatenate([ref[0, :, p, q].reshape(n2, -1) for p in range(2)], axis=0)


def _scatter_k1(ref, q, x, n2):
    for p in range(2):
        ref[0, :, p, q] = x[p * n2:(p + 1) * n2].reshape(n2 // SUBLANES, SUBLANES, -1)


def _spec_taps_kernel(f_ref, tw_ref, af_ref, ab_ref, o_ref):
    n2 = f_ref.shape[1]
    for q in range(tw_ref.shape[0]):
        g = _twiddled_block(f_ref, tw_ref, q)
        xf = _dotp(g, _gather_k1(af_ref, q, n2))
        xb = _dotp(g, _gather_k1(ab_ref, q, n2))
        _scatter_k1(o_ref, q, jnp.concatenate([xf[:n2] + xb[:n2], xf[n2:] - xb[n2:]], axis=0), n2)


def _spec_mid_kernel(f_ref, tw_ref, a_ref, h_ref, o_ref):
    n2 = f_ref.shape[1]
    for q in range(tw_ref.shape[0]):
        g = _twiddled_block(f_ref, tw_ref, q)
        x = _dotp(g, _gather_k1(a_ref, q, n2))
        h = _gather_k1(h_ref, q, n2)
        xr, xi = x[:n2], x[n2:]
        hr, hi = h[:n2], h[n2:]
        z = jnp.concatenate([xr * hr - xi * hi, xr * hi + xi * hr], axis=0)
        _scatter_k1(o_ref, q, _dotp(g.T, z), n2)


def spec_stage(fmat, tw, a, h=None, tc=1024):
    bx, na, rows8, ca = a.shape
    n2 = na * SUBLANES
    n1 = rows8 // (2 * SUBLANES)
    c = ca // 2 if h is None else ca
    tc = _row_tile(c, tc)
    nj = c // tc
    kk = DFT_K1_PER_STEP if n1 % DFT_K1_PER_STEP == 0 else 1
    six = lambda t: t.reshape(t.shape[0], na, 2, n1, SUBLANES, t.shape[-1])
    blk = lambda off: pl.BlockSpec((1, na, 2, kk, SUBLANES, tc), lambda k, j, bi: (bi, 0, 0, k, 0, off + j))
    in_specs = [pl.BlockSpec((2, n2, n2), lambda k, j, bi: (0, 0, 0)),
                pl.BlockSpec((kk, 2, n2), lambda k, j, bi: (k, 0, 0)), blk(0)]
    if h is None:
        kern, args = _spec_taps_kernel, [fmat, tw, six(a), six(a)]
        in_specs.append(blk(nj))
    else:
        kern, args = _spec_mid_kernel, [fmat, tw, six(a), six(h)]
        in_specs.append(pl.BlockSpec((1, na, 2, kk, SUBLANES, tc), lambda k, j, bi: (0, 0, 0, k, 0, j)))
    out = pl.pallas_call(
        kern,
        grid=(n1 // kk, nj, bx),
        in_specs=in_specs,
        out_specs=blk(0),
        out_shape=jax.ShapeDtypeStruct((bx, na, 2, n1, SUBLANES, c), F32),
        compiler_params=_cparams("parallel", "parallel", "arbitrary"),
        name="dft_minor",
    )(*args)
    return out.reshape(bx, na, rows8, c)


def _dft_constants(n1, n2):
    n = n1 * n2
    k1 = np.arange(n1, dtype=np.float64)
    ang1 = -2.0 * np.pi * np.outer(k1, k1) / n1
    w1r, w1i = np.cos(ang1), np.sin(ang1)
    k2 = np.arange(n2, dtype=np.float64)
    ang2 = -2.0 * np.pi * np.outer(k2, k2) / n2
    fmat = np.stack([np.cos(ang2), np.sin(ang2)])
    angt = -2.0 * np.pi * np.outer(k1, k2) / n
    tw = np.stack([np.cos(angt), np.sin(angt)], axis=1)
    fwd_half = np.concatenate([w1r[:, :n1 // 2], w1i[:, :n1 // 2]], axis=0)
    inv_half = np.concatenate([w1r[:n1 // 2], w1i[:n1 // 2]], axis=1) / n
    f = lambda a: jnp.asarray(a, F32)
    return f(fwd_half), f(inv_half), f(fmat), f(tw)


def long_conv(v, taps):
    b, L, c = v.shape
    n2 = FFT_N2
    n1 = 2 * L // n2
    fwd_half, inv_half, fmat, tw = _dft_constants(n1, n2)
    hs = spec_stage(fmat, tw, dft_major(fwd_half, _to_block_major(taps[None], n2)))
    a = dft_major(fwd_half, _to_block_major(v, n2))
    y = dft_major(inv_half, spec_stage(fmat, tw, a, hs))
    return _from_block_major(y, n2)


def _dense_conv_kernel(f_ref, g_ref, v_ref, hf_ref, hb_ref, o_ref):
    n = f_ref.shape[0] // 2
    f = f_ref[...]
    x = _dotp(f, v_ref[0])
    hf = _dotp(f, hf_ref[...])
    hb = _dotp(f, hb_ref[...])
    xr, xi = x[:n], x[n:]
    hr = hf[:n] + hb[:n]
    hi = hf[n:] - hb[n:]
    z = jnp.concatenate([xr * hr - xi * hi, xr * hi + xi * hr], axis=0)
    o_ref[0] = _dotp(g_ref[...], z)


def dense_long_conv(v, taps, tc=512):
    b, L, c = v.shape
    n = 2 * L
    ang = 2.0 * np.pi * np.outer(np.arange(n, dtype=np.float64), np.arange(L, dtype=np.float64)) / n
    fmat = jnp.asarray(np.concatenate([np.cos(ang), -np.sin(ang)], axis=0), F32)
    gmat = jnp.asarray(np.concatenate([np.cos(ang.T), -np.sin(ang.T)], axis=1) / n, F32)
    tc = _row_tile(c, tc)
    nj = c // tc
    return pl.pallas_call(
        _dense_conv_kernel,
        grid=(b, nj),
        in_specs=[pl.BlockSpec((2 * n, L), lambda bi, j: (0, 0)), pl.BlockSpec((L, 2 * n), lambda bi, j: (0, 0)),
                  pl.BlockSpec((1, L, tc), lambda bi, j: (bi, 0, j)),
                  pl.BlockSpec((L, tc), lambda bi, j: (0, j)), pl.BlockSpec((L, tc), lambda bi, j: (0, nj + j))],
        out_specs=pl.BlockSpec((1, L, tc), lambda bi, j: (bi, 0, j)),
        out_shape=jax.ShapeDtypeStruct((b, L, c), F32),
        compiler_params=_cparams("parallel", "parallel"),
        name="dense_long_conv",
    )(fmat, gmat, v, taps, taps)


def _pad_mod(m3):
    return jnp.pad(m3, ((0, 0), (0, MOD_ROWS - 3), (0, 0)))


def kernel(x, c, ctx, c_ctx, w_mod, b_mod, norm_g, w_ffn_in, w_ffn_out, attn_w_qkv, attn_w_o, attn_lambda,
           attn_subln_g, hy_w_in, hy_b_in, hy_w_short, hy_b_short, hy_f_w1, hy_f_b1, hy_f_w2, hy_f_b2, hy_f_w3,
           hy_f_b3, hy_f_freq, hy_f_w4, hy_skip, hy_w_out, hy_b_out, cv_w_pw1, cv_b_pw1, cv_w_dw, cv_b_dw,
           cv_ln_g, cv_ln_b, cv_w_pw2, cv_b_pw2, final_g):
    bsz, n_lat, d = x.shape
    n_ctx = ctx.shape[1]
    depth = w_mod.shape[0]
    assert bsz + 1 <= MOD_ROWS

    rows = jnp.concatenate([c, c_ctx[None, :], jnp.zeros((MOD_ROWS - bsz - 1, d), F32)], axis=0)
    table = mod_table(rows, w_mod, b_mod).reshape(depth, MOD_ROWS, N_MOD, d)

    def mods(i, s, latent):
        r = table[i, :bsz] if latent else jnp.broadcast_to(table[i, bsz:bsz + 1], (bsz, N_MOD, d))
        return _pad_mod(r[:, 3 * s:3 * s + 3])

    bf = lambda w: w.astype(BF16)
    xc = ctx
    tk_lat = 1280 if (n_lat + n_ctx) % 1280 == 0 else n_ctx
    rope = rope_tables(n_lat)

    for i in range(depth):
        kind = i % N_MIXERS
        j = i // N_MIXERS
        last = i == depth - 1
        ctx_in_use = (not last) or kind == 0
        ctx_advance = not last
        w_in0, w_out0 = bf(w_ffn_in[i, 0]), bf(w_ffn_out[i, 0])
        w_in1, w_out1 = bf(w_ffn_in[i, 1]), bf(w_ffn_out[i, 1])

        x = ffn(x, mods(i, 0, True), norm_g[i, 0], w_in0, w_out0)
        if ctx_in_use:
            xc = ffn(xc, mods(i, 0, False), norm_g[i, 0], w_in0, w_out0)

        ml, mc = mods(i, 1, True), mods(i, 1, False)
        if kind == 0:
            lam_init = 0.8 - 0.6 * math.exp(-0.3 * i)
            wqkv = bf(attn_w_qkv[j])
            wo = bf(attn_w_o[j])
            n_all = n_lat + n_ctx
            qkv = jnp.zeros((bsz, n_all, 3 * d), BF16)
            qkv = modproj(x, ml, norm_g[i, 1], wqkv, rope=rope, rope_blocks=2, out_rows=n_all, into=qkv)
            qkv = modproj(xc, mc, norm_g[i, 1], wqkv, out_rows=n_all, into=qkv, row_offset=n_lat)
            o_l = diff_attention(qkv, n_lat, 0, n_all, 0, attn_lambda[j], attn_subln_g[j], lam_init, tk_lat)
            x = attn_out(o_l, wo, x, ml)
            if ctx_advance:
                o_c = diff_attention(qkv, n_ctx, n_lat, n_ctx, n_lat, attn_lambda[j], attn_subln_g[j], lam_init,
                                     n_ctx)
                yc_fn = lambda xcur: attn_out(o_c, wo, xcur, mc)
        elif kind == 1:
            w_in, w_o = bf(hy_w_in[j]), bf(hy_w_out[j])
            filt = (hy_f_w1[j], hy_f_b1[j], hy_f_w2[j], hy_f_b2[j], hy_f_w3[j], hy_f_b3[j], hy_f_freq[j], hy_f_w4[j])

            def hyena(xs, mod, L):
                x0, vg = hyena_in(xs, mod, norm_g[i, 1], w_in, hy_b_in[j], hy_w_short[j], hy_b_short[j])
                taps = hyena_filter(L, *filt, d)
                y = long_conv(vg, taps) if L == n_lat else dense_long_conv(vg, taps)
                return y, vg, x0

            y_l, vg_l, x0_l = hyena(x, ml, n_lat)
            x = hyena_out(y_l, vg_l, x0_l, hy_skip[j], w_o, hy_b_out[j], x, ml)
            if ctx_advance:
                y_c, vg_c, x0_c = hyena(xc, mc, n_ctx)
                yc_fn = lambda xcur: hyena_out(y_c, vg_c, x0_c, hy_skip[j], w_o, hy_b_out[j], xcur, mc)
        else:
            w1, w2 = bf(cv_w_pw1[j]), bf(cv_w_pw2[j])
            u_l = modglu(x, ml, norm_g[i, 1], w1, cv_b_pw1[j])
            x = conformer_out(u_l, cv_w_dw[j], cv_b_dw[j], cv_ln_g[j], cv_ln_b[j], w2, cv_b_pw2[j], x, ml)
            if ctx_advance:
                u_c = modglu(xc, mc, norm_g[i, 1], w1, cv_b_pw1[j])
                yc_fn = lambda xcur: conformer_out(u_c, cv_w_dw[j], cv_b_dw[j], cv_ln_g[j], cv_ln_b[j], w2,
                                                   cv_b_pw2[j], xcur, mc)

        x = ffn(x, mods(i, 2, True), norm_g[i, 2], w_in1, w_out1, final_g=final_g if last else None)
        if ctx_advance:
            xc = yc_fn(xc)
            xc = ffn(xc, mods(i, 2, False), norm_g[i, 2], w_in1, w_out1)
    return x
```

```python
import functools
import math

import numpy as np
import jax
import jax.numpy as jnp
from jax import lax
from jax.experimental import pallas as pl
from jax.experimental.pallas import tpu as pltpu

F32 = jnp.float32
BF16 = jnp.bfloat16

GRID_W = 64
ROPE_THETA = 10000.0
HY_FAST_DECAY = 0.3
HY_SLOW_DECAY = 1.5
HY_DECAY_TARGET = 1e-2
EPS = 1e-6
LN_EPS = 1e-5
N_MIXERS = 3
N_MOD = 9

LANES = 128
SUBLANES = 8
BF16_SUBLANES = 16
VMEM_LIMIT_BYTES = 56 * 1024 * 1024

MOD_ROWS = SUBLANES
HIGHEST = lax.Precision.HIGHEST


def _cparams(*sem):
    return pltpu.CompilerParams(dimension_semantics=sem, vmem_limit_bytes=VMEM_LIMIT_BYTES)


def _row_tile(n, want):
    t = min(n, want)
    assert n % t == 0, (n, t)
    return t


def _dot(a, b):
    return jnp.dot(a, b, preferred_element_type=F32)


def _dot32(a, b):
    return jnp.dot(a, b, preferred_element_type=F32, precision=HIGHEST)


def _rmsnorm(x, g):
    return x * lax.rsqrt(jnp.mean(x * x, axis=-1, keepdims=True) + EPS) * g


def _modulate(x, g, shift, scale):
    return _rmsnorm(x, g) * (1.0 + scale) + shift


def _silu(x):
    return x * jax.nn.sigmoid(x)


def _mod_kernel(r_ref, w_ref, b_ref, o_ref):
    r = _silu(r_ref[...]).astype(BF16)
    o_ref[0] = _dot(r, w_ref[0].astype(BF16)) + b_ref[0]


def mod_table(rows, w_mod, b_mod):
    depth, d, nm = w_mod.shape
    tn = nm // N_MOD
    return pl.pallas_call(
        _mod_kernel,
        grid=(depth, nm // tn),
        in_specs=[pl.BlockSpec((MOD_ROWS, d), lambda i, j: (0, 0)),
                  pl.BlockSpec((1, d, tn), lambda i, j: (i, 0, j)),
                  pl.BlockSpec((1, 1, tn), lambda i, j: (i, 0, j))],
        out_specs=pl.BlockSpec((1, MOD_ROWS, tn), lambda i, j: (i, 0, j)),
        out_shape=jax.ShapeDtypeStruct((depth, MOD_ROWS, nm), F32),
        compiler_params=_cparams("parallel", "parallel"),
        name="mod_table",
    )(rows, w_mod, b_mod.reshape(depth, 1, nm))


FFN_CHUNKS = 11


def _ffn_kernel(x_ref, mod_ref, g_ref, wg_ref, wu_ref, wo_ref, *rest, final):
    if final:
        fg_ref, o_ref = rest
    else:
        (o_ref,) = rest
    x = x_ref[0]
    h = _modulate(x, g_ref[...], mod_ref[0, 0:1, :], mod_ref[0, 1:2, :]).astype(BF16)
    tf = wo_ref.shape[0] // FFN_CHUNKS
    acc = None
    for c in range(FFN_CHUNKS):
        gate = _dot(h, wg_ref[:, c * tf:(c + 1) * tf])
        up = _dot(h, wu_ref[:, c * tf:(c + 1) * tf])
        act = (_silu(gate) * up).astype(BF16)
        part = _dot(act, wo_ref[c * tf:(c + 1) * tf, :])
        acc = part if acc is None else acc + part
    xn = x + 0.5 * mod_ref[0, 2:3, :] * acc
    if final:
        xn = _rmsnorm(xn, fg_ref[...])
    o_ref[0] = xn


def ffn(x, mod, g, w_in, w_out, final_g=None, tm=1024):
    b, n, d = x.shape
    ff = w_out.shape[0]
    tm = _row_tile(n, tm)
    assert (ff // FFN_CHUNKS) % LANES == 0
    final = final_g is not None
    resident = dict(pipeline_mode=pl.Buffered(1))
    in_specs = [pl.BlockSpec((1, tm, d), lambda bi, i: (bi, i, 0)),
                pl.BlockSpec((1, MOD_ROWS, d), lambda bi, i: (bi, 0, 0)),
                pl.BlockSpec((1, d), lambda bi, i: (0, 0)),
                pl.BlockSpec((d, ff), lambda bi, i: (0, 0), **resident),
                pl.BlockSpec((d, ff), lambda bi, i: (0, 1), **resident),
                pl.BlockSpec((ff, d), lambda bi, i: (0, 0), **resident)]
    args = [x, mod, g.reshape(1, d), w_in, w_in, w_out]
    if final:
        in_specs.append(pl.BlockSpec((1, d), lambda bi, i: (0, 0)))
        args.append(final_g.reshape(1, d))
    return pl.pallas_call(
        functools.partial(_ffn_kernel, final=final),
        grid=(b, n // tm),
        in_specs=in_specs,
        out_specs=pl.BlockSpec((1, tm, d), lambda bi, i: (bi, i, 0)),
        out_shape=jax.ShapeDtypeStruct((b, n, d), F32),
        compiler_params=_cparams("parallel", "parallel"),
        name="ffn",
    )(*args)


def _rope_cols(y, cos, sa, sb):
    outs = []
    for c in range(y.shape[1] // LANES):
        yc = y[:, c * LANES:(c + 1) * LANES]
        outs.append(yc * cos + pltpu.roll(yc, LANES - 16, 1) * sa + pltpu.roll(yc, 16, 1) * sb)
    return jnp.concatenate(outs, axis=1)


def _modproj_kernel(x_ref, mod_ref, g_ref, w_ref, *rest, rope_blocks, aliased):
    if rope_blocks:
        cos_ref, sa_ref, sb_ref = rest[:3]
        rest = rest[3:]
    if aliased:
        rest = rest[1:]
    o_ref, h_scr = rest
    j = pl.program_id(2)

    @pl.when(j == 0)
    def _():
        h = _modulate(x_ref[0], g_ref[...], mod_ref[0, 0:1, :], mod_ref[0, 1:2, :])
        h_scr[...] = h.astype(BF16)

    y = _dot(h_scr[...], w_ref[...])
    if rope_blocks:
        @pl.when(j < rope_blocks)
        def _():
            o_ref[0] = _rope_cols(y, cos_ref[...], sa_ref[...], sb_ref[...]).astype(o_ref.dtype)

        @pl.when(j >= rope_blocks)
        def _():
            o_ref[0] = y.astype(o_ref.dtype)
    else:
        o_ref[0] = y.astype(o_ref.dtype)


def modproj(x, mod, g, w, rope=None, rope_blocks=0, out_rows=None, into=None, row_offset=0, tm=512, tn=1024):
    b, n, d = x.shape
    nout = w.shape[1]
    tm = _row_tile(n, tm)
    out_rows = n if out_rows is None else out_rows
    assert row_offset % tm == 0 and row_offset + n <= out_rows
    off = row_offset // tm
    in_specs = [pl.BlockSpec((1, tm, d), lambda bi, i, j: (bi, i, 0)),
                pl.BlockSpec((1, MOD_ROWS, d), lambda bi, i, j: (bi, 0, 0)),
                pl.BlockSpec((1, d), lambda bi, i, j: (0, 0)),
                pl.BlockSpec((d, tn), lambda bi, i, j: (0, j))]
    args = [x, mod, g.reshape(1, d), w]
    if rope_blocks:
        in_specs += [pl.BlockSpec((tm, LANES), lambda bi, i, j: (i, 0))] * 3
        args += list(rope)
    aliases = {}
    if into is not None:
        assert into.shape == (b, out_rows, nout) and into.dtype == BF16
        in_specs.append(pl.BlockSpec(memory_space=pl.ANY))
        aliases = {len(args): 0}
        args.append(into)
    return pl.pallas_call(
        functools.partial(_modproj_kernel, rope_blocks=rope_blocks, aliased=into is not None),
        grid=(b, n // tm, nout // tn),
        in_specs=in_specs,
        out_specs=pl.BlockSpec((1, tm, tn), lambda bi, i, j: (bi, off + i, j)),
        out_shape=jax.ShapeDtypeStruct((b, out_rows, nout), BF16),
        scratch_shapes=[pltpu.VMEM((tm, d), BF16)],
        input_output_aliases=aliases,
        compiler_params=_cparams("parallel", "parallel", "arbitrary"),
        name="modproj",
    )(*args)


def rope_tables(n):
    f32 = np.float32
    rows = n // GRID_W
    row = np.repeat(np.arange(rows), GRID_W).astype(f32)
    col = np.tile(np.arange(GRID_W), rows).astype(f32)
    quarter = 16
    half = 32
    inv = (f32(ROPE_THETA) ** (-(f32(2.0) * np.arange(quarter, dtype=f32)) / f32(half))).astype(f32)
    ang_r = row[:, None] * inv
    ang_c = col[:, None] * inv
    zero = np.zeros_like(ang_r)
    cr, sr, cc, sc = np.cos(ang_r), np.sin(ang_r), np.cos(ang_c), np.sin(ang_c)
    cos64 = np.concatenate([cr, cr, cc, cc], axis=1)
    sa64 = np.concatenate([-sr, zero, -sc, zero], axis=1)
    sb64 = np.concatenate([zero, sr, zero, sc], axis=1)
    tile2 = lambda t: jnp.asarray(np.concatenate([t, t], axis=1), F32)
    return tile2(cos64), tile2(sa64), tile2(sb64)


V_ROWS = LANES + BF16_SUBLANES
PIPE_UNROLL = 8


def _attn_kernel(q_ref, k_ref, v_ref, lam_ref, g_ref, o_ref, vt_scr, qm_scr, sa_scr, sb_scr, ma_scr, mb_scr, m_scr,
                 acc_scr, *, nkt, tk, nq, tq, lam_init, half):
    @pl.when(pl.program_id(2) == 0)
    def _():
        extra = lax.broadcasted_iota(jnp.int32, (V_ROWS - LANES, tk), 0)
        for t in range(nkt):
            vt_scr[t, 0:LANES, :] = v_ref[0, t * tk:(t + 1) * tk, :].astype(F32).T.astype(BF16)
            vt_scr[t, LANES:V_ROWS, :] = jnp.where(extra == 0, 1.0, 0.0).astype(BF16)

    lane = lax.broadcasted_iota(jnp.int32, (tq, LANES), 1)
    for qi in range(nq):
        q = q_ref[0, qi * tq:(qi + 1) * tq, :].astype(F32) * (half ** -0.5 * math.log2(math.e))
        qm_scr[qi, 0] = jnp.where(lane < half, q, 0.0).astype(BF16)
        qm_scr[qi, 1] = jnp.where(lane >= half, q, 0.0).astype(BF16)
    m_scr[...] = jnp.full(m_scr.shape, -jnp.inf, F32)
    acc_scr[...] = jnp.zeros_like(acc_scr)

    def scores(u, bufs):
        s_ref, mt_ref = bufs
        qi, t = u // nkt, u % nkt
        k = k_ref[0, pl.ds(pl.multiple_of(t * tk, tk), tk), :]
        for mi in range(2):
            s = lax.dot_general(k, qm_scr[qi, mi], (((1,), (1,)), ((), ())),
                                preferred_element_type=F32)
            s_ref[mi] = s
            mt_ref[mi] = jnp.max(s, axis=0, keepdims=True)

    def softmax_pv(u, bufs):
        s_ref, mt_ref = bufs
        qi, t = u // nkt, u % nkt
        vt = vt_scr[t]
        for mi in range(2):
            s = s_ref[mi]
            m_old = m_scr[qi, mi]
            m_new = jnp.maximum(m_old, mt_ref[mi])
            alpha = jnp.exp2(m_old - m_new)
            p = jnp.exp2(s - m_new).astype(BF16)
            acc_scr[qi, mi] = alpha * acc_scr[qi, mi] + _dot(vt, p)
            m_scr[qi, mi] = m_new

    units = nq * nkt
    bufs = ((sa_scr, ma_scr), (sb_scr, mb_scr))
    scores(0, bufs[0])

    def steps(first, count):
        for i in range(count):
            scores(first + i + 1, bufs[(i + 1) % 2])
            softmax_pv(first + i, bufs[i % 2])

    n_steps = units - 1
    n_loop = n_steps // PIPE_UNROLL

    def body(j, carry):
        steps(PIPE_UNROLL * j, PIPE_UNROLL)
        return carry

    lax.fori_loop(0, n_loop, body, 0)
    steps(n_loop * PIPE_UNROLL, n_steps - n_loop * PIPE_UNROLL)
    softmax_pv(units - 1, bufs[(units - 1) % 2])

    lv = lam_ref[...]
    lam = (jnp.exp(jnp.sum(lv[0:1] * lv[1:2], axis=1, keepdims=True))
           - jnp.exp(jnp.sum(lv[2:3] * lv[3:4], axis=1, keepdims=True)) + lam_init)
    for qi in range(nq):
        a1 = acc_scr[qi, 0]
        a2 = acc_scr[qi, 1]
        ot = a1[:LANES] / a1[LANES:LANES + 1] - lam * (a2[:LANES] / a2[LANES:LANES + 1])
        o = _rmsnorm(ot.T, g_ref[...]) * (1.0 - lam_init)
        o_ref[0, qi * tq:(qi + 1) * tq, :] = o.astype(o_ref.dtype)


def diff_attention(src, n_q, q_off, nk, kv_off, lam_vecs, subln_g, lam_init, tk, tq=256, nq=4):
    b, _, c3 = src.shape
    d = c3 // 3
    h = d // LANES
    assert nk % tk == 0 and kv_off % nk == 0
    nkt = nk // tk
    tq = _row_tile(n_q, tq)
    nq = min(nq, n_q // tq)
    tb = nq * tq
    assert n_q % tb == 0 and q_off % tb == 0
    qb, kb = q_off // tb, kv_off // nk
    return pl.pallas_call(
        functools.partial(_attn_kernel, nkt=nkt, tk=tk, nq=nq, tq=tq, lam_init=lam_init, half=LANES // 2),
        grid=(b, h, n_q // tb),
        in_specs=[pl.BlockSpec((1, tb, LANES), lambda bi, hi, i: (bi, qb + i, hi)),
                  pl.BlockSpec((1, nk, LANES), lambda bi, hi, i: (bi, kb, h + hi)),
                  pl.BlockSpec((1, nk, LANES), lambda bi, hi, i: (bi, kb, 2 * h + hi)),
                  pl.BlockSpec((4, LANES // 2), lambda bi, hi, i: (0, 0)),
                  pl.BlockSpec((1, LANES), lambda bi, hi, i: (0, 0))],
        out_specs=pl.BlockSpec((1, tb, LANES), lambda bi, hi, i: (bi, i, hi)),
        out_shape=jax.ShapeDtypeStruct((b, n_q, d), BF16),
        scratch_shapes=[pltpu.VMEM((nkt, V_ROWS, tk), BF16), pltpu.VMEM((nq, 2, tq, LANES), BF16),
                        pltpu.VMEM((2, tk, tq), F32), pltpu.VMEM((2, tk, tq), F32),
                        pltpu.VMEM((2, 1, tq), F32), pltpu.VMEM((2, 1, tq), F32),
                        pltpu.VMEM((nq, 2, 1, tq), F32), pltpu.VMEM((nq, 2, V_ROWS, tq), F32)],
        compiler_params=_cparams("parallel", "parallel", "arbitrary"),
        name="diff_attention",
    )(src, src, src, lam_vecs, subln_g.reshape(1, LANES))


def _resid_out(x_ref, mod_ref, y, o_ref):
    o_ref[0] = x_ref[0] + mod_ref[0, 2:3, :] * y


def _attn_out_kernel(a_ref, w_ref, x_ref, mod_ref, o_ref):
    _resid_out(x_ref, mod_ref, _dot(a_ref[0], w_ref[...]), o_ref)


def attn_out(a, w, x, mod, tm=512):
    b, n, d = x.shape
    tm = _row_tile(n, tm)
    row = pl.BlockSpec((1, tm, d), lambda bi, i: (bi, i, 0))
    return pl.pallas_call(
        _attn_out_kernel,
        grid=(b, n // tm),
        in_specs=[row, pl.BlockSpec((d, d), lambda bi, i: (0, 0)), row,
                  pl.BlockSpec((1, MOD_ROWS, d), lambda bi, i: (bi, 0, 0))],
        out_specs=row,
        out_shape=jax.ShapeDtypeStruct((b, n, d), F32),
        compiler_params=_cparams("parallel", "parallel"),
        name="attn_out",
    )(a, w, x, mod)


def _hyena_out_kernel(y_ref, vg_ref, x0_ref, skip_ref, w_ref, b_ref, x_ref, mod_ref, o_ref):
    a = ((y_ref[0] + vg_ref[0] * skip_ref[...]) * x0_ref[0]).astype(BF16)
    _resid_out(x_ref, mod_ref, _dot(a, w_ref[...]) + b_ref[...], o_ref)


def hyena_out(y, vg, x0, skip, w, bias, x, mod, tm=512):
    b, n, d = x.shape
    tm = _row_tile(n, tm)
    row = pl.BlockSpec((1, tm, d), lambda bi, i: (bi, i, 0))
    vec = pl.BlockSpec((1, d), lambda bi, i: (0, 0))
    return pl.pallas_call(
        _hyena_out_kernel,
        grid=(b, n // tm),
        in_specs=[row, row, row, vec, pl.BlockSpec((d, d), lambda bi, i: (0, 0)), vec, row,
                  pl.BlockSpec((1, MOD_ROWS, d), lambda bi, i: (bi, 0, 0))],
        out_specs=row,
        out_shape=jax.ShapeDtypeStruct((b, n, d), F32),
        compiler_params=_cparams("parallel", "parallel"),
        name="hyena_out",
    )(y, vg, x0, skip.reshape(1, d), w, bias.reshape(1, d), x, mod)


def _fill_window(ext_ref, prev_ref, cur_ref, next_ref, halo, tm, i, last):
    ext_ref[0:halo, :] = jnp.where(i == 0, 0.0, prev_ref[0])
    ext_ref[halo:halo + tm, :] = cur_ref[0]
    ext_ref[halo + tm:halo + tm + halo, :] = jnp.where(i == last, 0.0, next_ref[0])


def _dwconv(ext_ref, w_ref, halo, tm, z_scr):
    width = w_ref.shape[0]
    pad = (width - 1) // 2
    offs = [halo - pad + j for j in range(width)]
    acc = None
    rows = z_scr.shape[0]
    for r in sorted({o % SUBLANES for o in offs}):
        z_scr[...] = ext_ref[pl.ds(r, rows), :]
        for j, o in enumerate(offs):
            if o % SUBLANES == r:
                term = w_ref[j:j + 1, :] * z_scr[pl.ds(o - r, tm), :]
                acc = term if acc is None else acc + term
    return acc


def _halo_specs(tm, halo, n, c):
    per = tm // halo
    nblk = n // halo
    prev = pl.BlockSpec((1, halo, c), lambda bi, i: (bi, jnp.maximum(i * per - 1, 0), 0))
    cur = pl.BlockSpec((1, tm, c), lambda bi, i: (bi, i, 0))
    nxt = pl.BlockSpec((1, halo, c), lambda bi, i: (bi, jnp.minimum((i + 1) * per, nblk - 1), 0))
    return [prev, cur, nxt]


def _hyena_in_kernel(prev_ref, cur_ref, next_ref, mod_ref, g_ref, w_ref, bin_ref, ws_ref, bs_ref, x0_ref, vg_ref,
                     xe_scr, ext_scr, *, halo, tm, d, last):
    i = pl.program_id(1)
    xe_scr[0:halo, :] = prev_ref[0]
    xe_scr[halo:halo + tm, :] = cur_ref[0]
    xe_scr[halo + tm:halo + tm + halo, :] = next_ref[0]
    h = _modulate(xe_scr[...], g_ref[...], mod_ref[0, 0:1, :], mod_ref[0, 1:2, :]).astype(BF16)
    row = lax.broadcasted_iota(jnp.int32, (tm + 2 * halo, 1), 0)
    inside = ((row >= halo) | (i > 0)) & ((row < halo + tm) | (i < last))
    pad = (ws_ref.shape[0] - 1) // 2
    parts = []
    for c in range(3):
        cols = slice(c * d, (c + 1) * d)
        u = _dot(h, w_ref[:, cols]) + bin_ref[:, cols]
        ext_scr[...] = jnp.where(inside, u, 0.0)
        acc = bs_ref[:, cols]
        for j in range(ws_ref.shape[0]):
            acc = acc + ws_ref[j:j + 1, cols] * ext_scr[pl.ds(halo - pad + j, tm), :]
        parts.append(acc)
    x0_ref[0] = parts[0]
    vg_ref[0] = parts[2] * parts[1]


def hyena_in(x, mod, g, w_in, b_in, w_short, b_short, tm=512):
    b, n, d = x.shape
    c = w_in.shape[1]
    tm = _row_tile(n, tm)
    halo = SUBLANES
    row = pl.BlockSpec((1, tm, d), lambda bi, i: (bi, i, 0))
    vec = pl.BlockSpec((1, c), lambda bi, i: (0, 0))
    out = jax.ShapeDtypeStruct((b, n, d), F32)
    return pl.pallas_call(
        functools.partial(_hyena_in_kernel, halo=halo, tm=tm, d=d, last=n // tm - 1),
        grid=(b, n // tm),
        in_specs=_halo_specs(tm, halo, n, d) + [pl.BlockSpec((1, MOD_ROWS, d), lambda bi, i: (bi, 0, 0)),
                                                pl.BlockSpec((1, d), lambda bi, i: (0, 0)),
                                                pl.BlockSpec((d, c), lambda bi, i: (0, 0),
                                                             pipeline_mode=pl.Buffered(1)),
                                                vec, pl.BlockSpec(w_short.shape, lambda bi, i: (0, 0)), vec],
        out_specs=[row, row],
        out_shape=[out, out],
        scratch_shapes=[pltpu.VMEM((tm + 2 * halo, d), F32), pltpu.VMEM((tm + 2 * halo, d), F32)],
        compiler_params=_cparams("parallel", "parallel"),
        name="hyena_in",
    )(x, x, x, mod, g.reshape(1, d), w_in, b_in.reshape(1, c), w_short, b_short.reshape(1, c))


def _conformer_out_kernel(prev_ref, cur_ref, next_ref, wdw_ref, bdw_ref, lg_ref, lb_ref, w_ref, b_ref,
                          x_ref, mod_ref, o_ref, ext_ref, z_scr, *, halo, tm, d, last):
    i = pl.program_id(1)
    _fill_window(ext_ref, prev_ref, cur_ref, next_ref, halo, tm, i, last)
    u = _dwconv(ext_ref, wdw_ref, halo, tm, z_scr) + bdw_ref[...]
    mu = jnp.mean(u, axis=-1, keepdims=True)
    uc = u - mu
    var = jnp.mean(uc * uc, axis=-1, keepdims=True)
    z = _silu(uc * lax.rsqrt(var + LN_EPS) * lg_ref[...] + lb_ref[...]).astype(BF16)
    _resid_out(x_ref, mod_ref, _dot(z, w_ref[...]) + b_ref[...], o_ref)


def conformer_out(u, w_dw, b_dw, ln_g, ln_b, w, bias, x, mod, tm=256):
    b, n, d = x.shape
    tm = _row_tile(n, tm)
    halo = 2 * SUBLANES
    assert (w_dw.shape[0] - 1) // 2 <= halo
    row = pl.BlockSpec((1, tm, d), lambda bi, i: (bi, i, 0))
    vec = pl.BlockSpec((1, d), lambda bi, i: (0, 0))
    return pl.pallas_call(
        functools.partial(_conformer_out_kernel, halo=halo, tm=tm, d=d, last=n // tm - 1),
        grid=(b, n // tm),
        in_specs=_halo_specs(tm, halo, n, d) + [pl.BlockSpec(w_dw.shape, lambda bi, i: (0, 0)), vec, vec, vec,
                                                pl.BlockSpec((d, d), lambda bi, i: (0, 0)), vec, row,
                                                pl.BlockSpec((1, MOD_ROWS, d), lambda bi, i: (bi, 0, 0))],
        out_specs=row,
        out_shape=jax.ShapeDtypeStruct((b, n, d), F32),
        scratch_shapes=[pltpu.VMEM((tm + 2 * halo, d), F32), pltpu.VMEM((tm + 2 * halo - SUBLANES, d), F32)],
        compiler_params=_cparams("parallel", "parallel"),
        name="conformer_out",
    )(u, u, u, w_dw, b_dw.reshape(1, d), ln_g.reshape(1, d), ln_b.reshape(1, d), w, bias.reshape(1, d), x, mod)


def _modglu_kernel(x_ref, mod_ref, g_ref, wa_ref, wg_ref, ba_ref, bg_ref, o_ref):
    h = _modulate(x_ref[0], g_ref[...], mod_ref[0, 0:1, :], mod_ref[0, 1:2, :]).astype(BF16)
    a = _dot(h, wa_ref[...]) + ba_ref[...]
    gt = _dot(h, wg_ref[...]) + bg_ref[...]
    o_ref[0] = a * jax.nn.sigmoid(gt)


def modglu(x, mod, g, w, bias, tm=512):
    b, n, d = x.shape
    tm = _row_tile(n, tm)
    row = pl.BlockSpec((1, tm, d), lambda bi, i: (bi, i, 0))
    bias2 = bias.reshape(1, 2 * d)
    return pl.pallas_call(
        _modglu_kernel,
        grid=(b, n // tm),
        in_specs=[row, pl.BlockSpec((1, MOD_ROWS, d), lambda bi, i: (bi, 0, 0)),
                  pl.BlockSpec((1, d), lambda bi, i: (0, 0)),
                  pl.BlockSpec((d, d), lambda bi, i: (0, 0)), pl.BlockSpec((d, d), lambda bi, i: (0, 1)),
                  pl.BlockSpec((1, d), lambda bi, i: (0, 0)), pl.BlockSpec((1, d), lambda bi, i: (0, 1))],
        out_specs=row,
        out_shape=jax.ShapeDtypeStruct((b, n, d), F32),
        compiler_params=_cparams("parallel", "parallel"),
        name="modglu",
    )(x, mod, g.reshape(1, d), w, w, bias2, bias2)


def _filter_kernel(emb_ref, t_ref, w1_ref, b1_ref, w2_ref, b2_ref, w3_ref, b3_ref, fr_ref, w4_ref, dl_ref, o_ref):
    fr = fr_ref[...]
    hdn = jnp.sin(fr * (_dot32(emb_ref[...], w1_ref[...]) + b1_ref[...]))
    hdn = jnp.sin(fr * (_dot32(hdn, w2_ref[...]) + b2_ref[...]))
    hdn = jnp.sin(fr * (_dot32(hdn, w3_ref[...]) + b3_ref[...]))
    h = _dot32(hdn, w4_ref[...]) * jnp.exp(-t_ref[...] * dl_ref[...])
    row = lax.broadcasted_iota(jnp.int32, h.shape, 0)
    col = lax.broadcasted_iota(jnp.int32, h.shape, 1)
    drop = (row == 0) & (pl.program_id(0) == 0) & (col >= h.shape[1] // 2)
    o_ref[...] = jnp.where(drop, 0.0, h)


def hyena_filter(L, w1, b1, w2, b2, w3, b3, freq, w4, d_model, tl=256):
    bands = (w1.shape[0] - 1) // 2
    hid = w1.shape[1]
    f32 = np.float32
    t_np = np.linspace(0.0, 1.0, L, dtype=f32)[:, None]
    wpos = f32(2.0 * math.pi / L) * np.arange(L, dtype=f32)
    bnd = np.linspace(1e-4, bands - 1, bands, dtype=f32)
    fw = wpos[:, None] * bnd[None, :]
    emb = jnp.asarray(np.concatenate([t_np, np.cos(fw), -np.sin(fw)], axis=-1), F32)
    t = jnp.asarray(t_np, F32)
    max_decay = math.log(HY_DECAY_TARGET) / HY_FAST_DECAY
    min_decay = math.log(HY_DECAY_TARGET) / HY_SLOW_DECAY
    deltas = np.abs(np.linspace(min_decay, max_decay, d_model, dtype=f32))
    dl2 = jnp.asarray(np.concatenate([deltas, deltas])[None, :], F32)
    tl = _row_tile(L, tl)
    td = 2 * d_model
    full = lambda a: pl.BlockSpec(a.shape, lambda i, j: (0, 0))
    vec = lambda a: a.reshape(1, -1)
    args = [emb, t, w1, vec(b1), w2, vec(b2), w3, vec(b3), vec(freq)]
    return pl.pallas_call(
        _filter_kernel,
        grid=(L // tl, 2 * d_model // td),
        in_specs=[pl.BlockSpec((tl, emb.shape[1]), lambda i, j: (i, 0)), pl.BlockSpec((tl, 1), lambda i, j: (i, 0))]
                 + [full(a) for a in args[2:]]
                 + [pl.BlockSpec((hid, td), lambda i, j: (0, j)), pl.BlockSpec((1, td), lambda i, j: (0, j))],
        out_specs=pl.BlockSpec((tl, td), lambda i, j: (i, j)),
        out_shape=jax.ShapeDtypeStruct((L, 2 * d_model), F32),
        compiler_params=_cparams("parallel", "parallel"),
        name="hyena_filter",
    )(*args, w4, dl2)


FFT_N2 = 128
DFT_BLOCKS_PER_STEP = 2
DFT_K1_PER_STEP = 4


def _split_bf16(a):
    hi = a.astype(BF16)
    return hi, (a - hi.astype(F32)).astype(BF16)


def _dotp_split(ah, al, b):
    bh, bl = _split_bf16(b)
    return _dot(ah, bh) + (_dot(ah, bl) + _dot(al, bh))


def _dotp(a, b):
    return _dotp_split(*_split_bf16(a), b)


def _to_block_major(x, n2):
    bx, L, c = x.shape
    r = L // n2
    return x.reshape(bx, r, n2 // SUBLANES, SUBLANES, c).transpose(0, 2, 1, 3, 4).reshape(
        bx, n2 // SUBLANES, r * SUBLANES, c)


def _from_block_major(y, n2):
    bx, na, rows8, c = y.shape
    r = rows8 // SUBLANES
    return y.reshape(bx, na, r, SUBLANES, c).transpose(0, 2, 1, 3, 4).reshape(bx, r * n2, c)


def _major_kernel(m_ref, x_ref, o_ref):
    mh, ml = _split_bf16(m_ref[...])
    rows_out, rows_in = m_ref.shape
    for a in range(x_ref.shape[1]):
        x2 = x_ref.at[0, a]
        o2 = o_ref.at[0, a]
        for s in range(SUBLANES):
            y = _dotp_split(mh, ml, x2[pl.ds(s, rows_in, stride=SUBLANES), :])
            o2[pl.ds(s, rows_out, stride=SUBLANES), :] = y


def dft_major(mat, x):
    bx, na, rows8, c = x.shape
    m, r = mat.shape
    assert rows8 == r * SUBLANES
    tc = LANES
    ka = DFT_BLOCKS_PER_STEP if na % DFT_BLOCKS_PER_STEP == 0 else 1
    return pl.pallas_call(
        _major_kernel,
        grid=(bx, na // ka, c // tc),
        in_specs=[pl.BlockSpec((m, r), lambda bi, a, j: (0, 0)),
                  pl.BlockSpec((1, ka, rows8, tc), lambda bi, a, j: (bi, a, 0, j))],
        out_specs=pl.BlockSpec((1, ka, m * SUBLANES, tc), lambda bi, a, j: (bi, a, 0, j)),
        out_shape=jax.ShapeDtypeStruct((bx, na, m * SUBLANES, c), F32),
        compiler_params=_cparams("parallel", "parallel", "parallel"),
        name="dft_major",
    )(mat, x)


def _twiddled_block(f_ref, tw_ref, q):
    fr, fi = f_ref[0], f_ref[1]
    twr, twi = tw_ref[q, 0:1, :], tw_ref[q, 1:2, :]
    gr = fr * twr - fi * twi
    gi = fr * twi + fi * twr
    return jnp.concatenate([jnp.concatenate([gr, -gi], axis=1), jnp.concatenate([gi, gr], axis=1)], axis=0)


def _gather_k1(ref, q, n2):
    return jnp.concatenate([ref[0, :, p, q].reshape(n2, -1) for p in range(2)], axis=0)


def _scatter_k1(ref, q, x, n2):
    for p in range(2):
        ref[0, :, p, q] = x[p * n2:(p + 1) * n2].reshape(n2 // SUBLANES, SUBLANES, -1)


def _spec_taps_kernel(f_ref, tw_ref, af_ref, ab_ref, o_ref):
    n2 = f_ref.shape[1]
    for q in range(tw_ref.shape[0]):
        g = _twiddled_block(f_ref, tw_ref, q)
        xf = _dotp(g, _gather_k1(af_ref, q, n2))
        xb = _dotp(g, _gather_k1(ab_ref, q, n2))
        _scatter_k1(o_ref, q, jnp.concatenate([xf[:n2] + xb[:n2], xf[n2:] - xb[n2:]], axis=0), n2)


def _spec_mid_kernel(f_ref, tw_ref, a_ref, h_ref, o_ref):
    n2 = f_ref.shape[1]
    for q in range(tw_ref.shape[0]):
        g = _twiddled_block(f_ref, tw_ref, q)
        x = _dotp(g, _gather_k1(a_ref, q, n2))
        h = _gather_k1(h_ref, q, n2)
        xr, xi = x[:n2], x[n2:]
        hr, hi = h[:n2], h[n2:]
        z = jnp.concatenate([xr * hr - xi * hi, xr * hi + xi * hr], axis=0)
        _scatter_k1(o_ref, q, _dotp(g.T, z), n2)


def spec_stage(fmat, tw, a, h=None, tc=1024):
    bx, na, rows8, ca = a.shape
    n2 = na * SUBLANES
    n1 = rows8 // (2 * SUBLANES)
    c = ca // 2 if h is None else ca
    tc = _row_tile(c, tc)
    nj = c // tc
    kk = DFT_K1_PER_STEP if n1 % DFT_K1_PER_STEP == 0 else 1
    six = lambda t: t.reshape(t.shape[0], na, 2, n1, SUBLANES, t.shape[-1])
    blk = lambda off: pl.BlockSpec((1, na, 2, kk, SUBLANES, tc), lambda k, j, bi: (bi, 0, 0, k, 0, off + j))
    in_specs = [pl.BlockSpec((2, n2, n2), lambda k, j, bi: (0, 0, 0)),
                pl.BlockSpec((kk, 2, n2), lambda k, j, bi: (k, 0, 0)), blk(0)]
    if h is None:
        kern, args = _spec_taps_kernel, [fmat, tw, six(a), six(a)]
        in_specs.append(blk(nj))
    else:
        kern, args = _spec_mid_kernel, [fmat, tw, six(a), six(h)]
        in_specs.append(pl.BlockSpec((1, na, 2, kk, SUBLANES, tc), lambda k, j, bi: (0, 0, 0, k, 0, j)))
    out = pl.pallas_call(
        kern,
        grid=(n1 // kk, nj, bx),
        in_specs=in_specs,
        out_specs=blk(0),
        out_shape=jax.ShapeDtypeStruct((bx, na, 2, n1, SUBLANES, c), F32),
        compiler_params=_cparams("parallel", "parallel", "arbitrary"),
        name="dft_minor",
    )(*args)
    return out.reshape(bx, na, rows8, c)


def _dft_constants(n1, n2):
    n = n1 * n2
    k1 = np.arange(n1, dtype=np.float64)
    ang1 = -2.0 * np.pi * np.outer(k1, k1) / n1
    w1r, w1i = np.cos(ang1), np.sin(ang1)
    k2 = np.arange(n2, dtype=np.float64)
    ang2 = -2.0 * np.pi * np.outer(k2, k2) / n2
    fmat = np.stack([np.cos(ang2), np.sin(ang2)])
    angt = -2.0 * np.pi * np.outer(k1, k2) / n
    tw = np.stack([np.cos(angt), np.sin(angt)], axis=1)
    fwd_half = np.concatenate([w1r[:, :n1 // 2], w1i[:, :n1 // 2]], axis=0)
    inv_half = np.concatenate([w1r[:n1 // 2], w1i[:n1 // 2]], axis=1) / n
    f = lambda a: jnp.asarray(a, F32)
    return f(fwd_half), f(inv_half), f(fmat), f(tw)


def long_conv(v, taps):
    b, L, c = v.shape
    n2 = FFT_N2
    n1 = 2 * L // n2
    fwd_half, inv_half, fmat, tw = _dft_constants(n1, n2)
    hs = spec_stage(fmat, tw, dft_major(fwd_half, _to_block_major(taps[None], n2)))
    a = dft_major(fwd_half, _to_block_major(v, n2))
    y = dft_major(inv_half, spec_stage(fmat, tw, a, hs))
    return _from_block_major(y, n2)


def _dense_conv_kernel(f_ref, g_ref, v_ref, hf_ref, hb_ref, o_ref):
    n = f_ref.shape[0] // 2
    f = f_ref[...]
    x = _dotp(f, v_ref[0])
    hf = _dotp(f, hf_ref[...])
    hb = _dotp(f, hb_ref[...])
    xr, xi = x[:n], x[n:]
    hr = hf[:n] + hb[:n]
    hi = hf[n:] - hb[n:]
    z = jnp.concatenate([xr * hr - xi * hi, xr * hi + xi * hr], axis=0)
    o_ref[0] = _dotp(g_ref[...], z)


def dense_long_conv(v, taps, tc=512):
    b, L, c = v.shape
    n = 2 * L
    ang = 2.0 * np.pi * np.outer(np.arange(n, dtype=np.float64), np.arange(L, dtype=np.float64)) / n
    fmat = jnp.asarray(np.concatenate([np.cos(ang), -np.sin(ang)], axis=0), F32)
    gmat = jnp.asarray(np.concatenate([np.cos(ang.T), -np.sin(ang.T)], axis=1) / n, F32)
    tc = _row_tile(c, tc)
    nj = c // tc
    return pl.pallas_call(
        _dense_conv_kernel,
        grid=(b, nj),
        in_specs=[pl.BlockSpec((2 * n, L), lambda bi, j: (0, 0)), pl.BlockSpec((L, 2 * n), lambda bi, j: (0, 0)),
                  pl.BlockSpec((1, L, tc), lambda bi, j: (bi, 0, j)),
                  pl.BlockSpec((L, tc), lambda bi, j: (0, j)), pl.BlockSpec((L, tc), lambda bi, j: (0, nj + j))],
        out_specs=pl.BlockSpec((1, L, tc), lambda bi, j: (bi, 0, j)),
        out_shape=jax.ShapeDtypeStruct((b, L, c), F32),
        compiler_params=_cparams("parallel", "parallel"),
        name="dense_long_conv",
    )(fmat, gmat, v, taps, taps)


def _pad_mod(m3):
    return jnp.pad(m3, ((0, 0), (0, MOD_ROWS - 3), (0, 0)))


def kernel(x, c, ctx, c_ctx, w_mod, b_mod, norm_g, w_ffn_in, w_ffn_out, attn_w_qkv, attn_w_o, attn_lambda,
           attn_subln_g, hy_w_in, hy_b_in, hy_w_short, hy_b_short, hy_f_w1, hy_f_b1, hy_f_w2, hy_f_b2, hy_f_w3,
           hy_f_b3, hy_f_freq, hy_f_w4, hy_skip, hy_w_out, hy_b_out, cv_w_pw1, cv_b_pw1, cv_w_dw, cv_b_dw,
           cv_ln_g, cv_ln_b, cv_w_pw2, cv_b_pw2, final_g):
    bsz, n_lat, d = x.shape
    n_ctx = ctx.shape[1]
    depth = w_mod.shape[0]
    assert bsz + 1 <= MOD_ROWS

    rows = jnp.concatenate([c, c_ctx[None, :], jnp.zeros((MOD_ROWS - bsz - 1, d), F32)], axis=0)
    table = mod_table(rows, w_mod, b_mod).reshape(depth, MOD_ROWS, N_MOD, d)

    def mods(i, s, latent):
        r = table[i, :bsz] if latent else jnp.broadcast_to(table[i, bsz:bsz + 1], (bsz, N_MOD, d))
        return _pad_mod(r[:, 3 * s:3 * s + 3])

    bf = lambda w: w.astype(BF16)
    xc = ctx
    qkv_buf = None
    tk_lat = 1280 if (n_lat + n_ctx) % 1280 == 0 else n_ctx
    rope = rope_tables(n_lat)

    for i in range(depth):
        kind = i % N_MIXERS
        j = i // N_MIXERS
        last = i == depth - 1
        ctx_in_use = (not last) or kind == 0
        ctx_advance = not last
        w_in0, w_out0 = bf(w_ffn_in[i, 0]), bf(w_ffn_out[i, 0])
        w_in1, w_out1 = bf(w_ffn_in[i, 1]), bf(w_ffn_out[i, 1])

        x = ffn(x, mods(i, 0, True), norm_g[i, 0], w_in0, w_out0)
        if ctx_in_use:
            xc = ffn(xc, mods(i, 0, False), norm_g[i, 0], w_in0, w_out0)

        ml, mc = mods(i, 1, True), mods(i, 1, False)
        if kind == 0:
            lam_init = 0.8 - 0.6 * math.exp(-0.3 * i)
            wqkv = bf(attn_w_qkv[j])
            wo = bf(attn_w_o[j])
            n_all = n_lat + n_ctx
            qkv = jnp.zeros((bsz, n_all, 3 * d), BF16) if qkv_buf is None else qkv_buf
            qkv = modproj(x, ml, norm_g[i, 1], wqkv, rope=rope, rope_blocks=2, out_rows=n_all, into=qkv)
            qkv = modproj(xc, mc, norm_g[i, 1], wqkv, out_rows=n_all, into=qkv, row_offset=n_lat)
            qkv_buf = qkv
            o_l = diff_attention(qkv, n_lat, 0, n_all, 0, attn_lambda[j], attn_subln_g[j], lam_init, tk_lat)
            x = attn_out(o_l, wo, x, ml)
            if ctx_advance:
                o_c = diff_attention(qkv, n_ctx, n_lat, n_ctx, n_lat, attn_lambda[j], attn_subln_g[j], lam_init,
                                     n_ctx)
                yc_fn = lambda xcur: attn_out(o_c, wo, xcur, mc)
        elif kind == 1:
            w_in, w_o = bf(hy_w_in[j]), bf(hy_w_out[j])
            filt = (hy_f_w1[j], hy_f_b1[j], hy_f_w2[j], hy_f_b2[j], hy_f_w3[j], hy_f_b3[j], hy_f_freq[j], hy_f_w4[j])

            def hyena(xs, mod, L):
                x0, vg = hyena_in(xs, mod, norm_g[i, 1], w_in, hy_b_in[j], hy_w_short[j], hy_b_short[j])
                taps = hyena_filter(L, *filt, d)
                y = long_conv(vg, taps) if L == n_lat else dense_long_conv(vg, taps)
                return y, vg, x0

            y_l, vg_l, x0_l = hyena(x, ml, n_lat)
            x = hyena_out(y_l, vg_l, x0_l, hy_skip[j], w_o, hy_b_out[j], x, ml)
            if ctx_advance:
                y_c, vg_c, x0_c = hyena(xc, mc, n_ctx)
                yc_fn = lambda xcur: hyena_out(y_c, vg_c, x0_c, hy_skip[j], w_o, hy_b_out[j], xcur, mc)
        else:
            w1, w2 = bf(cv_w_pw1[j]), bf(cv_w_pw2[j])
            u_l = modglu(x, ml, norm_g[i, 1], w1, cv_b_pw1[j])
            x = conformer_out(u_l, cv_w_dw[j], cv_b_dw[j], cv_ln_g[j], cv_ln_b[j], w2, cv_b_pw2[j], x, ml)
            if ctx_advance:
                u_c = modglu(xc, mc, norm_g[i, 1], w1, cv_b_pw1[j])
                yc_fn = lambda xcur: conformer_out(u_c, cv_w_dw[j], cv_b_dw[j], cv_ln_g[j], cv_ln_b[j], w2,
                                                   cv_b_pw2[j], xcur, mc)

        x = ffn(x, mods(i, 2, True), norm_g[i, 2], w_in1, w_out1, final_g=final_g if last else None)
        if ctx_advance:
            xc = yc_fn(xc)
            xc = ffn(xc, mods(i, 2, False), norm_g[i, 2], w_in1, w_out1)
    return x
```
